```python
import jax, jax.numpy as jnp
from jax import lax
import numpy as np

D_MODEL = 1024
BATCH = 2
SEQ = 8192
DEPTH = 4
DEC_BATCH = 128
DEC_SEQ = 4
PAST_LEN = 2048
PAGE_SIZE = 128

NSA_HEADS = 8
NSA_KV = 2
HEAD_DIM = 64
NSA_HPG = NSA_HEADS // NSA_KV
NSA_WIDTH = NSA_HEADS * HEAD_DIM
ROT_DIM = HEAD_DIM // 4
ROPE_THETA = 500000.0
CMP_BLOCK = 32
CMP_STRIDE = 16
SLC_BLOCK = 64
N_SELECT = 16
WINDOW = 512
Q_BLOCK = 128
RWKV_HEADS = 8
RWKV_HEAD = 64
RWKV_WIDTH = RWKV_HEADS * RWKV_HEAD
RWKV_LORA_W = 64
RWKV_LORA_A = 64
SHIFT_W = 3 * RWKV_WIDTH + RWKV_LORA_W + RWKV_LORA_A
HGRN_HEADS = 4
HGRN_HEAD = 128
HGRN_WIDTH = HGRN_HEADS * HGRN_HEAD
HGRN_CHUNK = 64
N_BRANCH = 3
BRANCH_WIDTH = 512
KV_COLS = 6 * NSA_KV * HEAD_DIM
IN_SIZES = (NSA_WIDTH, KV_COLS, 3 * NSA_HEADS, NSA_WIDTH, SHIFT_W, RWKV_WIDTH,
            HGRN_WIDTH, HGRN_WIDTH, HGRN_WIDTH, HGRN_WIDTH, N_BRANCH * D_MODEL)
N_IN = sum(IN_SIZES)
RMS_EPS = 1e-6
GN_EPS = 64e-5

kernel_name = 'hybrid_nsa_rwkv7_hgrn2_step'


def rmsnorm(x, w):
    x32 = x.astype(jnp.float32)
    y = x32 * lax.rsqrt(jnp.mean(x32 * x32, axis=-1, keepdims=True) + RMS_EPS)
    return (y * w.astype(jnp.float32)).astype(x.dtype)


def partial_rope(x, pos):
    half = ROT_DIM // 2
    inv = ROPE_THETA ** (-jnp.arange(0, ROT_DIM, 2, dtype=jnp.float32) / ROT_DIM)
    ang = pos.astype(jnp.float32)[:, None] * inv[None, :]
    cos = jnp.cos(ang)[None, :, None, :]
    sin = jnp.sin(ang)[None, :, None, :]
    xr = x[..., :ROT_DIM].astype(jnp.float32)
    x1, x2 = xr[..., :half], xr[..., half:]
    rot = jnp.concatenate([x1 * cos - x2 * sin, x2 * cos + x1 * sin], axis=-1)
    return jnp.concatenate([rot.astype(x.dtype), x[..., ROT_DIM:]], axis=-1)


def masked_softmax(s, mask):
    s = jnp.where(mask, s.astype(jnp.float32), -jnp.inf)
    m = jnp.max(s, axis=-1, keepdims=True)
    m = jnp.where(jnp.isfinite(m), m, 0.0)
    e = jnp.exp(s - m)
    return e / jnp.maximum(jnp.sum(e, axis=-1, keepdims=True), 1e-30)


def compress_blocks(rows, pe, w):
    b, tk = rows.shape[:2]
    span = CMP_BLOCK // CMP_STRIDE
    n_sub = tk // CMP_STRIDE
    n_cmp = n_sub - span + 1
    sub = rows[:, :n_sub * CMP_STRIDE].reshape(b, n_sub, CMP_STRIDE, NSA_KV, HEAD_DIM)
    blocks = jnp.concatenate([sub[:, j:j + n_cmp] for j in range(span)], axis=2)
    return jnp.einsum('bnjgd,jde->bnge', blocks + pe[None, None, :, None, :], w)


def slc_cmp_map(n_slc, n_cmp):
    ratio = SLC_BLOCK // CMP_STRIDE
    span = CMP_BLOCK // CMP_STRIDE
    idx = ratio * np.arange(n_slc)[:, None] - (span - 1) + np.arange(ratio + span - 1)[None, :]
    valid = (idx >= 0) & (idx < n_cmp)
    return np.clip(idx, 0, n_cmp - 1), valid


def nsa_attend(q, gates, q_pos0, kv4, win, w0, pe, cw, qb):
    b, tq = q.shape[:2]
    tk = kv4.shape[1]
    scale = HEAD_DIM ** -0.5
    q_rot = partial_rope(q, q_pos0 + jnp.arange(tq))
    kc = compress_blocks(kv4[:, :, 0], pe[0], cw[0])
    vc = compress_blocks(kv4[:, :, 1], pe[1], cw[1])
    n_cmp = kc.shape[1]
    cmp_end = CMP_STRIDE * jnp.arange(n_cmp) + CMP_BLOCK - 1
    n_slc = -(-tk // SLC_BLOCK)
    pad = n_slc * SLC_BLOCK - tk

    def blockify(rows):
        rows = jnp.pad(rows, ((0, 0), (0, pad), (0, 0), (0, 0)))
        return rows.reshape(b, n_slc, SLC_BLOCK, NSA_KV, HEAD_DIM).transpose(0, 3, 1, 2, 4)

    ks_b = blockify(kv4[:, :, 2])
    vs_b = blockify(kv4[:, :, 3])
    idx, valid = slc_cmp_map(n_slc, n_cmp)
    n_sel = min(N_SELECT, n_slc)
    kw, vw = win[:, :, 0], win[:, :, 1]
    bi = jnp.arange(b)[:, None, None, None]
    gi = jnp.arange(NSA_KV)[None, None, :, None]
    blk = jnp.arange(n_slc)
    nb = tq // qb

    def to_blocks(a):
        return a.reshape(b, nb, qb, *a.shape[2:]).swapaxes(0, 1)

    def one_block(args):
        qn, qr, g, q0 = args
        qp = q0 + jnp.arange(qb)
        qn = qn.reshape(b, qb, NSA_KV, NSA_HPG, HEAD_DIM)
        qr = qr.reshape(b, qb, NSA_KV, NSA_HPG, HEAD_DIM)
        s1 = jnp.einsum('bqghd,bngd->bqghn', qn, kc) * scale
        p1 = masked_softmax(s1, (cmp_end[None, :] <= qp[:, None])[None, :, None, None, :])
        o_cmp = jnp.einsum('bqghn,bngd->bqghd', p1.astype(vc.dtype), vc)
        imp = jnp.sum(p1, axis=3)
        imp = jnp.sum(jnp.where(valid, imp[..., idx], 0.0), axis=-1)
        qblk = qp // SLC_BLOCK
        causal = (blk[None, :] * SLC_BLOCK <= qp[:, None])[None, :, None, :]
        forced = ((blk[None, :] == 0) | (blk[None, :] == qblk[:, None]) | (blk[None, :] == qblk[:, None] - 1))[None, :, None, :]
        score = jnp.where(causal, jnp.where(forced, jnp.inf, imp), -jnp.inf)
        top_val, sel = lax.top_k(score, n_sel)
        ks = ks_b[bi, gi, sel]
        vs = vs_b[bi, gi, sel]
        kpos = sel[..., None] * SLC_BLOCK + jnp.arange(SLC_BLOCK)
        m2 = (top_val > -jnp.inf)[..., None] & (kpos <= qp[None, :, None, None, None])
        s2 = jnp.einsum('bqghd,bqgnjd->bqghnj', qr, ks) * scale
        s2 = s2.reshape(b, qb, NSA_KV, NSA_HPG, n_sel * SLC_BLOCK)
        p2 = masked_softmax(s2, m2.reshape(b, qb, NSA_KV, 1, n_sel * SLC_BLOCK))
        p2 = p2.reshape(b, qb, NSA_KV, NSA_HPG, n_sel, SLC_BLOCK)
        o_slc = jnp.einsum('bqghnj,bqgnjd->bqghd', p2.astype(vs.dtype), vs)
        start = q0 - WINDOW - w0
        kwb = lax.dynamic_slice_in_dim(kw, start, WINDOW + qb, axis=1)
        vwb = lax.dynamic_slice_in_dim(vw, start, WINDOW + qb, axis=1)
        kp = q0 - WINDOW + jnp.arange(WINDOW + qb)
        m3 = (kp[None, :] >= 0) & (kp[None, :] <= qp[:, None]) & (qp[:, None] - kp[None, :] < WINDOW)
        s3 = jnp.einsum('bqghd,bkgd->bqghk', qr, kwb) * scale
        p3 = masked_softmax(s3, m3[None, :, None, None, :])
        o_win = jnp.einsum('bqghk,bkgd->bqghd', p3.astype(vwb.dtype), vwb)
        g = g.reshape(b, qb, NSA_KV, NSA_HPG, 3)
        o = g[..., 0:1] * o_cmp + g[..., 1:2] * o_slc + g[..., 2:3] * o_win
        return o.reshape(b, qb, NSA_WIDTH)

    out = lax.map(one_block, (to_blocks(q), to_blocks(q_rot), to_blocks(gates), q_pos0 + qb * jnp.arange(nb)))
    return out.swapaxes(0, 1).reshape(b, tq, NSA_WIDTH)


def rwkv_mix(cols, z, prev, s0, mu, w0, w_up, a0, a_up, k_k, k_a, r_k, ln_w):
    b, t = cols.shape[:2]
    shifted = jnp.concatenate([prev[:, None].astype(cols.dtype), cols[:, :-1]], axis=1)
    xx = (cols + (shifted - cols) * mu).astype(jnp.float32)
    sp = np.cumsum([RWKV_WIDTH, RWKV_WIDTH, RWKV_WIDTH, RWKV_LORA_W])
    r, k, v, wd, ad = jnp.split(xx, sp, axis=-1)
    w = -jax.nn.softplus(-(w0 + jnp.tanh(wd) @ w_up)) - 0.5
    decay = jnp.exp(-jnp.exp(w))
    a = jax.nn.sigmoid(a0 + ad @ a_up)

    def heads(u):
        return u.reshape(b, t, RWKV_HEADS, RWKV_HEAD)

    kk = heads(k * k_k)
    kk = kk / jnp.maximum(jnp.sqrt(jnp.sum(kk * kk, axis=-1, keepdims=True)), 1e-12)
    k = k * (1.0 + (a - 1.0) * k_a)
    r, k, v, decay, a = heads(r), heads(k), heads(v), heads(decay), heads(a)

    def step(s, inp):
        r_t, k_t, v_t, w_t, kk_t, a_t = inp
        sa = jnp.einsum('bhvk,bhk->bhv', s, -kk_t)
        s = s * w_t[:, :, None, :] + sa[..., None] * (kk_t * a_t)[:, :, None, :] + v_t[..., None] * k_t[:, :, None, :]
        return s, jnp.einsum('bhvk,bhk->bhv', s, r_t)

    s_t, o = lax.scan(step, s0.astype(jnp.float32), tuple(u.swapaxes(0, 1) for u in (r, k, v, decay, kk, a)))
    o = o.swapaxes(0, 1)
    mean = jnp.mean(o, axis=-1, keepdims=True)
    var = jnp.mean(jnp.square(o - mean), axis=-1, keepdims=True)
    o = (o - mean) * lax.rsqrt(var + GN_EPS) * ln_w.astype(jnp.float32).reshape(RWKV_HEADS, RWKV_HEAD)
    o = o + jnp.sum(r * k * r_k, axis=-1, keepdims=True) * v
    y = o.reshape(b, t, RWKV_WIDTH) * jax.nn.silu(z.astype(jnp.float32))
    return y.astype(cols.dtype), cols[:, -1], s_t.astype(cols.dtype)


def hgrn_mix(q, fz, i, z, s0, lb, ln_w):
    b, t = q.shape[:2]
    fz = fz.astype(jnp.float32)
    log_f = jnp.logaddexp(jnp.log(lb), jnp.log1p(-lb) + jax.nn.log_sigmoid(fz))
    kf = (1.0 - lb) * jax.nn.sigmoid(-fz)
    chunk = HGRN_CHUNK if t % HGRN_CHUNK == 0 else t
    nc = t // chunk

    def chunks(u):
        return u.astype(jnp.float32).reshape(b, nc, chunk, HGRN_HEADS, HGRN_HEAD).transpose(1, 0, 3, 2, 4)

    mask = jnp.tril(jnp.ones((chunk, chunk), dtype=bool))

    def step(s, inp):
        q_c, lf_c, k_c, i_c = inp
        cl = jnp.cumsum(lf_c, axis=2)
        dec = jnp.exp(jnp.where(mask[:, :, None], cl[:, :, :, None, :] - cl[:, :, None, :, :], -jnp.inf))
        att = jnp.einsum('bhtk,bhtsk,bhsk->bhts', q_c, dec, k_c)
        o = jnp.einsum('bhts,bhsv->bhtv', att, i_c) + jnp.einsum('bhtk,bhkv->bhtv', q_c * jnp.exp(cl), s)
        cl_end = cl[:, :, -1:]
        s = jnp.exp(cl_end[:, :, 0])[..., None] * s + jnp.einsum('bhsk,bhsv->bhkv', k_c * jnp.exp(cl_end - cl), i_c)
        return s, o

    s_t, o = lax.scan(step, s0.astype(jnp.float32), (chunks(q), chunks(log_f), chunks(kf), chunks(i)))
    o = o.transpose(1, 0, 3, 2, 4).reshape(b, t, HGRN_HEADS, HGRN_HEAD)
    o = rmsnorm(o, ln_w.reshape(HGRN_HEADS, HGRN_HEAD))
    y = o.reshape(b, t, HGRN_WIDTH) * jax.nn.silu(z.astype(jnp.float32))
    return y.astype(q.dtype), s_t.astype(q.dtype)


def hybrid_layer(x, pos0, qb, l, past, prm, lb):
    b, t = x.shape[:2]
    h = rmsnorm(x, prm['norm_w'][l])
    proj = h @ prm['w_in'][l]
    (nq, nkv, ngate, nz, rcols, rz, hq, hf, hi, hz, mg) = jnp.split(proj, np.cumsum(np.array(IN_SIZES))[:-1], axis=-1)
    pos = pos0 + jnp.arange(t)
    kv6 = nkv.reshape(b, t, 6, NSA_KV, HEAD_DIM)
    k_slc = partial_rope(kv6[:, :, 2], pos)
    k_win = partial_rope(kv6[:, :, 4], pos)
    kv_new = jnp.stack([kv6[:, :, 0], kv6[:, :, 1], k_slc, kv6[:, :, 3]], axis=2)
    win_new = jnp.stack([k_win, kv6[:, :, 5]], axis=2)
    if past is None:
        kv_all = kv_new
        win_all = jnp.pad(win_new, ((0, 0), (WINDOW, 0), (0, 0), (0, 0), (0, 0)))
        w0 = -WINDOW
        win_state = win_new[:, -min(WINDOW, t):]
        prev = jnp.zeros((b, SHIFT_W), x.dtype)
        s_r = jnp.zeros((b, RWKV_HEADS, RWKV_HEAD, RWKV_HEAD), jnp.float32)
        s_h = jnp.zeros((b, HGRN_HEADS, HGRN_HEAD, HGRN_HEAD), jnp.float32)
    else:
        kv_past, win_buf, s_r, prev, s_h = past
        kv_all = jnp.concatenate([kv_past.astype(kv_new.dtype), kv_new], axis=1)
        wb = win_buf.shape[1]
        win_all = jnp.concatenate([jnp.pad(win_buf.astype(win_new.dtype), ((0, 0), (WINDOW - wb, 0), (0, 0), (0, 0), (0, 0))), win_new], axis=1)
        w0 = pos0 - WINDOW
        win_state = jnp.concatenate([win_buf.astype(win_new.dtype), win_new], axis=1)[:, -wb:]
    q = nq.reshape(b, t, NSA_HEADS, HEAD_DIM)
    gates = jax.nn.sigmoid(ngate).reshape(b, t, NSA_HEADS, 3)
    y_nsa = nsa_attend(q, gates, pos0, kv_all, win_all, w0, prm['nsa_cmp_pe'][l], prm['nsa_cmp_w'][l], qb) * jax.nn.silu(nz)
    y_rwkv, shift_state, s_r = rwkv_mix(rcols, rz, prev, s_r, prm['rwkv_mu'][l], prm['rwkv_w0'][l], prm['rwkv_w_up'][l],
                                        prm['rwkv_a0'][l], prm['rwkv_a_up'][l], prm['rwkv_k_k'][l], prm['rwkv_k_a'][l],
                                        prm['rwkv_r_k'][l], prm['rwkv_ln_w'][l])
    y_hgrn, s_h = hgrn_mix(hq, hf, hi, hz, s_h, lb, prm['hgrn_ln_w'][l])
    branches = jnp.stack([y_nsa, y_rwkv.astype(y_nsa.dtype), y_hgrn.astype(y_nsa.dtype)], axis=2)
    g = jax.nn.sigmoid(mg).reshape(b, t, N_BRANCH, D_MODEL)
    merged = jnp.sum(g * jnp.einsum('btnw,nwd->btnd', branches, prm['w_branch'][l]), axis=2)
    x = x + merged @ prm['w_out'][l]
    return x, (kv_new, win_state, s_r, shift_state, s_h)


def setup_inputs(seed: int = 0) -> dict:
    key = jax.random.key(seed)
    ks = jax.random.split(key, 26)
    n_pages = PAST_LEN // PAGE_SIZE
    n_used = DEC_BATCH * n_pages
    n_pool = n_used + n_used // 4
    w_buf = min(WINDOW, PAST_LEN)
    f32 = jnp.float32

    def nrm(k, shape, s):
        return s * jax.random.normal(k, shape, f32)

    page_table = jax.random.permutation(ks[7], n_pool)[:n_used].reshape(DEC_BATCH, n_pages).astype(jnp.int32)
    return {
        'x_prompt': nrm(ks[0], (BATCH, SEQ, D_MODEL), 1.0),
        'x_sample': nrm(ks[1], (DEC_BATCH, DEC_SEQ, D_MODEL), 1.0),
        'cache_kv': nrm(ks[2], (DEPTH, n_pool, PAGE_SIZE, 4, NSA_KV, HEAD_DIM), 1.0),
        'cache_win': nrm(ks[3], (DEPTH, DEC_BATCH, w_buf, 2, NSA_KV, HEAD_DIM), 1.0),
        'state_rwkv': nrm(ks[4], (DEPTH, DEC_BATCH, RWKV_HEADS, RWKV_HEAD, RWKV_HEAD), 0.5),
        'state_shift': nrm(ks[5], (DEPTH, DEC_BATCH, SHIFT_W), 1.0),
        'state_hgrn': nrm(ks[6], (DEPTH, DEC_BATCH, HGRN_HEADS, HGRN_HEAD, HGRN_HEAD), 0.5),
        'page_table': page_table,
        'norm_w': 1.0 + nrm(ks[8], (DEPTH, D_MODEL), 0.02),
        'w_in': nrm(ks[9], (DEPTH, D_MODEL, N_IN), D_MODEL ** -0.5),
        'nsa_cmp_pe': nrm(ks[10], (DEPTH, 2, CMP_BLOCK, HEAD_DIM), 0.1),
        'nsa_cmp_w': nrm(ks[11], (DEPTH, 2, CMP_BLOCK, HEAD_DIM, HEAD_DIM), (CMP_BLOCK * HEAD_DIM) ** -0.5),
        'rwkv_mu': jax.random.uniform(ks[12], (DEPTH, SHIFT_W), f32),
        'rwkv_w0': -1.0 + nrm(ks[13], (DEPTH, RWKV_WIDTH), 0.5),
        'rwkv_w_up': nrm(ks[14], (DEPTH, RWKV_LORA_W, RWKV_WIDTH), 0.1),
        'rwkv_a0': nrm(ks[15], (DEPTH, RWKV_WIDTH), 0.5),
        'rwkv_a_up': nrm(ks[16], (DEPTH, RWKV_LORA_A, RWKV_WIDTH), 0.5 * RWKV_LORA_A ** -0.5),
        'rwkv_k_k': 0.85 + nrm(ks[17], (DEPTH, RWKV_WIDTH), 0.1),
        'rwkv_k_a': 1.0 + nrm(ks[18], (DEPTH, RWKV_WIDTH), 0.1),
        'rwkv_r_k': nrm(ks[19], (DEPTH, RWKV_HEADS, RWKV_HEAD), 0.1),
        'rwkv_ln_w': 1.0 + nrm(ks[20], (DEPTH, RWKV_WIDTH), 0.02),
        'hgrn_lb_logits': nrm(ks[21], (DEPTH, HGRN_WIDTH), 0.5),
        'hgrn_ln_w': 1.0 + nrm(ks[22], (DEPTH, HGRN_WIDTH), 0.02),
        'w_branch': nrm(ks[23], (DEPTH, N_BRANCH, BRANCH_WIDTH, D_MODEL), BRANCH_WIDTH ** -0.5),
        'w_out': nrm(ks[24], (DEPTH, D_MODEL, D_MODEL), D_MODEL ** -0.5),
        'norm_f': 1.0 + nrm(ks[25], (D_MODEL,), 0.02),
    }


def reference(x_prompt, x_sample, cache_kv, cache_win, state_rwkv, state_shift, state_hgrn, page_table,
              norm_w, w_in, nsa_cmp_pe, nsa_cmp_w, rwkv_mu, rwkv_w0, rwkv_w_up, rwkv_a0, rwkv_a_up,
              rwkv_k_k, rwkv_k_a, rwkv_r_k, rwkv_ln_w, hgrn_lb_logits, hgrn_ln_w, w_branch, w_out, norm_f):
    prm = {'norm_w': norm_w, 'w_in': w_in, 'nsa_cmp_pe': nsa_cmp_pe, 'nsa_cmp_w': nsa_cmp_w,
           'rwkv_mu': rwkv_mu, 'rwkv_w0': rwkv_w0, 'rwkv_w_up': rwkv_w_up, 'rwkv_a0': rwkv_a0,
           'rwkv_a_up': rwkv_a_up, 'rwkv_k_k': rwkv_k_k, 'rwkv_k_a': rwkv_k_a, 'rwkv_r_k': rwkv_r_k,
           'rwkv_ln_w': rwkv_ln_w, 'hgrn_ln_w': hgrn_ln_w, 'w_branch': w_branch, 'w_out': w_out}
    cs = jnp.cumsum(jax.nn.softmax(hgrn_lb_logits.astype(jnp.float32), axis=0), axis=0)
    lbs = cs - cs[0:1]
    n_dec = page_table.shape[0]
    xp, xs = x_prompt, x_sample
    kv_p, kv_s, win_p, win_s, rw_p, rw_s, sh_p, sh_s, hg_p, hg_s = ([] for _ in range(10))
    for l in range(DEPTH):
        xp, st_p = hybrid_layer(xp, 0, Q_BLOCK, l, None, prm, lbs[l])
        kv_past = cache_kv[l][page_table].reshape(n_dec, -1, 4, NSA_KV, HEAD_DIM)
        past = (kv_past, cache_win[l], state_rwkv[l], state_shift[l], state_hgrn[l])
        xs, st_s = hybrid_layer(xs, PAST_LEN, 1, l, past, prm, lbs[l])
        kv_p.append(st_p[0]); win_p.append(st_p[1]); rw_p.append(st_p[2]); sh_p.append(st_p[3]); hg_p.append(st_p[4])
        kv_s.append(st_s[0]); win_s.append(st_s[1]); rw_s.append(st_s[2]); sh_s.append(st_s[3]); hg_s.append(st_s[4])
    y_prompt = rmsnorm(xp, norm_f)
    y_sample = rmsnorm(xs, norm_f)
    return (y_prompt, y_sample, jnp.stack(kv_p), jnp.stack(kv_s), jnp.stack(win_p), jnp.stack(win_s),
            jnp.stack(rw_p), jnp.stack(rw_s), jnp.stack(sh_p), jnp.stack(sh_s), jnp.stack(hg_p), jnp.stack(hg_s))
```

```python
import functools

import jax
import jax.numpy as jnp
import numpy as np
from jax import lax
from jax.experimental import pallas as pl
from jax.experimental.pallas import tpu as pltpu

F32 = jnp.float32
BF16 = jnp.bfloat16
HI = lax.Precision.HIGHEST

D_MODEL = 1024
DEPTH = 4
PAST_LEN = 2048
PAGE_SIZE = 128
NSA_HEADS = 8
NSA_KV = 2
HEAD_DIM = 64
NSA_HPG = NSA_HEADS // NSA_KV
NSA_WIDTH = NSA_HEADS * HEAD_DIM
ROT_DIM = HEAD_DIM // 4
ROPE_THETA = 500000.0
CMP_BLOCK = 32
CMP_STRIDE = 16
SLC_BLOCK = 64
N_SELECT = 16
WINDOW = 512
Q_BLOCK = 128
RWKV_HEADS = 8
RWKV_HEAD = 64
RWKV_WIDTH = RWKV_HEADS * RWKV_HEAD
RWKV_LORA_W = 64
RWKV_LORA_A = 64
SHIFT_W = 3 * RWKV_WIDTH + RWKV_LORA_W + RWKV_LORA_A
HGRN_HEADS = 4
HGRN_HEAD = 128
HGRN_WIDTH = HGRN_HEADS * HGRN_HEAD
HGRN_CHUNK = 64
N_BRANCH = 3
BRANCH_WIDTH = 512
KV_COLS = 6 * NSA_KV * HEAD_DIM
IN_SIZES = (NSA_WIDTH, KV_COLS, 3 * NSA_HEADS, NSA_WIDTH, SHIFT_W, RWKV_WIDTH,
            HGRN_WIDTH, HGRN_WIDTH, HGRN_WIDTH, HGRN_WIDTH, N_BRANCH * D_MODEL)
RMS_EPS = 1e-6
GN_EPS = 64e-5

LANE = 128
VMEM_LIMIT = 56 * 1024 * 1024
NEG = -1e30
RWKV_CHUNK = 64

A_KV = 0
A_GATE = 768
A_Q = 1024
A_NZ = 1536
A_RZ = 2048
A_HQ = 2560
A_HF = 3072
A_HI = 3584
A_HZ = 4096
A_WIDTH = 4608


def _cparams(sem):
    return pltpu.CompilerParams(dimension_semantics=sem, vmem_limit_bytes=VMEM_LIMIT)


def _inproj_body(x_ref, nw_ref, w_ref, o_ref, h_ref):
    @pl.when(pl.program_id(1) == 0)
    def _():
        x = x_ref[...]
        ms = jnp.mean(x * x, axis=-1, keepdims=True)
        h_ref[...] = (x * lax.rsqrt(ms + RMS_EPS) * nw_ref[...]).astype(BF16)

    o_ref[...] = jnp.dot(h_ref[...], w_ref[...], preferred_element_type=F32)


def _inproj(x2d, nw, w, tm, tn):
    m, d = x2d.shape
    n = w.shape[1]
    return pl.pallas_call(
        _inproj_body,
        grid=(m // tm, n // tn),
        in_specs=[pl.BlockSpec((tm, d), lambda i, j: (i, 0)),
                  pl.BlockSpec((1, d), lambda i, j: (0, 0)),
                  pl.BlockSpec((d, tn), lambda i, j: (0, j))],
        out_specs=pl.BlockSpec((tm, tn), lambda i, j: (i, j)),
        out_shape=jax.ShapeDtypeStruct((m, n), F32),
        scratch_shapes=[pltpu.VMEM((tm, d), BF16)],
        compiler_params=_cparams(("arbitrary", "arbitrary")),
        name="inproj",
    )(x2d, nw, w)


def _rmsnorm_body(x_ref, w_ref, o_ref):
    x = x_ref[...]
    ms = jnp.mean(x * x, axis=-1, keepdims=True)
    o_ref[...] = x * lax.rsqrt(ms + RMS_EPS) * w_ref[...]


def _final_norm(x2d, w, tm):
    m, d = x2d.shape
    return pl.pallas_call(
        _rmsnorm_body,
        grid=(m // tm,),
        in_specs=[pl.BlockSpec((tm, d), lambda i: (i, 0)), pl.BlockSpec((1, d), lambda i: (0, 0))],
        out_specs=pl.BlockSpec((tm, d), lambda i: (i, 0)),
        out_shape=jax.ShapeDtypeStruct((m, d), F32),
        compiler_params=_cparams(("arbitrary",)),
        name="final_norm",
    )(x2d, w.reshape(1, d))


def _rope_tables(pos):
    half = ROT_DIM // 2
    inv = ROPE_THETA ** (-jnp.arange(0, ROT_DIM, 2, dtype=F32) / ROT_DIM)
    ang = pos.astype(F32)[:, None] * inv[None, :]
    cos, sin = jnp.cos(ang), jnp.sin(ang)
    n = pos.shape[0]
    ones = jnp.ones((n, HEAD_DIM - ROT_DIM), F32)
    zeros8 = jnp.zeros((n, half), F32)
    zeros = jnp.zeros((n, HEAD_DIM - ROT_DIM), F32)
    c = jnp.concatenate([cos, cos, ones], axis=1)
    s_up = jnp.concatenate([zeros8, sin, zeros], axis=1)
    s_dn = jnp.concatenate([-sin, zeros8, zeros], axis=1)
    tile = lambda a: jnp.concatenate([a] * NSA_KV, axis=1)
    return tile(c), tile(s_up), tile(s_dn), cos.T, sin.T


def _kvpost_body(with_attn, p_ref, c_ref, su_ref, sd_ref, kv_ref, win_ref, *extra):
    x = p_ref[...]
    c, su, sd = c_ref[...], su_ref[...], sd_ref[...]

    def rope(v):
        return v * c + pltpu.roll(v, ROT_DIM // 2, 1) * su + pltpu.roll(v, LANE - ROT_DIM // 2, 1) * sd

    k_slc = rope(x[:, 256:384])
    k_win = rope(x[:, 512:640])
    kv_ref[:, 0:256] = x[:, 0:256]
    kv_ref[:, 256:384] = k_slc
    kv_ref[:, 384:512] = x[:, 384:512]
    win_ref[:, 0:128] = k_win
    win_ref[:, 128:256] = x[:, 640:768]
    if with_attn:
        kc_ref, ks_ref, vst_ref, kw_ref, vwt_ref = extra
        kc_ref[...] = x[:, 0:256].astype(BF16)
        ks_ref[...] = k_slc.astype(BF16)
        vst_ref[0] = x[:, 384:512].T.astype(BF16)
        kw_ref[...] = k_win.astype(BF16)
        vwt_ref[0] = x[:, 640:768].T.astype(BF16)


def _kvpost(proj_a, tabs, b, t, tm, with_attn):
    m = b * t
    nt = tabs[0].shape[0] // tm
    tab_spec = pl.BlockSpec((tm, LANE), lambda i: (i % nt, 0))
    out_shape = [jax.ShapeDtypeStruct((m, 512), F32), jax.ShapeDtypeStruct((m, 256), F32)]
    out_specs = [pl.BlockSpec((tm, 512), lambda i: (i, 0)), pl.BlockSpec((tm, 256), lambda i: (i, 0))]
    if with_attn:
        out_shape += [jax.ShapeDtypeStruct((m, 256), BF16), jax.ShapeDtypeStruct((m, LANE), BF16),
                      jax.ShapeDtypeStruct((b, LANE, t), BF16), jax.ShapeDtypeStruct((m, LANE), BF16),
                      jax.ShapeDtypeStruct((b, LANE, t), BF16)]
        tspec = pl.BlockSpec((1, LANE, tm), lambda i: (i // nt, 0, i % nt))
        out_specs += [pl.BlockSpec((tm, 256), lambda i: (i, 0)), pl.BlockSpec((tm, LANE), lambda i: (i, 0)),
                      tspec, pl.BlockSpec((tm, LANE), lambda i: (i, 0)), tspec]
    return pl.pallas_call(
        functools.partial(_kvpost_body, with_attn),
        grid=(m // tm,),
        in_specs=[pl.BlockSpec((tm, KV_COLS), lambda i: (i, 0)), tab_spec, tab_spec, tab_spec],
        out_specs=out_specs,
        out_shape=out_shape,
        compiler_params=_cparams(("arbitrary",)),
        name="kvpost",
    )(proj_a, *tabs)


def _cmp_weights(pe, cw):
    eye = jnp.eye(NSA_KV, dtype=F32)
    cols = []
    for s in range(2):
        for part in range(2):
            w = cw[s, part * CMP_STRIDE:(part + 1) * CMP_STRIDE]
            blk = jnp.einsum('jde,gh->jgdhe', w, eye).reshape(CMP_STRIDE, LANE, LANE)
            full = jnp.zeros((CMP_STRIDE, 2, LANE, LANE), F32).at[:, s].set(blk)
            cols.append(full.reshape(CMP_STRIDE * 2 * LANE, LANE))
    w_all = jnp.concatenate(cols, axis=1).astype(BF16)
    bias = jnp.einsum('sjd,sjde->se', pe, cw)
    bias = jnp.concatenate([bias[0], bias[0], bias[1], bias[1]]).reshape(1, 2 * LANE)
    return w_all, bias


def _cmp_body(sub_ref, w_ref, b_ref, kc_ref, vct_ref):
    r = jnp.dot(sub_ref[0], w_ref[...], preferred_element_type=F32)
    n = r.shape[0]
    b = b_ref[...]
    kc = r[:, 0:128] + pltpu.roll(r[:, 128:256], n - 1, 0) + b[:, 0:128]
    vc = r[:, 256:384] + pltpu.roll(r[:, 384:512], n - 1, 0) + b[:, 128:256]
    kc_ref[0] = kc.astype(BF16)
    vct_ref[0] = vc.T.astype(BF16)


def _compress(kcsrc, w_all, bias, b, t):
    n_sub = t // CMP_STRIDE
    sub = kcsrc.reshape(b, n_sub, CMP_STRIDE * 256)
    return pl.pallas_call(
        _cmp_body,
        grid=(b,),
        in_specs=[pl.BlockSpec((1, n_sub, CMP_STRIDE * 256), lambda i: (i, 0, 0)),
                  pl.BlockSpec((CMP_STRIDE * 256, 512), lambda i: (0, 0)),
                  pl.BlockSpec((1, 256), lambda i: (0, 0))],
        out_specs=[pl.BlockSpec((1, n_sub, LANE), lambda i: (i, 0, 0)),
                   pl.BlockSpec((1, LANE, n_sub), lambda i: (i, 0, 0))],
        out_shape=[jax.ShapeDtypeStruct((b, n_sub, LANE), BF16), jax.ShapeDtypeStruct((b, LANE, n_sub), BF16)],
        compiler_params=_cparams(("arbitrary",)),
        name="compress",
    )(sub, w_all, bias)


def _slc_cmp_matrix(n_slc, n_cmp, n_cmp_pad):
    ratio = SLC_BLOCK // CMP_STRIDE
    span = CMP_BLOCK // CMP_STRIDE
    m = np.zeros((n_slc, n_cmp_pad), np.float32)
    for j in range(n_slc):
        for k in range(ratio + span - 1):
            n = ratio * j - (span - 1) + k
            if 0 <= n < n_cmp:
                m[j, n] = 1.0
    return m


def _nsa_body(n_cmp_pad, n_slc, q_ref, gt_ref, cos_ref, sin_ref, kc_ref, vct_ref, ks_ref, vst_ref,
              kw_ref, vwt_ref, mt_ref, o_ref, sc_ref, sel_ref):
    i = pl.program_id(1)
    q0 = i * Q_BLOCK
    hq = NSA_HPG * Q_BLOCK
    q = q_ref[0]
    cos4 = jnp.concatenate([cos_ref[...]] * NSA_HPG, axis=1)
    sin4 = jnp.concatenate([sin_ref[...]] * NSA_HPG, axis=1)
    gs = jax.nn.sigmoid(gt_ref[0])
    lane = lax.broadcasted_iota(jnp.int32, (Q_BLOCK, Q_BLOCK), 1)
    sub = lax.broadcasted_iota(jnp.int32, (Q_BLOCK, Q_BLOCK), 0)
    qp = q0 + lane
    zeros_g = jnp.zeros((HEAD_DIM, hq), F32)
    half = ROT_DIM // 2

    for g in range(NSA_KV):
        xa = q[:, g * 256:g * 256 + 128].T
        xb = q[:, g * 256 + 128:g * 256 + 256].T
        qn = jnp.concatenate([xa[0:64], xa[64:128], xb[0:64], xb[64:128]], axis=1) * (HEAD_DIM ** -0.5)
        x1, x2 = qn[0:half], qn[half:ROT_DIM]
        qr = jnp.concatenate([x1 * cos4 - x2 * sin4, x2 * cos4 + x1 * sin4, qn[ROT_DIM:]], axis=0)

        def pad(a):
            parts = [zeros_g] * NSA_KV
            parts[g] = a
            return jnp.concatenate(parts, axis=0).astype(BF16)

        qn_p, qr_p = pad(qn), pad(qr)

        s1 = jnp.dot(kc_ref[0], qn_p, preferred_element_type=F32)
        n_idx = lax.broadcasted_iota(jnp.int32, (n_cmp_pad, Q_BLOCK), 0)
        lane_c = lax.broadcasted_iota(jnp.int32, (n_cmp_pad, Q_BLOCK), 1)
        cmp_ok = (CMP_STRIDE * n_idx + CMP_BLOCK - 1) <= (q0 + lane_c)
        ps = []
        for h in range(NSA_HPG):
            s = jnp.where(cmp_ok, s1[:, h * Q_BLOCK:(h + 1) * Q_BLOCK], NEG)
            m = jnp.max(s, axis=0, keepdims=True)
            e = jnp.where(cmp_ok, jnp.exp(s - m), 0.0)
            d = jnp.maximum(jnp.sum(e, axis=0, keepdims=True), 1e-30)
            ps.append(e / d)
        imp = ps[0] + ps[1] + ps[2] + ps[3]
        p1 = jnp.concatenate(ps, axis=1).astype(BF16)
        o_cmp = jnp.dot(vct_ref[0, g * HEAD_DIM:(g + 1) * HEAD_DIM, :], p1, preferred_element_type=F32)
        imps = jnp.dot(mt_ref[...], imp, preferred_element_type=F32, precision=HI)

        blk = lax.broadcasted_iota(jnp.int32, (n_slc, Q_BLOCK), 0)
        qp_s = q0 + lax.broadcasted_iota(jnp.int32, (n_slc, Q_BLOCK), 1)
        causal = blk * SLC_BLOCK <= qp_s
        qblk = qp_s // SLC_BLOCK
        forced = (blk == 0) | (blk == qblk) | (blk == qblk - 1)
        score = jnp.where(causal, jnp.where(forced, jnp.inf, imps), -jnp.inf)
        sc_ref[...] = score

        def rank_step(j, cnt):
            row = sc_ref[pl.ds(j, 1), :]
            ge = jnp.where(row >= score, 1.0, 0.0)
            gt = jnp.where(row > score, 1.0, 0.0)
            return cnt + jnp.where(blk > j, ge, gt)

        cnt = lax.fori_loop(0, jnp.minimum(2 * i + 2, n_slc), rank_step, jnp.zeros((n_slc, Q_BLOCK), F32))
        sel_ref[...] = jnp.where(cnt < N_SELECT, jnp.where(score > -jnp.inf, 1.0, 0.0), 0.0)

        def attend(k_ref, vt_ref, t, carry, ok):
            m, l, acc = carry
            off = pl.multiple_of(t * Q_BLOCK, Q_BLOCK)
            s = jnp.dot(k_ref[0, pl.ds(off, Q_BLOCK), :], qr_p, preferred_element_type=F32)
            s = jnp.concatenate([jnp.where(ok, s[:, h * Q_BLOCK:(h + 1) * Q_BLOCK], NEG)
                                 for h in range(NSA_HPG)], axis=1)
            m_new = jnp.maximum(m, jnp.max(s, axis=0, keepdims=True))
            alpha = jnp.exp(m - m_new)
            p = jnp.exp(s - m_new)
            l = alpha * l + jnp.sum(p, axis=0, keepdims=True)
            vt = vt_ref[0, g * HEAD_DIM:(g + 1) * HEAD_DIM, pl.ds(off, Q_BLOCK)]
            acc = alpha * acc + jnp.dot(vt, p.astype(BF16), preferred_element_type=F32)
            return m_new, l, acc

        init = (jnp.full((1, hq), NEG, F32), jnp.zeros((1, hq), F32), jnp.zeros((HEAD_DIM, hq), F32))

        def slc_step(t, carry):
            r0 = sel_ref[pl.ds(2 * t, 1), :]
            r1 = sel_ref[pl.ds(2 * t + 1, 1), :]
            selm = jnp.concatenate([jnp.broadcast_to(r0, (SLC_BLOCK, Q_BLOCK)),
                                    jnp.broadcast_to(r1, (SLC_BLOCK, Q_BLOCK))], axis=0)
            ok = jnp.where(t * Q_BLOCK + sub <= qp, selm, 0.0) > 0.5
            return attend(ks_ref, vst_ref, t, carry, ok)

        _, l_s, acc_s = lax.fori_loop(0, i + 1, slc_step, init)
        o_slc = acc_s / l_s

        def win_step(k, carry):
            t = i - k
            kp = t * Q_BLOCK + sub
            ok = jnp.where(kp <= qp, qp - kp, WINDOW) < WINDOW
            return attend(kw_ref, vwt_ref, t, carry, ok)

        _, l_w, acc_w = lax.fori_loop(0, jnp.minimum(i, WINDOW // Q_BLOCK) + 1, win_step, init)
        o_win = acc_w / l_w

        def gate(jj):
            return jnp.concatenate([gs[(g * NSA_HPG + h) * 3 + jj:(g * NSA_HPG + h) * 3 + jj + 1, :]
                                    for h in range(NSA_HPG)], axis=1)

        o_t = gate(0) * o_cmp + gate(1) * o_slc + gate(2) * o_win
        ya = jnp.concatenate([o_t[:, 0:128], o_t[:, 128:256]], axis=0).T
        yb = jnp.concatenate([o_t[:, 256:384], o_t[:, 384:512]], axis=0).T
        o_ref[0, :, g * 256:g * 256 + 128] = ya
        o_ref[0, :, g * 256 + 128:g * 256 + 256] = yb


def _nsa_prompt(proj_a, gate_t, cos_t, sin_t, kc, vct, ksb, vst, kwb, vwt, b, t):
    nb = t // Q_BLOCK
    n_sub = t // CMP_STRIDE
    n_slc = t // SLC_BLOCK
    mt = jnp.asarray(_slc_cmp_matrix(n_slc, n_sub - 1, n_sub))
    seq = lambda w: pl.BlockSpec((1, t, w), lambda bi, i: (bi, 0, 0))
    seq_t = pl.BlockSpec((1, LANE, t), lambda bi, i: (bi, 0, 0))
    return pl.pallas_call(
        functools.partial(_nsa_body, n_sub, n_slc),
        grid=(b, nb),
        in_specs=[pl.BlockSpec((1, Q_BLOCK, NSA_WIDTH), lambda bi, i: (bi, i, A_Q // NSA_WIDTH)),
                  pl.BlockSpec((1, 3 * NSA_HEADS, Q_BLOCK), lambda bi, i: (bi, 0, i)),
                  pl.BlockSpec((ROT_DIM // 2, Q_BLOCK), lambda bi, i: (0, i)),
                  pl.BlockSpec((ROT_DIM // 2, Q_BLOCK), lambda bi, i: (0, i)),
                  pl.BlockSpec((1, n_sub, LANE), lambda bi, i: (bi, 0, 0)),
                  pl.BlockSpec((1, LANE, n_sub), lambda bi, i: (bi, 0, 0)),
                  seq(LANE), seq_t, seq(LANE), seq_t,
                  pl.BlockSpec((n_slc, n_sub), lambda bi, i: (0, 0))],
        out_specs=pl.BlockSpec((1, Q_BLOCK, NSA_WIDTH), lambda bi, i: (bi, i, 0)),
        out_shape=jax.ShapeDtypeStruct((b, t, NSA_WIDTH), F32),
        scratch_shapes=[pltpu.VMEM((n_slc, Q_BLOCK), F32), pltpu.VMEM((n_slc, Q_BLOCK), F32)],
        compiler_params=_cparams(("arbitrary", "arbitrary")),
        name="nsa_prompt",
    )(proj_a.reshape(b, t, A_WIDTH), gate_t, cos_t, sin_t, kc, vct,
      ksb.reshape(b, t, LANE), vst, kwb.reshape(b, t, LANE), vwt, mt)


def _masked_softmax(s, mask):
    s = jnp.where(mask, s, -jnp.inf)
    m = jnp.max(s, axis=-1, keepdims=True)
    m = jnp.where(jnp.isfinite(m), m, 0.0)
    e = jnp.exp(s - m)
    return e / jnp.maximum(jnp.sum(e, axis=-1, keepdims=True), 1e-30)


def _rope_rows(x, pos):
    half = ROT_DIM // 2
    inv = ROPE_THETA ** (-jnp.arange(0, ROT_DIM, 2, dtype=F32) / ROT_DIM)
    ang = pos.astype(F32)[:, None] * inv[None, :]
    cos = jnp.cos(ang)[None, :, None, :]
    sin = jnp.sin(ang)[None, :, None, :]
    x1, x2 = x[..., :half], x[..., half:ROT_DIM]
    return jnp.concatenate([x1 * cos - x2 * sin, x2 * cos + x1 * sin, x[..., ROT_DIM:]], axis=-1)


def _nsa_decode(q, gates, kv_all, win_all, pe, cw):
    b, tq = q.shape[:2]
    tk = kv_all.shape[1]
    scale = HEAD_DIM ** -0.5
    qpos = PAST_LEN + jnp.arange(tq)
    qn = q.reshape(b, tq, NSA_KV, NSA_HPG, HEAD_DIM)
    qr = _rope_rows(q, qpos).reshape(b, tq, NSA_KV, NSA_HPG, HEAD_DIM)
    n_sub = tk // CMP_STRIDE
    n_cmp = n_sub - 1

    def compress(rows, s):
        sub = rows[:, :n_sub * CMP_STRIDE].reshape(b, n_sub, CMP_STRIDE, NSA_KV, HEAD_DIM)
        a = jnp.einsum('bnjgd,jde->bnge', sub, cw[s, :CMP_STRIDE])
        c = jnp.einsum('bnjgd,jde->bnge', sub, cw[s, CMP_STRIDE:])
        bias = jnp.einsum('jd,jde->e', pe[s], cw[s])
        return a[:, :n_cmp] + c[:, 1:] + bias

    kc = compress(kv_all[:, :, 0], 0)
    vc = compress(kv_all[:, :, 1], 1)
    cmp_end = CMP_STRIDE * jnp.arange(n_cmp) + CMP_BLOCK - 1
    s1 = jnp.einsum('bqghd,bngd->bqghn', qn, kc) * scale
    p1 = _masked_softmax(s1, (cmp_end[None, :] <= qpos[:, None])[None, :, None, None, :])
    o_cmp = jnp.einsum('bqghn,bngd->bqghd', p1, vc)
    n_slc = -(-tk // SLC_BLOCK)
    mt = jnp.asarray(_slc_cmp_matrix(n_slc, n_cmp, n_cmp))
    imp = jnp.einsum('bqgn,jn->bqgj', jnp.sum(p1, axis=3), mt, precision=HI)
    blk = jnp.arange(n_slc)
    qblk = qpos // SLC_BLOCK
    causal = (blk[None, :] * SLC_BLOCK <= qpos[:, None])[None, :, None, :]
    forced = ((blk[None, :] == 0) | (blk[None, :] == qblk[:, None]) | (blk[None, :] == qblk[:, None] - 1))[None, :, None, :]
    score = jnp.where(causal, jnp.where(forced, jnp.inf, imp), -jnp.inf)
    ahead = (score[..., None, :] > score[..., :, None]) | (
        (score[..., None, :] == score[..., :, None]) & (blk[None, :] < blk[:, None]))
    rank = jnp.sum(ahead, axis=-1)
    sel = (rank < N_SELECT) & (score > -jnp.inf)
    kpos = jnp.arange(tk)
    key_ok = sel[..., kpos // SLC_BLOCK] & (kpos[None, :] <= qpos[:, None])[None, :, None, :]
    s2 = jnp.einsum('bqghd,bkgd->bqghk', qr, kv_all[:, :, 2]) * scale
    p2 = _masked_softmax(s2, key_ok[:, :, :, None, :])
    o_slc = jnp.einsum('bqghk,bkgd->bqghd', p2, kv_all[:, :, 3])
    wpos = PAST_LEN - WINDOW + jnp.arange(win_all.shape[1])
    m3 = (wpos[None, :] <= qpos[:, None]) & (qpos[:, None] - wpos[None, :] < WINDOW)
    s3 = jnp.einsum('bqghd,bkgd->bqghk', qr, win_all[:, :, 0]) * scale
    p3 = _masked_softmax(s3, m3[None, :, None, None, :])
    o_win = jnp.einsum('bqghk,bkgd->bqghd', p3, win_all[:, :, 1])
    g = gates.reshape(b, tq, NSA_KV, NSA_HPG, 3)
    o = g[..., 0:1] * o_cmp + g[..., 1:2] * o_slc + g[..., 2:3] * o_win
    return o.reshape(b, tq, NSA_WIDTH)


def _rwkv_mix(cols, z, prev, s0, mu, w0, w_up, a0, a_up, k_k, k_a, r_k, ln_w):
    b, t = cols.shape[:2]
    shifted = jnp.concatenate([prev[:, None], cols[:, :-1]], axis=1)
    xx = cols + (shifted - cols) * mu
    sp = np.cumsum([RWKV_WIDTH, RWKV_WIDTH, RWKV_WIDTH, RWKV_LORA_W])
    r, k, v, wd, ad = jnp.split(xx, sp, axis=-1)
    w = -jax.nn.softplus(-(w0 + jnp.dot(jnp.tanh(wd), w_up, precision=HI))) - 0.5
    logw = -jnp.exp(w)
    a = jax.nn.sigmoid(a0 + jnp.dot(ad, a_up, precision=HI))

    def heads(u):
        return u.reshape(b, t, RWKV_HEADS, RWKV_HEAD)

    kk = heads(k * k_k)
    kk = kk / jnp.maximum(jnp.sqrt(jnp.sum(kk * kk, axis=-1, keepdims=True)), 1e-12)
    k = k * (1.0 + (a - 1.0) * k_a)
    r, k, v, logw, a = heads(r), heads(k), heads(v), heads(logw), heads(a)
    c = min(RWKV_CHUNK, t)
    nc = t // c

    def chunks(u):
        return u.reshape(b, nc, c, RWKV_HEADS, RWKV_HEAD).transpose(1, 0, 3, 2, 4)

    r, k, v, logw, kk, a = (chunks(u) for u in (r, k, v, logw, kk, a))
    cum = jnp.cumsum(logw, axis=3)
    w_in = jnp.exp(cum)
    w_ex = jnp.exp(cum - logw)
    w_end = w_in[..., -1:, :]
    alpha_t = -kk * w_ex
    beta_h = kk * a / w_in
    k_h = k / w_in
    r_t = r * w_in
    tri_s = jnp.tril(jnp.ones((c, c), F32), -1)
    tri_i = jnp.tril(jnp.ones((c, c), F32))
    mm = functools.partial(jnp.einsum, precision=HI)
    a_ab = mm('nbhtk,nbhsk->nbhts', alpha_t, beta_h) * tri_s
    a_ak = mm('nbhtk,nbhsk->nbhts', alpha_t, k_h) * tri_s
    a_rb = mm('nbhtk,nbhsk->nbhts', r_t, beta_h) * tri_i
    a_rk = mm('nbhtk,nbhsk->nbhts', r_t, k_h) * tri_i
    eye = jnp.eye(c, dtype=F32)
    tm = eye + a_ab
    pw = a_ab
    steps = 1
    while steps * 2 < c:
        pw = mm('nbhts,nbhsu->nbhtu', pw, pw)
        tm = mm('nbhts,nbhsu->nbhtu', tm, eye + pw)
        steps *= 2
    p_m = mm('nbhts,nbhsk->nbhtk', tm, alpha_t)
    q_m = mm('nbhts,nbhsv->nbhtv', tm, mm('nbhts,nbhsv->nbhtv', a_ak, v))
    b_t = beta_h * w_end
    k_t = k_h * w_end
    m_m = mm('nbhtk,nbhtj->nbhkj', p_m, b_t)
    n_m = mm('nbhtv,nbhtk->nbhvk', q_m, b_t) + mm('nbhtv,nbhtk->nbhvk', v, k_t)
    r_p = r_t + mm('nbhts,nbhsk->nbhtk', a_rb, p_m)
    o_p = mm('nbhts,nbhsv->nbhtv', a_rb, q_m) + mm('nbhts,nbhsv->nbhtv', a_rk, v)

    def step(s, inp):
        m_c, n_c, we_c, rp_c, op_c = inp
        o = mm('bhtk,bhvk->bhtv', rp_c, s) + op_c
        s = s * we_c + mm('bhvk,bhkj->bhvj', s, m_c) + n_c
        return s, o

    s_t, o = lax.scan(step, s0, (m_m, n_m, w_end, r_p, o_p))
    o = o.transpose(1, 0, 3, 2, 4).reshape(b, t, RWKV_HEADS, RWKV_HEAD)
    r, k, v = (u.transpose(1, 0, 3, 2, 4).reshape(b, t, RWKV_HEADS, RWKV_HEAD) for u in (r, k, v))
    mean = jnp.mean(o, axis=-1, keepdims=True)
    var = jnp.mean(jnp.square(o - mean), axis=-1, keepdims=True)
    o = (o - mean) * lax.rsqrt(var + GN_EPS) * ln_w.reshape(RWKV_HEADS, RWKV_HEAD)
    o = o + jnp.sum(r * k * r_k, axis=-1, keepdims=True) * v
    y = o.reshape(b, t, RWKV_WIDTH) * jax.nn.silu(z)
    return y, cols[:, -1], s_t


def _hgrn_mix(q, fz, i, z, s0, lb, ln_w):
    b, t = q.shape[:2]
    log_f = jnp.logaddexp(jnp.log(lb), jnp.log1p(-lb) + jax.nn.log_sigmoid(fz))
    kf = (1.0 - lb) * jax.nn.sigmoid(-fz)
    chunk = HGRN_CHUNK if t % HGRN_CHUNK == 0 else t
    nc = t // chunk

    def chunks(u):
        return u.reshape(b, nc, chunk, HGRN_HEADS, HGRN_HEAD).transpose(1, 0, 3, 2, 4)

    mask = jnp.tril(jnp.ones((chunk, chunk), dtype=bool))

    def step(s, inp):
        q_c, lf_c, k_c, i_c = inp
        cl = jnp.cumsum(lf_c, axis=2)
        dec = jnp.exp(jnp.where(mask[:, :, None], cl[:, :, :, None, :] - cl[:, :, None, :, :], -jnp.inf))
        att = jnp.einsum('bhtk,bhtsk,bhsk->bhts', q_c, dec, k_c)
        o = jnp.einsum('bhts,bhsv->bhtv', att, i_c) + jnp.einsum('bhtk,bhkv->bhtv', q_c * jnp.exp(cl), s)
        cl_end = cl[:, :, -1:]
        s = jnp.exp(cl_end[:, :, 0])[..., None] * s + jnp.einsum('bhsk,bhsv->bhkv', k_c * jnp.exp(cl_end - cl), i_c)
        return s, o

    s_t, o = lax.scan(step, s0, (chunks(q), chunks(log_f), chunks(kf), chunks(i)))
    o = o.transpose(1, 0, 3, 2, 4).reshape(b, t, HGRN_HEADS, HGRN_HEAD)
    o = o * lax.rsqrt(jnp.mean(o * o, axis=-1, keepdims=True) + RMS_EPS) * ln_w.reshape(HGRN_HEADS, HGRN_HEAD)
    y = o.reshape(b, t, HGRN_WIDTH) * jax.nn.silu(z)
    return y, s_t


def _merge_body(x_ref, yn_ref, nz_ref, yr_ref, yh_ref, mg_ref, wb_ref, wo_ref, o_ref):
    nz = nz_ref[...]
    branches = (yn_ref[...] * (nz * jax.nn.sigmoid(nz)), yr_ref[...], yh_ref[...])
    acc = jnp.zeros(o_ref.shape, F32)
    for n, y in enumerate(branches):
        t = jnp.dot(y.astype(BF16), wb_ref[n], preferred_element_type=F32)
        acc = acc + jax.nn.sigmoid(mg_ref[:, n * D_MODEL:(n + 1) * D_MODEL]) * t
    o_ref[...] = x_ref[...] + jnp.dot(acc.astype(BF16), wo_ref[...], preferred_element_type=F32)


def _merge(x2d, y_nsa, proj_a, y_rwkv, y_hgrn, mg, wb, wo, tm):
    m = x2d.shape[0]
    row = lambda w: pl.BlockSpec((tm, w), lambda i: (i, 0))
    return pl.pallas_call(
        _merge_body,
        grid=(m // tm,),
        in_specs=[row(D_MODEL), row(BRANCH_WIDTH),
                  pl.BlockSpec((tm, BRANCH_WIDTH), lambda i: (i, A_NZ // BRANCH_WIDTH)),
                  row(BRANCH_WIDTH), row(BRANCH_WIDTH), row(N_BRANCH * D_MODEL),
                  pl.BlockSpec((N_BRANCH, BRANCH_WIDTH, D_MODEL), lambda i: (0, 0, 0)),
                  pl.BlockSpec((D_MODEL, D_MODEL), lambda i: (0, 0))],
        out_specs=row(D_MODEL),
        out_shape=jax.ShapeDtypeStruct((m, D_MODEL), F32),
        compiler_params=_cparams(("arbitrary",)),
        name="merge",
    )(x2d, y_nsa, proj_a, y_rwkv, y_hgrn, mg, wb, wo)


def _split_w_in(w):
    o = np.concatenate([[0], np.cumsum(IN_SIZES)])
    seg = lambda n: w[:, o[n]:o[n + 1]]
    pad = jnp.zeros((w.shape[0], A_Q - A_GATE - IN_SIZES[2]), w.dtype)
    wa = jnp.concatenate([seg(1), seg(2), pad, seg(0), seg(3), seg(5), seg(6), seg(7), seg(8), seg(9)], axis=1)
    return wa.astype(BF16), seg(4).astype(BF16), seg(10).astype(BF16)


def _layer(x, l, past, prm, lb, tabs, tm):
    b, t = x.shape[:2]
    m = b * t
    x2d = x.reshape(m, D_MODEL)
    nw = prm['norm_w'][l].reshape(1, D_MODEL)
    wa, wb_cols, wc = prm['w_split'][l]
    proj_a = _inproj(x2d, nw, wa, tm, 1536)
    rcols = _inproj(x2d, nw, wb_cols, tm, SHIFT_W).reshape(b, t, SHIFT_W)
    mg = _inproj(x2d, nw, wc, tm, 1536)
    seg = lambda off, w: proj_a[:, off:off + w].reshape(b, t, w)
    c_tab, su_tab, sd_tab, cos_t, sin_t = tabs
    if past is None:
        kv_new, win_new, kcsrc, ksb, vst, kwb, vwt = _kvpost(proj_a, (c_tab, su_tab, sd_tab), b, t, min(tm, t), True)
        w_all, bias = _cmp_weights(prm['nsa_cmp_pe'][l], prm['nsa_cmp_w'][l])
        kc, vct = _compress(kcsrc, w_all, bias, b, t)
        gate_t = seg(A_GATE, 3 * NSA_HEADS).transpose(0, 2, 1)
        y_nsa = _nsa_prompt(proj_a, gate_t, cos_t, sin_t, kc, vct, ksb, vst, kwb, vwt, b, t)
        win_state = win_new.reshape(b, t, 2, NSA_KV, HEAD_DIM)[:, -min(WINDOW, t):]
        prev = jnp.zeros((b, SHIFT_W), F32)
        s_r = jnp.zeros((b, RWKV_HEADS, RWKV_HEAD, RWKV_HEAD), F32)
        s_h = jnp.zeros((b, HGRN_HEADS, HGRN_HEAD, HGRN_HEAD), F32)
    else:
        kv_past, win_buf, s_r, prev, s_h = past
        kv_new, win_new = _kvpost(proj_a, (c_tab, su_tab, sd_tab), b, t, tm, False)
        kv5 = kv_new.reshape(b, t, 4, NSA_KV, HEAD_DIM)
        win5 = win_new.reshape(b, t, 2, NSA_KV, HEAD_DIM)
        kv_all = jnp.concatenate([kv_past, kv5], axis=1)
        wbuf = win_buf.shape[1]
        win_all = jnp.concatenate([jnp.pad(win_buf, ((0, 0), (WINDOW - wbuf, 0), (0, 0), (0, 0), (0, 0))), win5], axis=1)
        win_state = jnp.concatenate([win_buf, win5], axis=1)[:, -wbuf:]
        gates = jax.nn.sigmoid(seg(A_GATE, 3 * NSA_HEADS)).reshape(b, t, NSA_HEADS, 3)
        q = seg(A_Q, NSA_WIDTH).reshape(b, t, NSA_HEADS, HEAD_DIM)
        y_nsa = _nsa_decode(q, gates, kv_all, win_all, prm['nsa_cmp_pe'][l], prm['nsa_cmp_w'][l])
    y_rwkv, shift_state, s_r = _rwkv_mix(rcols, seg(A_RZ, RWKV_WIDTH), prev, s_r, prm['rwkv_mu'][l], prm['rwkv_w0'][l],
                                         prm['rwkv_w_up'][l], prm['rwkv_a0'][l], prm['rwkv_a_up'][l], prm['rwkv_k_k'][l],
                                         prm['rwkv_k_a'][l], prm['rwkv_r_k'][l], prm['rwkv_ln_w'][l])
    y_hgrn, s_h = _hgrn_mix(seg(A_HQ, HGRN_WIDTH), seg(A_HF, HGRN_WIDTH), seg(A_HI, HGRN_WIDTH), seg(A_HZ, HGRN_WIDTH),
                            s_h, lb, prm['hgrn_ln_w'][l])
    x_new = _merge(x2d, y_nsa.reshape(m, NSA_WIDTH), proj_a, y_rwkv.reshape(m, RWKV_WIDTH), y_hgrn.reshape(m, HGRN_WIDTH),
                   mg, prm['w_branch_bf16'][l], prm['w_out_bf16'][l], min(tm, 512))
    kv_state = kv_new.reshape(b, t, 4, NSA_KV, HEAD_DIM)
    return x_new.reshape(b, t, D_MODEL), (kv_state, win_state, s_r, shift_state, s_h)


def kernel(x_prompt, x_sample, cache_kv, cache_win, state_rwkv, state_shift, state_hgrn, page_table,
           norm_w, w_in, nsa_cmp_pe, nsa_cmp_w, rwkv_mu, rwkv_w0, rwkv_w_up, rwkv_a0, rwkv_a_up,
           rwkv_k_k, rwkv_k_a, rwkv_r_k, rwkv_ln_w, hgrn_lb_logits, hgrn_ln_w, w_branch, w_out, norm_f):
    prm = {'norm_w': norm_w, 'nsa_cmp_pe': nsa_cmp_pe, 'nsa_cmp_w': nsa_cmp_w,
           'rwkv_mu': rwkv_mu, 'rwkv_w0': rwkv_w0, 'rwkv_w_up': rwkv_w_up, 'rwkv_a0': rwkv_a0,
           'rwkv_a_up': rwkv_a_up, 'rwkv_k_k': rwkv_k_k, 'rwkv_k_a': rwkv_k_a, 'rwkv_r_k': rwkv_r_k,
           'rwkv_ln_w': rwkv_ln_w, 'hgrn_ln_w': hgrn_ln_w,
           'w_split': [_split_w_in(w_in[l]) for l in range(DEPTH)],
           'w_branch_bf16': w_branch.astype(BF16), 'w_out_bf16': w_out.astype(BF16)}
    cs = jnp.cumsum(jax.nn.softmax(hgrn_lb_logits, axis=0), axis=0)
    lbs = cs - cs[0:1]
    n_dec, dec_t = x_sample.shape[:2]
    bp, tp = x_prompt.shape[:2]
    tabs_p = _rope_tables(jnp.arange(tp))
    tabs_s = _rope_tables(PAST_LEN + jnp.arange(n_dec * dec_t) % dec_t)
    tm_p = min(1024, bp * tp)
    tm_s = n_dec * dec_t
    xp, xs = x_prompt, x_sample
    outs = [[] for _ in range(10)]
    for l in range(DEPTH):
        xp, st_p = _layer(xp, l, None, prm, lbs[l], tabs_p, tm_p)
        pages = jnp.take(cache_kv[l].reshape(cache_kv.shape[1], -1), page_table.reshape(-1), axis=0)
        kv_past = pages.reshape(n_dec, -1, 4, NSA_KV, HEAD_DIM)
        past = (kv_past, cache_win[l], state_rwkv[l], state_shift[l], state_hgrn[l])
        xs, st_s = _layer(xs, l, past, prm, lbs[l], tabs_s, tm_s)
        for n in range(5):
            outs[2 * n].append(st_p[n])
            outs[2 * n + 1].append(st_s[n])
    y_prompt = _final_norm(xp.reshape(bp * tp, D_MODEL), norm_f, tm_p).reshape(xp.shape)
    y_sample = _final_norm(xs.reshape(n_dec * dec_t, D_MODEL), norm_f, tm_s).reshape(xs.shape)
    return (y_prompt, y_sample) + tuple(jnp.stack(o) for o in outs)
```

```python
import functools

import jax
import jax.numpy as jnp
import numpy as np
from jax import lax
from jax.experimental import pallas as pl
from jax.experimental.pallas import tpu as pltpu

F32 = jnp.float32
BF16 = jnp.bfloat16
HI = lax.Precision.HIGHEST

D_MODEL = 1024
DEPTH = 4
PAST_LEN = 2048
PAGE_SIZE = 128
NSA_HEADS = 8
NSA_KV = 2
HEAD_DIM = 64
NSA_HPG = NSA_HEADS // NSA_KV
NSA_WIDTH = NSA_HEADS * HEAD_DIM
ROT_DIM = HEAD_DIM // 4
ROPE_THETA = 500000.0
CMP_BLOCK = 32
CMP_STRIDE = 16
SLC_BLOCK = 64
N_SELECT = 16
WINDOW = 512
Q_BLOCK = 128
RWKV_HEADS = 8
RWKV_HEAD = 64
RWKV_WIDTH = RWKV_HEADS * RWKV_HEAD
RWKV_LORA_W = 64
RWKV_LORA_A = 64
SHIFT_W = 3 * RWKV_WIDTH + RWKV_LORA_W + RWKV_LORA_A
HGRN_HEADS = 4
HGRN_HEAD = 128
HGRN_WIDTH = HGRN_HEADS * HGRN_HEAD
HGRN_CHUNK = 64
N_BRANCH = 3
BRANCH_WIDTH = 512
KV_COLS = 6 * NSA_KV * HEAD_DIM
IN_SIZES = (NSA_WIDTH, KV_COLS, 3 * NSA_HEADS, NSA_WIDTH, SHIFT_W, RWKV_WIDTH,
            HGRN_WIDTH, HGRN_WIDTH, HGRN_WIDTH, HGRN_WIDTH, N_BRANCH * D_MODEL)
RMS_EPS = 1e-6
GN_EPS = 64e-5

LANE = 128
VMEM_LIMIT = 56 * 1024 * 1024
NEG = -1e30
KEY_TILE = 256
ATT_UNROLL = 2
LOG2E = 1.4426950408889634
RWKV_CHUNK = 64

A_KV = 0
A_GATE = 768
A_Q = 1024
A_NZ = 1536
A_RZ = 2048
A_HQ = 2560
A_HF = 3072
A_HI = 3584
A_HZ = 4096
A_WIDTH = 4608


def _cparams(sem):
    return pltpu.CompilerParams(dimension_semantics=sem, vmem_limit_bytes=VMEM_LIMIT)


def _inproj_body(x_ref, nw_ref, w_ref, o_ref, h_ref):
    @pl.when(pl.program_id(1) == 0)
    def _():
        x = x_ref[...]
        ms = jnp.mean(x * x, axis=-1, keepdims=True)
        h_ref[...] = (x * lax.rsqrt(ms + RMS_EPS) * nw_ref[...]).astype(BF16)

    o_ref[...] = jnp.dot(h_ref[...], w_ref[...], preferred_element_type=F32)


def _inproj(x2d, nw, w, tm, tn):
    m, d = x2d.shape
    n = w.shape[1]
    return pl.pallas_call(
        _inproj_body,
        grid=(m // tm, n // tn),
        in_specs=[pl.BlockSpec((tm, d), lambda i, j: (i, 0)),
                  pl.BlockSpec((1, d), lambda i, j: (0, 0)),
                  pl.BlockSpec((d, tn), lambda i, j: (0, j))],
        out_specs=pl.BlockSpec((tm, tn), lambda i, j: (i, j)),
        out_shape=jax.ShapeDtypeStruct((m, n), F32),
        scratch_shapes=[pltpu.VMEM((tm, d), BF16)],
        compiler_params=_cparams(("arbitrary", "arbitrary")),
        name="inproj",
    )(x2d, nw, w)


def _rmsnorm_body(x_ref, w_ref, o_ref):
    x = x_ref[...]
    ms = jnp.mean(x * x, axis=-1, keepdims=True)
    o_ref[...] = x * lax.rsqrt(ms + RMS_EPS) * w_ref[...]


def _final_norm(x2d, w, tm):
    m, d = x2d.shape
    return pl.pallas_call(
        _rmsnorm_body,
        grid=(m // tm,),
        in_specs=[pl.BlockSpec((tm, d), lambda i: (i, 0)), pl.BlockSpec((1, d), lambda i: (0, 0))],
        out_specs=pl.BlockSpec((tm, d), lambda i: (i, 0)),
        out_shape=jax.ShapeDtypeStruct((m, d), F32),
        compiler_params=_cparams(("arbitrary",)),
        name="final_norm",
    )(x2d, w.reshape(1, d))


def _rope_tables(pos):
    half = ROT_DIM // 2
    inv = ROPE_THETA ** (-jnp.arange(0, ROT_DIM, 2, dtype=F32) / ROT_DIM)
    ang = pos.astype(F32)[:, None] * inv[None, :]
    cos, sin = jnp.cos(ang), jnp.sin(ang)
    n = pos.shape[0]
    ones = jnp.ones((n, HEAD_DIM - ROT_DIM), F32)
    zeros8 = jnp.zeros((n, half), F32)
    zeros = jnp.zeros((n, HEAD_DIM - ROT_DIM), F32)
    c = jnp.concatenate([cos, cos, ones], axis=1)
    s_up = jnp.concatenate([zeros8, sin, zeros], axis=1)
    s_dn = jnp.concatenate([-sin, zeros8, zeros], axis=1)
    tile = lambda a: jnp.concatenate([a] * NSA_KV, axis=1)
    return tile(c), tile(s_up), tile(s_dn), cos.T, sin.T


def _kvpost_body(with_attn, p_ref, c_ref, su_ref, sd_ref, kv_ref, win_ref, *extra):
    x = p_ref[...]
    c, su, sd = c_ref[...], su_ref[...], sd_ref[...]

    def rope(v):
        return v * c + pltpu.roll(v, ROT_DIM // 2, 1) * su + pltpu.roll(v, LANE - ROT_DIM // 2, 1) * sd

    k_slc = rope(x[:, 256:384])
    k_win = rope(x[:, 512:640])
    kv_ref[:, 0:256] = x[:, 0:256]
    kv_ref[:, 256:384] = k_slc
    kv_ref[:, 384:512] = x[:, 384:512]
    win_ref[:, 0:128] = k_win
    win_ref[:, 128:256] = x[:, 640:768]
    if with_attn:
        kc_ref, ks_ref, vst_ref, kw_ref, vwt_ref = extra
        kc_ref[...] = x[:, 0:256].astype(BF16)
        ks_ref[...] = k_slc.astype(BF16)
        vst_ref[0] = x[:, 384:512].T.astype(BF16)
        kw_ref[...] = k_win.astype(BF16)
        vwt_ref[0] = x[:, 640:768].T.astype(BF16)


def _kvpost(proj_a, tabs, b, t, tm, with_attn):
    m = b * t
    nt = tabs[0].shape[0] // tm
    tab_spec = pl.BlockSpec((tm, LANE), lambda i: (i % nt, 0))
    out_shape = [jax.ShapeDtypeStruct((m, 512), F32), jax.ShapeDtypeStruct((m, 256), F32)]
    out_specs = [pl.BlockSpec((tm, 512), lambda i: (i, 0)), pl.BlockSpec((tm, 256), lambda i: (i, 0))]
    if with_attn:
        out_shape += [jax.ShapeDtypeStruct((m, 256), BF16), jax.ShapeDtypeStruct((m, LANE), BF16),
                      jax.ShapeDtypeStruct((b, LANE, t), BF16), jax.ShapeDtypeStruct((m, LANE), BF16),
                      jax.ShapeDtypeStruct((b, LANE, t), BF16)]
        tspec = pl.BlockSpec((1, LANE, tm), lambda i: (i // nt, 0, i % nt))
        out_specs += [pl.BlockSpec((tm, 256), lambda i: (i, 0)), pl.BlockSpec((tm, LANE), lambda i: (i, 0)),
                      tspec, pl.BlockSpec((tm, LANE), lambda i: (i, 0)), tspec]
    return pl.pallas_call(
        functools.partial(_kvpost_body, with_attn),
        grid=(m // tm,),
        in_specs=[pl.BlockSpec((tm, KV_COLS), lambda i: (i, 0)), tab_spec, tab_spec, tab_spec],
        out_specs=out_specs,
        out_shape=out_shape,
        compiler_params=_cparams(("arbitrary",)),
        name="kvpost",
    )(proj_a, *tabs)


def _cmp_weights(pe, cw):
    eye = jnp.eye(NSA_KV, dtype=F32)
    cols = []
    for s in range(2):
        for part in range(2):
            w = cw[s, part * CMP_STRIDE:(part + 1) * CMP_STRIDE]
            blk = jnp.einsum('jde,gh->jgdhe', w, eye).reshape(CMP_STRIDE, LANE, LANE)
            full = jnp.zeros((CMP_STRIDE, 2, LANE, LANE), F32).at[:, s].set(blk)
            cols.append(full.reshape(CMP_STRIDE * 2 * LANE, LANE))
    w_all = jnp.concatenate(cols, axis=1).astype(BF16)
    bias = jnp.einsum('sjd,sjde->se', pe, cw)
    bias = jnp.concatenate([bias[0], bias[0], bias[1], bias[1]]).reshape(1, 2 * LANE)
    return w_all, bias


def _cmp_body(sub_ref, w_ref, b_ref, kc_ref, vct_ref):
    r = jnp.dot(sub_ref[0], w_ref[...], preferred_element_type=F32)
    n = r.shape[0]
    b = b_ref[...]
    kc = r[:, 0:128] + pltpu.roll(r[:, 128:256], n - 1, 0) + b[:, 0:128]
    vc = r[:, 256:384] + pltpu.roll(r[:, 384:512], n - 1, 0) + b[:, 128:256]
    kc_ref[0] = kc.astype(BF16)
    vct_ref[0] = vc.T.astype(BF16)


def _compress(kcsrc, w_all, bias, b, t):
    n_sub = t // CMP_STRIDE
    sub = kcsrc.reshape(b, n_sub, CMP_STRIDE * 256)
    return pl.pallas_call(
        _cmp_body,
        grid=(b,),
        in_specs=[pl.BlockSpec((1, n_sub, CMP_STRIDE * 256), lambda i: (i, 0, 0)),
                  pl.BlockSpec((CMP_STRIDE * 256, 512), lambda i: (0, 0)),
                  pl.BlockSpec((1, 256), lambda i: (0, 0))],
        out_specs=[pl.BlockSpec((1, n_sub, LANE), lambda i: (i, 0, 0)),
                   pl.BlockSpec((1, LANE, n_sub), lambda i: (i, 0, 0))],
        out_shape=[jax.ShapeDtypeStruct((b, n_sub, LANE), BF16), jax.ShapeDtypeStruct((b, LANE, n_sub), BF16)],
        compiler_params=_cparams(("arbitrary",)),
        name="compress",
    )(sub, w_all, bias)


def _slc_cmp_matrix(n_slc, n_cmp, n_cmp_pad):
    ratio = SLC_BLOCK // CMP_STRIDE
    span = CMP_BLOCK // CMP_STRIDE
    m = np.zeros((n_slc, n_cmp_pad), np.float32)
    for j in range(n_slc):
        for k in range(ratio + span - 1):
            n = ratio * j - (span - 1) + k
            if 0 <= n < n_cmp:
                m[j, n] = 1.0
    return m


def _nsa_body(n_cmp_pad, n_slc, q_ref, gt_ref, cos_ref, sin_ref, kc_ref, vct_ref, ks_ref, vst_ref,
              kw_ref, vwt_ref, mt_ref, o_ref, sc_ref, sel_ref, ml_ref, acc_ref):
    i = pl.program_id(1)
    q0 = i * Q_BLOCK
    hq = NSA_HPG * Q_BLOCK
    q = q_ref[0]
    cos4 = jnp.concatenate([cos_ref[...]] * NSA_HPG, axis=1)
    sin4 = jnp.concatenate([sin_ref[...]] * NSA_HPG, axis=1)
    gs = jax.nn.sigmoid(gt_ref[0])
    lane = lax.broadcasted_iota(jnp.int32, (Q_BLOCK, Q_BLOCK), 1)
    sub = lax.broadcasted_iota(jnp.int32, (Q_BLOCK, Q_BLOCK), 0)
    sub_k = lax.broadcasted_iota(jnp.int32, (KEY_TILE, Q_BLOCK), 0)
    qp_k = q0 + lax.broadcasted_iota(jnp.int32, (KEY_TILE, Q_BLOCK), 1)
    tpk = KEY_TILE // Q_BLOCK
    bpt = KEY_TILE // SLC_BLOCK
    zeros_g = jnp.zeros((HEAD_DIM, hq), F32)
    half = ROT_DIM // 2

    for g in range(NSA_KV):
        xa = q[:, g * 256:g * 256 + 128].T
        xb = q[:, g * 256 + 128:g * 256 + 256].T
        qn = jnp.concatenate([xa[0:64], xa[64:128], xb[0:64], xb[64:128]], axis=1) * (HEAD_DIM ** -0.5 * LOG2E)
        x1, x2 = qn[0:half], qn[half:ROT_DIM]
        qr = jnp.concatenate([x1 * cos4 - x2 * sin4, x2 * cos4 + x1 * sin4, qn[ROT_DIM:]], axis=0)

        def pad(a):
            parts = [zeros_g] * NSA_KV
            parts[g] = a
            return jnp.concatenate(parts, axis=0).astype(BF16)

        qn_p, qr_p = pad(qn), pad(qr)

        s1 = jnp.dot(kc_ref[0], qn_p, preferred_element_type=F32)
        n_idx = lax.broadcasted_iota(jnp.int32, (n_cmp_pad, Q_BLOCK), 0)
        lane_c = lax.broadcasted_iota(jnp.int32, (n_cmp_pad, Q_BLOCK), 1)
        cmp_ok = (CMP_STRIDE * n_idx + CMP_BLOCK - 1) <= (q0 + lane_c)
        ps = []
        for h in range(NSA_HPG):
            s = jnp.where(cmp_ok, s1[:, h * Q_BLOCK:(h + 1) * Q_BLOCK], NEG)
            m = jnp.max(s, axis=0, keepdims=True)
            e = jnp.where(cmp_ok, jnp.exp2(s - m), 0.0)
            d = jnp.maximum(jnp.sum(e, axis=0, keepdims=True), 1e-30)
            ps.append(e / d)
        imp = ps[0] + ps[1] + ps[2] + ps[3]
        p1 = jnp.concatenate(ps, axis=1).astype(BF16)
        o_cmp = jnp.dot(vct_ref[0, g * HEAD_DIM:(g + 1) * HEAD_DIM, :], p1, preferred_element_type=F32)
        imps = jnp.dot(mt_ref[...], imp, preferred_element_type=F32, precision=HI)

        blk = lax.broadcasted_iota(jnp.int32, (n_slc, Q_BLOCK), 0)
        qp_s = q0 + lax.broadcasted_iota(jnp.int32, (n_slc, Q_BLOCK), 1)
        causal = blk * SLC_BLOCK <= qp_s
        qblk = qp_s // SLC_BLOCK
        forced = (blk == 0) | (blk == qblk) | (blk == qblk - 1)
        score = jnp.where(causal, jnp.where(forced, jnp.inf, imps), -jnp.inf)
        sc_ref[...] = score

        def rank_step(j, cnt):
            row = sc_ref[pl.ds(j, 1), :]
            ge = jnp.where(row >= score, 1.0, 0.0)
            gt = jnp.where(row > score, 1.0, 0.0)
            return cnt + jnp.where(blk > j, ge, gt)

        cnt = lax.fori_loop(0, jnp.minimum(2 * i + 2, n_slc), rank_step, jnp.zeros((n_slc, Q_BLOCK), F32))
        sel_ref[...] = jnp.where(cnt < N_SELECT, jnp.where(score > -jnp.inf, 1.0, 0.0), 0.0)

        init = (jnp.full((1, hq), NEG, F32), jnp.zeros((1, hq), F32), jnp.zeros((HEAD_DIM, hq), F32))

        def attend(k_ref, vt_ref, t, ok, state):
            m, l, acc = state
            off = pl.multiple_of(t * KEY_TILE, KEY_TILE)
            s = jnp.dot(k_ref[0, pl.ds(off, KEY_TILE), :], qr_p, preferred_element_type=F32)
            ps, alphas, ms, ls = [], [], [], []
            for h in range(NSA_HPG):
                hs = slice(h * Q_BLOCK, (h + 1) * Q_BLOCK)
                sh = jnp.where(ok, s[:, hs], NEG)
                m_new = jnp.maximum(m[:, hs], jnp.max(sh, axis=0, keepdims=True))
                alpha = jnp.exp2(m[:, hs] - m_new)
                p = jnp.exp2(sh - m_new)
                ms.append(m_new)
                ls.append(alpha * l[:, hs] + jnp.sum(p, axis=0, keepdims=True))
                ps.append(p.astype(BF16))
                alphas.append(alpha)
            vt = vt_ref[0, g * HEAD_DIM:(g + 1) * HEAD_DIM, pl.ds(off, KEY_TILE)]
            pv = jnp.dot(vt, jnp.concatenate(ps, axis=1), preferred_element_type=F32)
            cat = lambda parts: jnp.concatenate(parts, axis=1)
            return cat(ms), cat(ls), acc * cat(alphas) + pv

        def slc_step(u, carry):
            state = (ml_ref[0:1, :], ml_ref[1:2, :], acc_ref[...])
            for j in range(ATT_UNROLL):
                t = u * ATT_UNROLL + j
                rows = [jnp.broadcast_to(sel_ref[pl.ds(bpt * t + jj, 1), :], (SLC_BLOCK, Q_BLOCK))
                        for jj in range(bpt)]
                ok = jnp.where(t * KEY_TILE + sub_k <= qp_k, jnp.concatenate(rows, axis=0), 0.0) > 0.5
                state = attend(ks_ref, vst_ref, t, ok, state)
            ml_ref[0:1, :], ml_ref[1:2, :], acc_ref[...] = state
            return carry

        ml_ref[0:1, :], ml_ref[1:2, :], acc_ref[...] = init
        keys_per_step = KEY_TILE * ATT_UNROLL
        lax.fori_loop(0, (q0 + Q_BLOCK + keys_per_step - 1) // keys_per_step, slc_step, 0)
        o_slc = acc_ref[...] / ml_ref[1:2, :]

        state = init
        for k in range(WINDOW // KEY_TILE + 1):
            t = i // tpk - k
            kp = jnp.maximum(t, 0) * KEY_TILE + sub_k
            ok = jnp.where(kp <= qp_k, qp_k - kp, WINDOW) < jnp.where(t >= 0, WINDOW, 0)
            state = attend(kw_ref, vwt_ref, jnp.maximum(t, 0), ok, state)
        o_win = state[2] / state[1]

        def gate(jj):
            return jnp.concatenate([gs[(g * NSA_HPG + h) * 3 + jj:(g * NSA_HPG + h) * 3 + jj + 1, :]
                                    for h in range(NSA_HPG)], axis=1)

        o_t = gate(0) * o_cmp + gate(1) * o_slc + gate(2) * o_win
        ya = jnp.concatenate([o_t[:, 0:128], o_t[:, 128:256]], axis=0).T
        yb = jnp.concatenate([o_t[:, 256:384], o_t[:, 384:512]], axis=0).T
        o_ref[0, :, g * 256:g * 256 + 128] = ya
        o_ref[0, :, g * 256 + 128:g * 256 + 256] = yb


def _nsa_prompt(proj_a, gate_t, cos_t, sin_t, kc, vct, ksb, vst, kwb, vwt, b, t):
    nb = t // Q_BLOCK
    n_sub = t // CMP_STRIDE
    n_slc = t // SLC_BLOCK
    mt = jnp.asarray(_slc_cmp_matrix(n_slc, n_sub - 1, n_sub))
    seq = lambda w: pl.BlockSpec((1, t, w), lambda bi, i: (bi, 0, 0))
    seq_t = pl.BlockSpec((1, LANE, t), lambda bi, i: (bi, 0, 0))
    return pl.pallas_call(
        functools.partial(_nsa_body, n_sub, n_slc),
        grid=(b, nb),
        in_specs=[pl.BlockSpec((1, Q_BLOCK, NSA_WIDTH), lambda bi, i: (bi, i, A_Q // NSA_WIDTH)),
                  pl.BlockSpec((1, 3 * NSA_HEADS, Q_BLOCK), lambda bi, i: (bi, 0, i)),
                  pl.BlockSpec((ROT_DIM // 2, Q_BLOCK), lambda bi, i: (0, i)),
                  pl.BlockSpec((ROT_DIM // 2, Q_BLOCK), lambda bi, i: (0, i)),
                  pl.BlockSpec((1, n_sub, LANE), lambda bi, i: (bi, 0, 0)),
                  pl.BlockSpec((1, LANE, n_sub), lambda bi, i: (bi, 0, 0)),
                  seq(LANE), seq_t, seq(LANE), seq_t,
                  pl.BlockSpec((n_slc, n_sub), lambda bi, i: (0, 0))],
        out_specs=pl.BlockSpec((1, Q_BLOCK, NSA_WIDTH), lambda bi, i: (bi, i, 0)),
        out_shape=jax.ShapeDtypeStruct((b, t, NSA_WIDTH), F32),
        scratch_shapes=[pltpu.VMEM((n_slc, Q_BLOCK), F32), pltpu.VMEM((n_slc, Q_BLOCK), F32),
                        pltpu.VMEM((8, NSA_HPG * Q_BLOCK), F32), pltpu.VMEM((HEAD_DIM, NSA_HPG * Q_BLOCK), F32)],
        compiler_params=_cparams(("arbitrary", "arbitrary")),
        name="nsa_prompt",
    )(proj_a.reshape(b, t, A_WIDTH), gate_t, cos_t, sin_t, kc, vct,
      ksb.reshape(b, t, LANE), vst, kwb.reshape(b, t, LANE), vwt, mt)


def _masked_softmax(s, mask):
    s = jnp.where(mask, s, -jnp.inf)
    m = jnp.max(s, axis=-1, keepdims=True)
    m = jnp.where(jnp.isfinite(m), m, 0.0)
    e = jnp.exp(s - m)
    return e / jnp.maximum(jnp.sum(e, axis=-1, keepdims=True), 1e-30)


def _rope_rows(x, pos):
    half = ROT_DIM // 2
    inv = ROPE_THETA ** (-jnp.arange(0, ROT_DIM, 2, dtype=F32) / ROT_DIM)
    ang = pos.astype(F32)[:, None] * inv[None, :]
    cos = jnp.cos(ang)[None, :, None, :]
    sin = jnp.sin(ang)[None, :, None, :]
    x1, x2 = x[..., :half], x[..., half:ROT_DIM]
    return jnp.concatenate([x1 * cos - x2 * sin, x2 * cos + x1 * sin, x[..., ROT_DIM:]], axis=-1)


def _nsa_decode(q, gates, kv_all, win_all, pe, cw):
    b, tq = q.shape[:2]
    tk = kv_all.shape[1]
    scale = HEAD_DIM ** -0.5
    qpos = PAST_LEN + jnp.arange(tq)
    qn = q.reshape(b, tq, NSA_KV, NSA_HPG, HEAD_DIM)
    qr = _rope_rows(q, qpos).reshape(b, tq, NSA_KV, NSA_HPG, HEAD_DIM)
    n_sub = tk // CMP_STRIDE
    n_cmp = n_sub - 1

    def compress(rows, s):
        sub = rows[:, :n_sub * CMP_STRIDE].reshape(b, n_sub, CMP_STRIDE, NSA_KV, HEAD_DIM)
        a = jnp.einsum('bnjgd,jde->bnge', sub, cw[s, :CMP_STRIDE])
        c = jnp.einsum('bnjgd,jde->bnge', sub, cw[s, CMP_STRIDE:])
        bias = jnp.einsum('jd,jde->e', pe[s], cw[s])
        return a[:, :n_cmp] + c[:, 1:] + bias

    kc = compress(kv_all[:, :, 0], 0)
    vc = compress(kv_all[:, :, 1], 1)
    cmp_end = CMP_STRIDE * jnp.arange(n_cmp) + CMP_BLOCK - 1
    s1 = jnp.einsum('bqghd,bngd->bqghn', qn, kc) * scale
    p1 = _masked_softmax(s1, (cmp_end[None, :] <= qpos[:, None])[None, :, None, None, :])
    o_cmp = jnp.einsum('bqghn,bngd->bqghd', p1, vc)
    n_slc = -(-tk // SLC_BLOCK)
    mt = jnp.asarray(_slc_cmp_matrix(n_slc, n_cmp, n_cmp))
    imp = jnp.einsum('bqgn,jn->bqgj', jnp.sum(p1, axis=3), mt, precision=HI)
    blk = jnp.arange(n_slc)
    qblk = qpos // SLC_BLOCK
    causal = (blk[None, :] * SLC_BLOCK <= qpos[:, None])[None, :, None, :]
    forced = ((blk[None, :] == 0) | (blk[None, :] == qblk[:, None]) | (blk[None, :] == qblk[:, None] - 1))[None, :, None, :]
    score = jnp.where(causal, jnp.where(forced, jnp.inf, imp), -jnp.inf)
    ahead = (score[..., None, :] > score[..., :, None]) | (
        (score[..., None, :] == score[..., :, None]) & (blk[None, :] < blk[:, None]))
    rank = jnp.sum(ahead, axis=-1)
    sel = (rank < N_SELECT) & (score > -jnp.inf)
    kpos = jnp.arange(tk)
    key_ok = sel[..., kpos // SLC_BLOCK] & (kpos[None, :] <= qpos[:, None])[None, :, None, :]
    s2 = jnp.einsum('bqghd,bkgd->bqghk', qr, kv_all[:, :, 2]) * scale
    p2 = _masked_softmax(s2, key_ok[:, :, :, None, :])
    o_slc = jnp.einsum('bqghk,bkgd->bqghd', p2, kv_all[:, :, 3])
    wpos = PAST_LEN - WINDOW + jnp.arange(win_all.shape[1])
    m3 = (wpos[None, :] <= qpos[:, None]) & (qpos[:, None] - wpos[None, :] < WINDOW)
    s3 = jnp.einsum('bqghd,bkgd->bqghk', qr, win_all[:, :, 0]) * scale
    p3 = _masked_softmax(s3, m3[None, :, None, None, :])
    o_win = jnp.einsum('bqghk,bkgd->bqghd', p3, win_all[:, :, 1])
    g = gates.reshape(b, tq, NSA_KV, NSA_HPG, 3)
    o = g[..., 0:1] * o_cmp + g[..., 1:2] * o_slc + g[..., 2:3] * o_win
    return o.reshape(b, tq, NSA_WIDTH)


N_PAGES = PAST_LEN // PAGE_SIZE
DEC_ROWS = 8
DEC_QROWS = NSA_HEADS * 4


def _nsa_dec_body(dec_t, pt_ref, *refs):
    pages = refs[:N_PAGES]
    kvn_ref, cw_ref, wn_ref, qn_ref, qr_ref, g_ref, w_ref, b_ref, mt_ref, o_ref, ck_ref, cv_ref = refs[N_PAGES:]
    del pt_ref
    qn, qr = qn_ref[0], qr_ref[0]
    nq = qn.shape[0]
    lane = lax.broadcasted_iota(jnp.int32, (nq, LANE), 1)
    qi = lax.broadcasted_iota(jnp.int32, (nq, LANE), 0) % dec_t
    qpos = PAST_LEN + qi
    n_sub = PAST_LEN // CMP_STRIDE

    for p in range(N_PAGES):
        ck_ref[p * PAGE_SIZE:(p + 1) * PAGE_SIZE, :] = pages[p][0, 0, :, 0:128]
        cv_ref[p * PAGE_SIZE:(p + 1) * PAGE_SIZE, :] = pages[p][0, 0, :, 128:256]
    parts = []
    for j in range(CMP_STRIDE):
        parts += [ck_ref[pl.ds(j, n_sub, stride=CMP_STRIDE), :], cv_ref[pl.ds(j, n_sub, stride=CMP_STRIDE), :]]
    r = _mm(jnp.concatenate(parts, axis=1), w_ref[...])
    bias = b_ref[...]
    kc = r[:, 0:128] + pltpu.roll(r[:, 128:256], n_sub - 1, 0) + bias[:, 0:128]
    vc = r[:, 256:384] + pltpu.roll(r[:, 384:512], n_sub - 1, 0) + bias[:, 128:256]

    def softmax_rows(scores, oks):
        scores = [jnp.where(ok, s, NEG) for s, ok in zip(scores, oks)]
        m = functools.reduce(jnp.maximum, [jnp.max(s, axis=1, keepdims=True) for s in scores])
        es = [jnp.where(ok, jnp.exp(s - m), 0.0) for s, ok in zip(scores, oks)]
        d = jnp.maximum(functools.reduce(jnp.add, [jnp.sum(e, axis=1, keepdims=True) for e in es]), 1e-30)
        return [e / d for e in es]

    cmp_ok = (CMP_STRIDE * lane + CMP_BLOCK - 1) <= qpos
    (p1,) = softmax_rows([_mm_nt(qn, kc)], [cmp_ok])
    o_cmp = _mm(p1, vc)
    gq = NSA_KV * dec_t
    imp = functools.reduce(jnp.add, [p1[h * gq:(h + 1) * gq] for h in range(NSA_HPG)])
    imps = _mm_hi(imp, mt_ref[...])
    blk = lane[0:gq]
    qp8 = qpos[0:gq]
    qblk = qp8 // SLC_BLOCK
    causal = blk * SLC_BLOCK <= qp8
    forced = (blk == 0) | (blk == qblk) | (blk == qblk - 1)
    score = jnp.where(causal, jnp.where(forced, jnp.inf, imps), -jnp.inf)
    n_slc = -(-(PAST_LEN + dec_t) // SLC_BLOCK)
    cnt = jnp.zeros((gq, LANE), F32)
    for j in range(n_slc):
        col = score[:, j:j + 1]
        cnt = cnt + jnp.where(blk > j, jnp.where(col >= score, 1.0, 0.0), jnp.where(col > score, 1.0, 0.0))
    sel = jnp.where(cnt < N_SELECT, jnp.where(score > -jnp.inf, 1.0, 0.0), 0.0)

    def sel_rows(j0):
        pick = jnp.where(lane[0:gq] < SLC_BLOCK, sel[:, j0:j0 + 1], sel[:, j0 + 1:j0 + 2])
        return jnp.concatenate([pick] * NSA_HPG, axis=0) > 0.5

    zeros_k = jnp.zeros((PAGE_SIZE - DEC_ROWS, LANE), F32)
    new_ok = (lane < dec_t) & (lane <= qi)
    kvn = kvn_ref[0]
    s_list = [_mm_nt(qr, pages[p][0, 0, :, 256:384]) for p in range(N_PAGES)]
    s_list.append(_mm_nt(qr, jnp.concatenate([kvn[:, 256:384], zeros_k], axis=0)))
    ok_list = [sel_rows(2 * p) for p in range(N_PAGES)]
    own = jnp.concatenate([sel[:, 2 * N_PAGES:2 * N_PAGES + 1]] * NSA_HPG, axis=0) > 0.5
    ok_list.append(new_ok & own)
    p_list = softmax_rows(s_list, ok_list)
    o_slc = functools.reduce(jnp.add, [_mm(p_list[p], pages[p][0, 0, :, 384:512]) for p in range(N_PAGES)])
    o_slc = o_slc + _mm(p_list[N_PAGES], jnp.concatenate([kvn[:, 384:512], zeros_k], axis=0))

    n_wt = WINDOW // PAGE_SIZE
    wn = wn_ref[0]
    s_list, ok_list = [], []
    for t in range(n_wt):
        s_list.append(_mm_nt(qr, cw_ref[0, 0, t * PAGE_SIZE:(t + 1) * PAGE_SIZE, 0:128]))
        ok_list.append(qpos - (PAST_LEN - WINDOW + t * PAGE_SIZE + lane) < WINDOW)
    s_list.append(_mm_nt(qr, jnp.concatenate([wn[:, 0:128], zeros_k], axis=0)))
    ok_list.append(new_ok)
    p_list = softmax_rows(s_list, ok_list)
    o_win = functools.reduce(jnp.add, [_mm(p_list[t], cw_ref[0, 0, t * PAGE_SIZE:(t + 1) * PAGE_SIZE, 128:256])
                                       for t in range(n_wt)])
    o_win = o_win + _mm(p_list[n_wt], jnp.concatenate([wn[:, 128:256], zeros_k], axis=0))
    g = g_ref[0]
    o_ref[0] = g[:, 0:128] * o_cmp + g[:, 128:256] * o_slc + g[:, 256:384] * o_win


def _nsa_decode_pallas(l, q, gates, cache_kv, page_table, kv_new, cache_win, win_new, pe, cw):
    b, tq = q.shape[:2]
    scale = HEAD_DIM ** -0.5
    eye = jnp.eye(NSA_KV, dtype=F32)

    def rows(x):
        x5 = x.reshape(b, tq, NSA_KV, NSA_HPG, HEAD_DIM).transpose(0, 3, 2, 1, 4)
        return jnp.einsum('bhgqd,gk->bhgqkd', x5, eye).reshape(b, DEC_QROWS, LANE).astype(BF16)

    qn = rows(q * scale)
    qr = rows(_rope_rows(q, PAST_LEN + jnp.arange(tq)) * scale)
    g5 = gates.reshape(b, tq, NSA_KV, NSA_HPG, 3).transpose(0, 3, 2, 1, 4)
    lane_g = jnp.repeat(eye, HEAD_DIM, axis=1)
    gate_b = jnp.einsum('bhgqj,gn->bhgqjn', g5, lane_g).reshape(b, DEC_QROWS, 3 * LANE)
    w_all, bias = _cmp_weights(pe, cw)
    n_cmp = PAST_LEN // CMP_STRIDE - 1
    n_slc = -(-(PAST_LEN + tq) // SLC_BLOCK)
    mt = np.zeros((LANE, LANE), np.float32)
    mt[:n_cmp, :n_slc] = _slc_cmp_matrix(n_slc, n_cmp, n_cmp).T
    pad_rows = lambda a: jnp.pad(a, ((0, 0), (0, DEC_ROWS - tq), (0, 0)))
    n_pool = cache_kv.shape[1]
    ckv = cache_kv.reshape(DEPTH, n_pool, PAGE_SIZE, 512)
    page_spec = lambda p: pl.BlockSpec((1, 1, PAGE_SIZE, 512), lambda bi, pt: (l, pt[bi, p], 0, 0))
    per_seq = lambda r, w: pl.BlockSpec((1, r, w), lambda bi, pt: (bi, 0, 0))
    const = lambda shape: pl.BlockSpec(shape, lambda bi, pt: (0,) * len(shape))
    grid_spec = pltpu.PrefetchScalarGridSpec(
        num_scalar_prefetch=1,
        grid=(b,),
        in_specs=[page_spec(p) for p in range(N_PAGES)]
                 + [per_seq(DEC_ROWS, 512),
                    pl.BlockSpec((1, 1, WINDOW, 256), lambda bi, pt: (l, bi, 0, 0)),
                    per_seq(DEC_ROWS, 256), per_seq(DEC_QROWS, LANE), per_seq(DEC_QROWS, LANE),
                    per_seq(DEC_QROWS, 3 * LANE), const((CMP_STRIDE * 256, 512)), const((1, 256)), const((LANE, LANE))],
        out_specs=per_seq(DEC_QROWS, LANE),
        scratch_shapes=[pltpu.VMEM((PAST_LEN, LANE), F32), pltpu.VMEM((PAST_LEN, LANE), F32)],
    )
    o = pl.pallas_call(
        functools.partial(_nsa_dec_body, tq),
        grid_spec=grid_spec,
        out_shape=jax.ShapeDtypeStruct((b, DEC_QROWS, LANE), F32),
        compiler_params=_cparams(("arbitrary",)),
        name="nsa_decode",
    )(page_table, *([ckv] * N_PAGES), pad_rows(kv_new), cache_win.reshape(DEPTH, b, WINDOW, 256), pad_rows(win_new),
      qn, qr, gate_b, w_all, bias, jnp.asarray(mt))
    o6 = o.reshape(b, NSA_HPG, NSA_KV, tq, NSA_KV, HEAD_DIM)
    return jnp.einsum('bhgqkd,gk->bqghd', o6, eye).reshape(b, tq, NSA_WIDTH)


def _rwkv_mix(cols, z, prev, s0, mu, w0, w_up, a0, a_up, k_k, k_a, r_k, ln_w):
    b, t = cols.shape[:2]
    shifted = jnp.concatenate([prev[:, None], cols[:, :-1]], axis=1)
    xx = cols + (shifted - cols) * mu
    sp = np.cumsum([RWKV_WIDTH, RWKV_WIDTH, RWKV_WIDTH, RWKV_LORA_W])
    r, k, v, wd, ad = jnp.split(xx, sp, axis=-1)
    w = -jax.nn.softplus(-(w0 + jnp.dot(jnp.tanh(wd), w_up, precision=HI))) - 0.5
    logw = -jnp.exp(w)
    a = jax.nn.sigmoid(a0 + jnp.dot(ad, a_up, precision=HI))

    def heads(u):
        return u.reshape(b, t, RWKV_HEADS, RWKV_HEAD)

    kk = heads(k * k_k)
    kk = kk / jnp.maximum(jnp.sqrt(jnp.sum(kk * kk, axis=-1, keepdims=True)), 1e-12)
    k = k * (1.0 + (a - 1.0) * k_a)
    r, k, v, logw, a = heads(r), heads(k), heads(v), heads(logw), heads(a)
    c = min(RWKV_CHUNK, t)
    nc = t // c

    def chunks(u):
        return u.reshape(b, nc, c, RWKV_HEADS, RWKV_HEAD).transpose(1, 0, 3, 2, 4)

    r, k, v, logw, kk, a = (chunks(u) for u in (r, k, v, logw, kk, a))
    cum = jnp.cumsum(logw, axis=3)
    w_in = jnp.exp(cum)
    w_ex = jnp.exp(cum - logw)
    w_end = w_in[..., -1:, :]
    alpha_t = -kk * w_ex
    beta_h = kk * a / w_in
    k_h = k / w_in
    r_t = r * w_in
    tri_s = jnp.tril(jnp.ones((c, c), F32), -1)
    tri_i = jnp.tril(jnp.ones((c, c), F32))
    mm = functools.partial(jnp.einsum, precision=HI)
    a_ab = mm('nbhtk,nbhsk->nbhts', alpha_t, beta_h) * tri_s
    a_ak = mm('nbhtk,nbhsk->nbhts', alpha_t, k_h) * tri_s
    a_rb = mm('nbhtk,nbhsk->nbhts', r_t, beta_h) * tri_i
    a_rk = mm('nbhtk,nbhsk->nbhts', r_t, k_h) * tri_i
    eye = jnp.eye(c, dtype=F32)
    tm = eye + a_ab
    pw = a_ab
    steps = 1
    while steps * 2 < c:
        pw = mm('nbhts,nbhsu->nbhtu', pw, pw)
        tm = mm('nbhts,nbhsu->nbhtu', tm, eye + pw)
        steps *= 2
    p_m = mm('nbhts,nbhsk->nbhtk', tm, alpha_t)
    q_m = mm('nbhts,nbhsv->nbhtv', tm, mm('nbhts,nbhsv->nbhtv', a_ak, v))
    b_t = beta_h * w_end
    k_t = k_h * w_end
    m_m = mm('nbhtk,nbhtj->nbhkj', p_m, b_t)
    n_m = mm('nbhtv,nbhtk->nbhvk', q_m, b_t) + mm('nbhtv,nbhtk->nbhvk', v, k_t)
    r_p = r_t + mm('nbhts,nbhsk->nbhtk', a_rb, p_m)
    o_p = mm('nbhts,nbhsv->nbhtv', a_rb, q_m) + mm('nbhts,nbhsv->nbhtv', a_rk, v)

    def step(s, inp):
        m_c, n_c, we_c, rp_c, op_c = inp
        o = mm('bhtk,bhvk->bhtv', rp_c, s) + op_c
        s = s * we_c + mm('bhvk,bhkj->bhvj', s, m_c) + n_c
        return s, o

    s_t, o = lax.scan(step, s0, (m_m, n_m, w_end, r_p, o_p))
    o = o.transpose(1, 0, 3, 2, 4).reshape(b, t, RWKV_HEADS, RWKV_HEAD)
    r, k, v = (u.transpose(1, 0, 3, 2, 4).reshape(b, t, RWKV_HEADS, RWKV_HEAD) for u in (r, k, v))
    mean = jnp.mean(o, axis=-1, keepdims=True)
    var = jnp.mean(jnp.square(o - mean), axis=-1, keepdims=True)
    o = (o - mean) * lax.rsqrt(var + GN_EPS) * ln_w.reshape(RWKV_HEADS, RWKV_HEAD)
    o = o + jnp.sum(r * k * r_k, axis=-1, keepdims=True) * v
    y = o.reshape(b, t, RWKV_WIDTH) * jax.nn.silu(z)
    return y, cols[:, -1], s_t


def _hgrn_mix(q, fz, i, z, s0, lb, ln_w):
    b, t = q.shape[:2]
    log_f = jnp.logaddexp(jnp.log(lb), jnp.log1p(-lb) + jax.nn.log_sigmoid(fz))
    kf = (1.0 - lb) * jax.nn.sigmoid(-fz)
    chunk = HGRN_CHUNK if t % HGRN_CHUNK == 0 else t
    nc = t // chunk

    def chunks(u):
        return u.reshape(b, nc, chunk, HGRN_HEADS, HGRN_HEAD).transpose(1, 0, 3, 2, 4)

    mask = jnp.tril(jnp.ones((chunk, chunk), dtype=bool))

    def step(s, inp):
        q_c, lf_c, k_c, i_c = inp
        cl = jnp.cumsum(lf_c, axis=2)
        dec = jnp.exp(jnp.where(mask[:, :, None], cl[:, :, :, None, :] - cl[:, :, None, :, :], -jnp.inf))
        att = jnp.einsum('bhtk,bhtsk,bhsk->bhts', q_c, dec, k_c)
        o = jnp.einsum('bhts,bhsv->bhtv', att, i_c) + jnp.einsum('bhtk,bhkv->bhtv', q_c * jnp.exp(cl), s)
        cl_end = cl[:, :, -1:]
        s = jnp.exp(cl_end[:, :, 0])[..., None] * s + jnp.einsum('bhsk,bhsv->bhkv', k_c * jnp.exp(cl_end - cl), i_c)
        return s, o

    s_t, o = lax.scan(step, s0, (chunks(q), chunks(log_f), chunks(kf), chunks(i)))
    o = o.transpose(1, 0, 3, 2, 4).reshape(b, t, HGRN_HEADS, HGRN_HEAD)
    o = o * lax.rsqrt(jnp.mean(o * o, axis=-1, keepdims=True) + RMS_EPS) * ln_w.reshape(HGRN_HEADS, HGRN_HEAD)
    y = o.reshape(b, t, HGRN_WIDTH) * jax.nn.silu(z)
    return y, s_t


def _mm(a, b):
    return jnp.dot(a.astype(BF16), b.astype(BF16), preferred_element_type=F32)


def _mm_nt(a, b):
    return lax.dot_general(a.astype(BF16), b.astype(BF16), (((1,), (1,)), ((), ())), preferred_element_type=F32)


def _mm_tn(a, b):
    return lax.dot_general(a.astype(BF16), b.astype(BF16), (((0,), (0,)), ((), ())), preferred_element_type=F32)


def _mm_hi(a, b):
    return jnp.dot(a, b, preferred_element_type=F32, precision=HI)


def _log_sigmoid(x):
    return jnp.minimum(x, 0.0) - jnp.log1p(jnp.exp(-jnp.abs(x)))


HGRN_SUB = 16
MIX_TILE = 512


def _hgrn_body(n_t, q_ref, f_ref, i_ref, z_ref, lbc_ref, lnw_ref, y_ref, s_ref, st_ref, o_ref):
    ti = pl.program_id(2)
    c, sc = HGRN_CHUNK, HGRN_SUB

    @pl.when(ti == 0)
    def _():
        st_ref[...] = jnp.zeros(st_ref.shape, F32)

    log_lb, log_1m_lb, one_m_lb = lbc_ref[0:1, :], lbc_ref[1:2, :], lbc_ref[2:3, :]
    tri = (lax.broadcasted_iota(jnp.int32, (c, c), 0) >= lax.broadcasted_iota(jnp.int32, (c, c), 1)).astype(F32)
    sub_s = lax.broadcasted_iota(jnp.int32, (sc, LANE), 0)

    def chunk(ci, carry):
        r0 = pl.multiple_of(ci * c, c)
        q = q_ref[0, pl.ds(r0, c), :]
        fz = f_ref[0, pl.ds(r0, c), :]
        iv = i_ref[0, pl.ds(r0, c), :]
        b_ = log_1m_lb + _log_sigmoid(fz)
        lf = jnp.maximum(log_lb, b_) + jnp.log1p(jnp.exp(-jnp.abs(log_lb - b_)))
        kf = one_m_lb * jax.nn.sigmoid(-fz)
        cl = _mm_hi(tri, lf)
        st = st_ref[...]
        o = _mm_nt(q * jnp.exp(cl), st)
        pieces = []
        for blk in range(c // sc):
            lo = blk * sc
            cl_r, q_r, k_r, i_r = cl[lo:lo + sc], q[lo:lo + sc], kf[lo:lo + sc], iv[lo:lo + sc]
            acc = o[lo:lo + sc]
            if blk > 0:
                bnd = cl[lo - 1:lo]
                att = _mm_nt(q_r * jnp.exp(cl_r - bnd), kf[:lo] * jnp.exp(bnd - cl[:lo]))
                acc = acc + _mm(att, iv[:lo])
            rows = []
            for t in range(sc):
                w = jnp.where(sub_s <= t, jnp.exp(cl_r[t:t + 1] - cl_r), 0.0)
                d = jnp.sum(q_r[t:t + 1] * w * k_r, axis=1, keepdims=True)
                rows.append(jnp.sum(d * i_r, axis=0, keepdims=True))
            pieces.append(acc + jnp.concatenate(rows, axis=0))
        o_ref[pl.ds(r0, c), :] = jnp.concatenate(pieces, axis=0)
        c_end = cl[c - 1:c]
        st_ref[...] = st * jnp.exp(c_end) + _mm_tn(iv, kf * jnp.exp(c_end - cl))
        return carry

    lax.fori_loop(0, o_ref.shape[0] // c, chunk, 0)
    o = o_ref[...]
    z = z_ref[0]
    y_ref[0] = o * lax.rsqrt(jnp.mean(o * o, axis=-1, keepdims=True) + RMS_EPS) * lnw_ref[...] * (z * jax.nn.sigmoid(z))

    @pl.when(ti == n_t - 1)
    def _():
        s_ref[0, 0] = st_ref[...].T


def _hgrn_prompt(proj_a3, lb, ln_w):
    b, t = proj_a3.shape[:2]
    tt = min(MIX_TILE, t)
    n_t = t // tt
    lbc = jnp.concatenate([jnp.log(lb)[None], jnp.log1p(-lb)[None], (1.0 - lb)[None],
                           jnp.zeros((5, HGRN_WIDTH), F32)], axis=0)
    col = lambda off: pl.BlockSpec((1, tt, LANE), lambda bi, h, i: (bi, i, off // LANE + h))
    return pl.pallas_call(
        functools.partial(_hgrn_body, n_t),
        grid=(b, HGRN_HEADS, n_t),
        in_specs=[col(A_HQ), col(A_HF), col(A_HI), col(A_HZ),
                  pl.BlockSpec((8, LANE), lambda bi, h, i: (0, h)),
                  pl.BlockSpec((1, LANE), lambda bi, h, i: (0, h))],
        out_specs=[pl.BlockSpec((1, tt, LANE), lambda bi, h, i: (bi, i, h)),
                   pl.BlockSpec((1, 1, HGRN_HEAD, HGRN_HEAD), lambda bi, h, i: (bi, h, 0, 0))],
        out_shape=[jax.ShapeDtypeStruct((b, t, HGRN_WIDTH), F32),
                   jax.ShapeDtypeStruct((b, HGRN_HEADS, HGRN_HEAD, HGRN_HEAD), F32)],
        scratch_shapes=[pltpu.VMEM((HGRN_HEAD, HGRN_HEAD), F32), pltpu.VMEM((tt, LANE), F32)],
        compiler_params=_cparams(("arbitrary", "arbitrary", "arbitrary")),
        name="hgrn_prompt",
    )(proj_a3, proj_a3, proj_a3, proj_a3, lbc, ln_w.reshape(1, HGRN_WIDTH))


def _rwkv_body(n_t, cols_ref, z_ref, mu_ref, w0_ref, wup_ref, a0_ref, aup_ref, kk_ref, ka_ref, rk_ref, lnw_ref, bd_ref,
               y_ref, s_ref, prev_ref, st_ref, r_s, k_s, v_s, lw_s, kk_s, a_s, o_s):
    ti = pl.program_id(1)
    tt = cols_ref.shape[1]
    c = RWKV_CHUNK
    n_pair = RWKV_HEADS // 2

    @pl.when(ti == 0)
    def _():
        prev_ref[...] = jnp.zeros(prev_ref.shape, F32)
        st_ref[...] = jnp.zeros(st_ref.shape, F32)

    x = cols_ref[0]
    first = lax.broadcasted_iota(jnp.int32, (tt, 1), 0) == 0
    x_prev = jnp.where(first, prev_ref[0:1, :], pltpu.roll(x, 1, 0))
    prev_ref[0:1, :] = x[tt - 1:tt, :]
    xx = x + (x_prev - x) * mu_ref[...]
    r, k, v = xx[:, 0:RWKV_WIDTH], xx[:, RWKV_WIDTH:2 * RWKV_WIDTH], xx[:, 2 * RWKV_WIDTH:3 * RWKV_WIDTH]
    lora = xx[:, 3 * RWKV_WIDTH:]
    nx = -(w0_ref[...] + _mm_hi(jnp.tanh(lora), wup_ref[...]))
    w_log = -(jnp.maximum(nx, 0.0) + jnp.log1p(jnp.exp(-jnp.abs(nx)))) - 0.5
    lw_s[...] = -jnp.exp(w_log)
    a = jax.nn.sigmoid(a0_ref[...] + _mm_hi(lora, aup_ref[...]))
    bd = bd_ref[...]
    kk_raw = k * kk_ref[...]
    kk_s[...] = kk_raw / jnp.maximum(jnp.sqrt(_mm_hi(kk_raw * kk_raw, bd)), 1e-12)
    k2 = k * (1.0 + (a - 1.0) * ka_ref[...])
    r_s[...] = r
    k_s[...] = k2
    v_s[...] = v
    a_s[...] = a
    bonus = _mm_hi(r * k2 * rk_ref[...], bd) * v

    rows = lax.broadcasted_iota(jnp.int32, (c, LANE), 0)
    lane_in = lax.broadcasted_iota(jnp.int32, (c, LANE), 1) % RWKV_HEAD
    strict = rows > lane_in
    incl = rows >= lane_in
    eye2 = (rows == lane_in).astype(F32)
    tri = (lax.broadcasted_iota(jnp.int32, (c, c), 0) >= lax.broadcasted_iota(jnp.int32, (c, c), 1)).astype(F32)
    lane1 = lax.broadcasted_iota(jnp.int32, (1, LANE), 1)
    m0, m1 = lane1 < RWKV_HEAD, lane1 >= RWKV_HEAD
    bdm = (lax.broadcasted_iota(jnp.int32, (LANE, LANE), 0) // RWKV_HEAD
           == lax.broadcasted_iota(jnp.int32, (LANE, LANE), 1) // RWKV_HEAD)

    def blockdiag(zz):
        zz = zz.astype(BF16)
        return jnp.concatenate([jnp.where(m0, zz, 0), jnp.where(m1, zz, 0)], axis=0)

    def chunk(ci, carry):
        r0 = pl.multiple_of(ci * c, c)
        ds = pl.ds(r0, c)
        lw = lw_s[ds, :]
        cum = _mm_hi(tri, lw)
        w_in, w_ex, inv_in = jnp.exp(cum), jnp.exp(cum - lw), jnp.exp(-cum)
        w_end = jnp.exp(cum[c - 1:c])
        kk_c, vv = kk_s[ds, :], v_s[ds, :]
        alpha = -kk_c * w_ex
        beta_h = kk_c * a_s[ds, :] * inv_in
        k_h = k_s[ds, :] * inv_in
        r_t = r_s[ds, :] * w_in
        beta_e, k_e = beta_h * w_end, k_h * w_end
        outs = []
        for p in range(n_pair):
            sl = slice(p * LANE, (p + 1) * LANE)
            al, bh, kh, rt, vp = alpha[:, sl], beta_h[:, sl], k_h[:, sl], r_t[:, sl], vv[:, sl]
            aa = _mm_nt(jnp.concatenate([al, rt], axis=0),
                        jnp.concatenate([blockdiag(bh), blockdiag(kh)], axis=0))
            a_ab = jnp.where(strict, aa[0:c, 0:LANE], 0.0)
            a_ak = jnp.where(strict, aa[0:c, LANE:], 0.0)
            a_rb = jnp.where(incl, aa[c:, 0:LANE], 0.0)
            a_rk = jnp.where(incl, aa[c:, LANE:], 0.0)
            tm = eye2 + a_ab
            pw = a_ab
            n = 1
            while 2 * n < c:
                pw = _mm(pw, blockdiag(pw))
                tm = tm + _mm(tm, blockdiag(pw))
                n *= 2
            akv = _mm(a_ak, blockdiag(vp))
            pq = _mm(tm, jnp.concatenate([blockdiag(al), blockdiag(akv)], axis=1))
            p_m, q_m = pq[:, 0:LANE], pq[:, LANE:]
            ro = _mm(a_rb, jnp.concatenate([blockdiag(p_m), blockdiag(q_m)], axis=1))
            r_p = rt + ro[:, 0:LANE]
            o_p = ro[:, LANE:] + _mm(a_rk, blockdiag(vp))
            mn = _mm_tn(pq, beta_e[:, sl])
            m_bd = jnp.where(bdm, mn[0:LANE], 0.0)
            n_f = jnp.where(bdm, mn[LANE:] + _mm_tn(vp, k_e[:, sl]), 0.0)
            n_m = n_f[0:RWKV_HEAD] + n_f[RWKV_HEAD:]
            st = st_ref[p]
            outs.append(_mm_nt(r_p, blockdiag(st)) + o_p)
            st_ref[p] = st * w_end[:, sl] + _mm(st, m_bd) + n_m
        o_s[ds, :] = jnp.concatenate(outs, axis=1)
        return carry

    lax.fori_loop(0, tt // c, chunk, 0)
    o = o_s[...]
    inv_n = 1.0 / RWKV_HEAD
    d = o - _mm_hi(o, bd) * inv_n
    var = _mm_hi(d * d, bd) * inv_n
    z = z_ref[0]
    y_ref[0] = (d * lax.rsqrt(var + GN_EPS) * lnw_ref[...] + bonus) * (z * jax.nn.sigmoid(z))

    @pl.when(ti == n_t - 1)
    def _():
        s_ref[0] = st_ref[...]


def _rwkv_prompt(rcols, proj_a3, mu, w0, w_up, a0, a_up, k_k, k_a, r_k, ln_w):
    b, t = rcols.shape[:2]
    tt = min(MIX_TILE, t)
    n_t = t // tt
    n_pair = RWKV_HEADS // 2
    zpad = jnp.zeros((RWKV_LORA_W, RWKV_WIDTH), F32)
    wup = jnp.concatenate([w_up, zpad], axis=0)
    aup = jnp.concatenate([zpad, a_up], axis=0)
    hid = np.arange(RWKV_WIDTH) // RWKV_HEAD
    bd = jnp.asarray((hid[:, None] == hid[None, :]).astype(np.float32))
    vec = lambda a: a.reshape(1, -1)
    full = lambda shape: pl.BlockSpec(shape, lambda bi, i: (0,) * len(shape))
    y, s = pl.pallas_call(
        functools.partial(_rwkv_body, n_t),
        grid=(b, n_t),
        in_specs=[pl.BlockSpec((1, tt, SHIFT_W), lambda bi, i: (bi, i, 0)),
                  pl.BlockSpec((1, tt, RWKV_WIDTH), lambda bi, i: (bi, i, A_RZ // RWKV_WIDTH)),
                  full((1, SHIFT_W)), full((1, RWKV_WIDTH)), full((LANE, RWKV_WIDTH)), full((1, RWKV_WIDTH)),
                  full((LANE, RWKV_WIDTH)), full((1, RWKV_WIDTH)), full((1, RWKV_WIDTH)), full((1, RWKV_WIDTH)),
                  full((1, RWKV_WIDTH)), full((RWKV_WIDTH, RWKV_WIDTH))],
        out_specs=[pl.BlockSpec((1, tt, RWKV_WIDTH), lambda bi, i: (bi, i, 0)),
                   pl.BlockSpec((1, n_pair, RWKV_HEAD, LANE), lambda bi, i: (bi, 0, 0, 0))],
        out_shape=[jax.ShapeDtypeStruct((b, t, RWKV_WIDTH), F32),
                   jax.ShapeDtypeStruct((b, n_pair, RWKV_HEAD, LANE), F32)],
        scratch_shapes=[pltpu.VMEM((8, SHIFT_W), F32), pltpu.VMEM((n_pair, RWKV_HEAD, LANE), F32)]
                       + [pltpu.VMEM((tt, RWKV_WIDTH), F32)] * 7,
        compiler_params=_cparams(("arbitrary", "arbitrary")),
        name="rwkv_prompt",
    )(rcols, proj_a3, vec(mu), vec(w0), wup, vec(a0), aup, vec(k_k), vec(k_a), vec(r_k), vec(ln_w), bd)
    s = s.reshape(b, n_pair, RWKV_HEAD, 2, RWKV_HEAD).transpose(0, 1, 3, 2, 4)
    return y, s.reshape(b, RWKV_HEADS, RWKV_HEAD, RWKV_HEAD)


def _merge_body(x_ref, yn_ref, nz_ref, yr_ref, yh_ref, mg_ref, wb_ref, wo_ref, o_ref):
    nz = nz_ref[...]
    branches = (yn_ref[...] * (nz * jax.nn.sigmoid(nz)), yr_ref[...], yh_ref[...])
    acc = jnp.zeros(o_ref.shape, F32)
    for n, y in enumerate(branches):
        t = jnp.dot(y.astype(BF16), wb_ref[n], preferred_element_type=F32)
        acc = acc + jax.nn.sigmoid(mg_ref[:, n * D_MODEL:(n + 1) * D_MODEL]) * t
    o_ref[...] = x_ref[...] + jnp.dot(acc.astype(BF16), wo_ref[...], preferred_element_type=F32)


def _merge(x2d, y_nsa, proj_a, y_rwkv, y_hgrn, mg, wb, wo, tm):
    m = x2d.shape[0]
    row = lambda w: pl.BlockSpec((tm, w), lambda i: (i, 0))
    return pl.pallas_call(
        _merge_body,
        grid=(m // tm,),
        in_specs=[row(D_MODEL), row(BRANCH_WIDTH),
                  pl.BlockSpec((tm, BRANCH_WIDTH), lambda i: (i, A_NZ // BRANCH_WIDTH)),
                  row(BRANCH_WIDTH), row(BRANCH_WIDTH), row(N_BRANCH * D_MODEL),
                  pl.BlockSpec((N_BRANCH, BRANCH_WIDTH, D_MODEL), lambda i: (0, 0, 0)),
                  pl.BlockSpec((D_MODEL, D_MODEL), lambda i: (0, 0))],
        out_specs=row(D_MODEL),
        out_shape=jax.ShapeDtypeStruct((m, D_MODEL), F32),
        compiler_params=_cparams(("arbitrary",)),
        name="merge",
    )(x2d, y_nsa, proj_a, y_rwkv, y_hgrn, mg, wb, wo)


def _split_w_in(w):
    o = np.concatenate([[0], np.cumsum(IN_SIZES)])
    seg = lambda n: w[:, o[n]:o[n + 1]]
    pad = jnp.zeros((w.shape[0], A_Q - A_GATE - IN_SIZES[2]), w.dtype)
    wa = jnp.concatenate([seg(1), seg(2), pad, seg(0), seg(3), seg(5), seg(6), seg(7), seg(8), seg(9)], axis=1)
    return wa.astype(BF16), seg(4).astype(BF16), seg(10).astype(BF16)


def _layer(x, l, past, prm, lb, tabs, tm):
    b, t = x.shape[:2]
    m = b * t
    x2d = x.reshape(m, D_MODEL)
    nw = prm['norm_w'][l].reshape(1, D_MODEL)
    wa, wb_cols, wc = prm['w_split'][l]
    proj_a = _inproj(x2d, nw, wa, tm, 1536)
    rcols = _inproj(x2d, nw, wb_cols, tm, SHIFT_W).reshape(b, t, SHIFT_W)
    mg = _inproj(x2d, nw, wc, tm, 1536)
    seg = lambda off, w: proj_a[:, off:off + w].reshape(b, t, w)
    c_tab, su_tab, sd_tab, cos_t, sin_t = tabs
    if past is None:
        kv_new, win_new, kcsrc, ksb, vst, kwb, vwt = _kvpost(proj_a, (c_tab, su_tab, sd_tab), b, t, min(tm, t), True)
        w_all, bias = _cmp_weights(prm['nsa_cmp_pe'][l], prm['nsa_cmp_w'][l])
        kc, vct = _compress(kcsrc, w_all, bias, b, t)
        gate_t = seg(A_GATE, 3 * NSA_HEADS).transpose(0, 2, 1)
        y_nsa = _nsa_prompt(proj_a, gate_t, cos_t, sin_t, kc, vct, ksb, vst, kwb, vwt, b, t)
        win_state = win_new.reshape(b, t, 2, NSA_KV, HEAD_DIM)[:, -min(WINDOW, t):]
    else:
        cache_kv, page_table, cache_win, s_r, prev, s_h = past
        kv_new, win_new = _kvpost(proj_a, (c_tab, su_tab, sd_tab), b, t, tm, False)
        win5 = win_new.reshape(b, t, 2, NSA_KV, HEAD_DIM)
        win_state = jnp.concatenate([cache_win[l][:, t:], win5], axis=1)
        gates = jax.nn.sigmoid(seg(A_GATE, 3 * NSA_HEADS)).reshape(b, t, NSA_HEADS, 3)
        q = seg(A_Q, NSA_WIDTH).reshape(b, t, NSA_HEADS, HEAD_DIM)
        y_nsa = _nsa_decode_pallas(l, q, gates, cache_kv, page_table, kv_new.reshape(b, t, 512), cache_win,
                                   win_new.reshape(b, t, 256), prm['nsa_cmp_pe'][l], prm['nsa_cmp_w'][l])
    rwkv_prm = (prm['rwkv_mu'][l], prm['rwkv_w0'][l], prm['rwkv_w_up'][l], prm['rwkv_a0'][l], prm['rwkv_a_up'][l],
                prm['rwkv_k_k'][l], prm['rwkv_k_a'][l], prm['rwkv_r_k'][l], prm['rwkv_ln_w'][l])
    if past is None:
        proj_a3 = proj_a.reshape(b, t, A_WIDTH)
        y_rwkv, s_r = _rwkv_prompt(rcols, proj_a3, *rwkv_prm)
        shift_state = rcols[:, -1]
        y_hgrn, s_h = _hgrn_prompt(proj_a3, lb, prm['hgrn_ln_w'][l])
    else:
        y_rwkv, shift_state, s_r = _rwkv_mix(rcols, seg(A_RZ, RWKV_WIDTH), prev, s_r, *rwkv_prm)
        y_hgrn, s_h = _hgrn_mix(seg(A_HQ, HGRN_WIDTH), seg(A_HF, HGRN_WIDTH), seg(A_HI, HGRN_WIDTH),
                                seg(A_HZ, HGRN_WIDTH), s_h, lb, prm['hgrn_ln_w'][l])
    x_new = _merge(x2d, y_nsa.reshape(m, NSA_WIDTH), proj_a, y_rwkv.reshape(m, RWKV_WIDTH), y_hgrn.reshape(m, HGRN_WIDTH),
                   mg, prm['w_branch_bf16'][l], prm['w_out_bf16'][l], min(tm, 512))
    kv_state = kv_new.reshape(b, t, 4, NSA_KV, HEAD_DIM)
    return x_new.reshape(b, t, D_MODEL), (kv_state, win_state, s_r, shift_state, s_h)


def kernel(x_prompt, x_sample, cache_kv, cache_win, state_rwkv, state_shift, state_hgrn, page_table,
           norm_w, w_in, nsa_cmp_pe, nsa_cmp_w, rwkv_mu, rwkv_w0, rwkv_w_up, rwkv_a0, rwkv_a_up,
           rwkv_k_k, rwkv_k_a, rwkv_r_k, rwkv_ln_w, hgrn_lb_logits, hgrn_ln_w, w_branch, w_out, norm_f):
    prm = {'norm_w': norm_w, 'nsa_cmp_pe': nsa_cmp_pe, 'nsa_cmp_w': nsa_cmp_w,
           'rwkv_mu': rwkv_mu, 'rwkv_w0': rwkv_w0, 'rwkv_w_up': rwkv_w_up, 'rwkv_a0': rwkv_a0,
           'rwkv_a_up': rwkv_a_up, 'rwkv_k_k': rwkv_k_k, 'rwkv_k_a': rwkv_k_a, 'rwkv_r_k': rwkv_r_k,
           'rwkv_ln_w': rwkv_ln_w, 'hgrn_ln_w': hgrn_ln_w,
           'w_split': [_split_w_in(w_in[l]) for l in range(DEPTH)],
           'w_branch_bf16': w_branch.astype(BF16), 'w_out_bf16': w_out.astype(BF16)}
    cs = jnp.cumsum(jax.nn.softmax(hgrn_lb_logits, axis=0), axis=0)
    lbs = cs - cs[0:1]
    n_dec, dec_t = x_sample.shape[:2]
    bp, tp = x_prompt.shape[:2]
    tabs_p = _rope_tables(jnp.arange(tp))
    tabs_s = _rope_tables(PAST_LEN + jnp.arange(n_dec * dec_t) % dec_t)
    tm_p = min(1024, bp * tp)
    tm_s = n_dec * dec_t
    xp, xs = x_prompt, x_sample
    outs = [[] for _ in range(10)]
    for l in range(DEPTH):
        xp, st_p = _layer(xp, l, None, prm, lbs[l], tabs_p, tm_p)
        past = (cache_kv, page_table, cache_win, state_rwkv[l], state_shift[l], state_hgrn[l])
        xs, st_s = _layer(xs, l, past, prm, lbs[l], tabs_s, tm_s)
        for n in range(5):
            outs[2 * n].append(st_p[n])
            outs[2 * n + 1].append(st_s[n])
    y_prompt = _final_norm(xp.reshape(bp * tp, D_MODEL), norm_f, tm_p).reshape(xp.shape)
    y_sample = _final_norm(xs.reshape(n_dec * dec_t, D_MODEL), norm_f, tm_s).reshape(xs.shape)
    return (y_prompt, y_sample) + tuple(jnp.stack(o) for o in outs)
```

```python
import functools

import jax
import jax.numpy as jnp
import numpy as np
from jax import lax
from jax.experimental import pallas as pl
from jax.experimental.pallas import tpu as pltpu

F32 = jnp.float32
BF16 = jnp.bfloat16
HI = lax.Precision.HIGHEST

D_MODEL = 1024
DEPTH = 4
PAST_LEN = 2048
PAGE_SIZE = 128
NSA_HEADS = 8
NSA_KV = 2
HEAD_DIM = 64
NSA_HPG = NSA_HEADS // NSA_KV
NSA_WIDTH = NSA_HEADS * HEAD_DIM
ROT_DIM = HEAD_DIM // 4
ROPE_THETA = 500000.0
CMP_BLOCK = 32
CMP_STRIDE = 16
SLC_BLOCK = 64
N_SELECT = 16
WINDOW = 512
Q_BLOCK = 128
RWKV_HEADS = 8
RWKV_HEAD = 64
RWKV_WIDTH = RWKV_HEADS * RWKV_HEAD
RWKV_LORA_W = 64
RWKV_LORA_A = 64
SHIFT_W = 3 * RWKV_WIDTH + RWKV_LORA_W + RWKV_LORA_A
HGRN_HEADS = 4
HGRN_HEAD = 128
HGRN_WIDTH = HGRN_HEADS * HGRN_HEAD
HGRN_CHUNK = 64
N_BRANCH = 3
BRANCH_WIDTH = 512
KV_COLS = 6 * NSA_KV * HEAD_DIM
IN_SIZES = (NSA_WIDTH, KV_COLS, 3 * NSA_HEADS, NSA_WIDTH, SHIFT_W, RWKV_WIDTH,
            HGRN_WIDTH, HGRN_WIDTH, HGRN_WIDTH, HGRN_WIDTH, N_BRANCH * D_MODEL)
RMS_EPS = 1e-6
GN_EPS = 64e-5

LANE = 128
VMEM_LIMIT = 56 * 1024 * 1024
NEG = -1e30
KEY_TILE = 256
ATT_UNROLL = 2
LOG2E = 1.4426950408889634
RWKV_CHUNK = 64

A_KV = 0
A_GATE = 768
A_Q = 1024
A_NZ = 1536
A_RZ = 2048
A_HQ = 2560
A_HF = 3072
A_HI = 3584
A_HZ = 4096
A_WIDTH = 4608


def _cparams(sem):
    return pltpu.CompilerParams(dimension_semantics=sem, vmem_limit_bytes=VMEM_LIMIT)


def _inproj_body(x_ref, nw_ref, w_ref, o_ref, h_ref):
    @pl.when(pl.program_id(1) == 0)
    def _():
        x = x_ref[...]
        ms = jnp.mean(x * x, axis=-1, keepdims=True)
        h_ref[...] = (x * lax.rsqrt(ms + RMS_EPS) * nw_ref[...]).astype(BF16)

    o_ref[...] = jnp.dot(h_ref[...], w_ref[...], preferred_element_type=F32)


def _inproj(x2d, nw, w, tm, tn):
    m, d = x2d.shape
    n = w.shape[1]
    return pl.pallas_call(
        _inproj_body,
        grid=(m // tm, n // tn),
        in_specs=[pl.BlockSpec((tm, d), lambda i, j: (i, 0)),
                  pl.BlockSpec((1, d), lambda i, j: (0, 0)),
                  pl.BlockSpec((d, tn), lambda i, j: (0, j))],
        out_specs=pl.BlockSpec((tm, tn), lambda i, j: (i, j)),
        out_shape=jax.ShapeDtypeStruct((m, n), F32),
        scratch_shapes=[pltpu.VMEM((tm, d), BF16)],
        compiler_params=_cparams(("arbitrary", "arbitrary")),
        name="inproj",
    )(x2d, nw, w)


def _rmsnorm_body(x_ref, w_ref, o_ref):
    x = x_ref[...]
    ms = jnp.mean(x * x, axis=-1, keepdims=True)
    o_ref[...] = x * lax.rsqrt(ms + RMS_EPS) * w_ref[...]


def _final_norm(x2d, w, tm):
    m, d = x2d.shape
    return pl.pallas_call(
        _rmsnorm_body,
        grid=(m // tm,),
        in_specs=[pl.BlockSpec((tm, d), lambda i: (i, 0)), pl.BlockSpec((1, d), lambda i: (0, 0))],
        out_specs=pl.BlockSpec((tm, d), lambda i: (i, 0)),
        out_shape=jax.ShapeDtypeStruct((m, d), F32),
        compiler_params=_cparams(("arbitrary",)),
        name="final_norm",
    )(x2d, w.reshape(1, d))


def _rope_tables(pos):
    half = ROT_DIM // 2
    inv = ROPE_THETA ** (-jnp.arange(0, ROT_DIM, 2, dtype=F32) / ROT_DIM)
    ang = pos.astype(F32)[:, None] * inv[None, :]
    cos, sin = jnp.cos(ang), jnp.sin(ang)
    n = pos.shape[0]
    ones = jnp.ones((n, HEAD_DIM - ROT_DIM), F32)
    zeros8 = jnp.zeros((n, half), F32)
    zeros = jnp.zeros((n, HEAD_DIM - ROT_DIM), F32)
    c = jnp.concatenate([cos, cos, ones], axis=1)
    s_up = jnp.concatenate([zeros8, sin, zeros], axis=1)
    s_dn = jnp.concatenate([-sin, zeros8, zeros], axis=1)
    tile = lambda a: jnp.concatenate([a] * NSA_KV, axis=1)
    return tile(c), tile(s_up), tile(s_dn), cos.T, sin.T


def _kvpost_body(with_attn, p_ref, c_ref, su_ref, sd_ref, kv_ref, win_ref, *extra):
    x = p_ref[...]
    c, su, sd = c_ref[...], su_ref[...], sd_ref[...]

    def rope(v):
        return v * c + pltpu.roll(v, ROT_DIM // 2, 1) * su + pltpu.roll(v, LANE - ROT_DIM // 2, 1) * sd

    k_slc = rope(x[:, 256:384])
    k_win = rope(x[:, 512:640])
    kv_ref[:, 0:256] = x[:, 0:256]
    kv_ref[:, 256:384] = k_slc
    kv_ref[:, 384:512] = x[:, 384:512]
    win_ref[:, 0:128] = k_win
    win_ref[:, 128:256] = x[:, 640:768]
    if with_attn:
        kc_ref, ks_ref, vst_ref, kw_ref, vwt_ref = extra
        kc_ref[...] = x[:, 0:256].astype(BF16)
        ks_ref[...] = k_slc.astype(BF16)
        vst_ref[0] = x[:, 384:512].T.astype(BF16)
        kw_ref[...] = k_win.astype(BF16)
        vwt_ref[0] = x[:, 640:768].T.astype(BF16)


def _kvpost(proj_a, tabs, b, t, tm, with_attn):
    m = b * t
    nt = tabs[0].shape[0] // tm
    tab_spec = pl.BlockSpec((tm, LANE), lambda i: (i % nt, 0))
    out_shape = [jax.ShapeDtypeStruct((m, 512), F32), jax.ShapeDtypeStruct((m, 256), F32)]
    out_specs = [pl.BlockSpec((tm, 512), lambda i: (i, 0)), pl.BlockSpec((tm, 256), lambda i: (i, 0))]
    if with_attn:
        out_shape += [jax.ShapeDtypeStruct((m, 256), BF16), jax.ShapeDtypeStruct((m, LANE), BF16),
                      jax.ShapeDtypeStruct((b, LANE, t), BF16), jax.ShapeDtypeStruct((m, LANE), BF16),
                      jax.ShapeDtypeStruct((b, LANE, t), BF16)]
        tspec = pl.BlockSpec((1, LANE, tm), lambda i: (i // nt, 0, i % nt))
        out_specs += [pl.BlockSpec((tm, 256), lambda i: (i, 0)), pl.BlockSpec((tm, LANE), lambda i: (i, 0)),
                      tspec, pl.BlockSpec((tm, LANE), lambda i: (i, 0)), tspec]
    return pl.pallas_call(
        functools.partial(_kvpost_body, with_attn),
        grid=(m // tm,),
        in_specs=[pl.BlockSpec((tm, KV_COLS), lambda i: (i, 0)), tab_spec, tab_spec, tab_spec],
        out_specs=out_specs,
        out_shape=out_shape,
        compiler_params=_cparams(("arbitrary",)),
        name="kvpost",
    )(proj_a, *tabs)


def _cmp_weights(pe, cw):
    eye = jnp.eye(NSA_KV, dtype=F32)
    cols = []
    for s in range(2):
        for part in range(2):
            w = cw[s, part * CMP_STRIDE:(part + 1) * CMP_STRIDE]
            blk = jnp.einsum('jde,gh->jgdhe', w, eye).reshape(CMP_STRIDE, LANE, LANE)
            full = jnp.zeros((CMP_STRIDE, 2, LANE, LANE), F32).at[:, s].set(blk)
            cols.append(full.reshape(CMP_STRIDE * 2 * LANE, LANE))
    w_all = jnp.concatenate(cols, axis=1).astype(BF16)
    bias = jnp.einsum('sjd,sjde->se', pe, cw)
    bias = jnp.concatenate([bias[0], bias[0], bias[1], bias[1]]).reshape(1, 2 * LANE)
    return w_all, bias


def _cmp_body(sub_ref, w_ref, b_ref, kc_ref, vct_ref):
    r = jnp.dot(sub_ref[0], w_ref[...], preferred_element_type=F32)
    n = r.shape[0]
    b = b_ref[...]
    kc = r[:, 0:128] + pltpu.roll(r[:, 128:256], n - 1, 0) + b[:, 0:128]
    vc = r[:, 256:384] + pltpu.roll(r[:, 384:512], n - 1, 0) + b[:, 128:256]
    kc_ref[0] = kc.astype(BF16)
    vct_ref[0] = vc.T.astype(BF16)


def _compress(kcsrc, w_all, bias, b, t):
    n_sub = t // CMP_STRIDE
    sub = kcsrc.reshape(b, n_sub, CMP_STRIDE * 256)
    return pl.pallas_call(
        _cmp_body,
        grid=(b,),
        in_specs=[pl.BlockSpec((1, n_sub, CMP_STRIDE * 256), lambda i: (i, 0, 0)),
                  pl.BlockSpec((CMP_STRIDE * 256, 512), lambda i: (0, 0)),
                  pl.BlockSpec((1, 256), lambda i: (0, 0))],
        out_specs=[pl.BlockSpec((1, n_sub, LANE), lambda i: (i, 0, 0)),
                   pl.BlockSpec((1, LANE, n_sub), lambda i: (i, 0, 0))],
        out_shape=[jax.ShapeDtypeStruct((b, n_sub, LANE), BF16), jax.ShapeDtypeStruct((b, LANE, n_sub), BF16)],
        compiler_params=_cparams(("arbitrary",)),
        name="compress",
    )(sub, w_all, bias)


def _slc_cmp_matrix(n_slc, n_cmp, n_cmp_pad):
    ratio = SLC_BLOCK // CMP_STRIDE
    span = CMP_BLOCK // CMP_STRIDE
    m = np.zeros((n_slc, n_cmp_pad), np.float32)
    for j in range(n_slc):
        for k in range(ratio + span - 1):
            n = ratio * j - (span - 1) + k
            if 0 <= n < n_cmp:
                m[j, n] = 1.0
    return m


def _nsa_body(n_cmp_pad, n_slc, q_ref, gt_ref, cos_ref, sin_ref, kc_ref, vct_ref, ks_ref, vst_ref,
              kw_ref, vwt_ref, mt_ref, o_ref, sc_ref, sel_ref, ml_ref, acc_ref, s_ref):
    i = pl.program_id(1)
    q0 = i * Q_BLOCK
    hq = NSA_HPG * Q_BLOCK
    q = q_ref[0]
    cos4 = jnp.concatenate([cos_ref[...]] * NSA_HPG, axis=1)
    sin4 = jnp.concatenate([sin_ref[...]] * NSA_HPG, axis=1)
    gs = jax.nn.sigmoid(gt_ref[0])
    lane = lax.broadcasted_iota(jnp.int32, (Q_BLOCK, Q_BLOCK), 1)
    sub = lax.broadcasted_iota(jnp.int32, (Q_BLOCK, Q_BLOCK), 0)
    sub_k = lax.broadcasted_iota(jnp.int32, (KEY_TILE, Q_BLOCK), 0)
    qp_k = q0 + lax.broadcasted_iota(jnp.int32, (KEY_TILE, Q_BLOCK), 1)
    tpk = KEY_TILE // Q_BLOCK
    bpt = KEY_TILE // SLC_BLOCK
    zeros_g = jnp.zeros((HEAD_DIM, hq), F32)
    half = ROT_DIM // 2

    for g in range(NSA_KV):
        xa = q[:, g * 256:g * 256 + 128].T
        xb = q[:, g * 256 + 128:g * 256 + 256].T
        qn = jnp.concatenate([xa[0:64], xa[64:128], xb[0:64], xb[64:128]], axis=1) * (HEAD_DIM ** -0.5 * LOG2E)
        x1, x2 = qn[0:half], qn[half:ROT_DIM]
        qr = jnp.concatenate([x1 * cos4 - x2 * sin4, x2 * cos4 + x1 * sin4, qn[ROT_DIM:]], axis=0)

        def pad(a):
            parts = [zeros_g] * NSA_KV
            parts[g] = a
            return jnp.concatenate(parts, axis=0).astype(BF16)

        qn_p, qr_p = pad(qn), pad(qr)

        s1 = jnp.dot(kc_ref[0], qn_p, preferred_element_type=F32)
        n_idx = lax.broadcasted_iota(jnp.int32, (n_cmp_pad, Q_BLOCK), 0)
        lane_c = lax.broadcasted_iota(jnp.int32, (n_cmp_pad, Q_BLOCK), 1)
        cmp_ok = (CMP_STRIDE * n_idx + CMP_BLOCK - 1) <= (q0 + lane_c)
        ps = []
        for h in range(NSA_HPG):
            s = jnp.where(cmp_ok, s1[:, h * Q_BLOCK:(h + 1) * Q_BLOCK], NEG)
            m = jnp.max(s, axis=0, keepdims=True)
            e = jnp.where(cmp_ok, jnp.exp2(s - m), 0.0)
            d = jnp.maximum(jnp.sum(e, axis=0, keepdims=True), 1e-30)
            ps.append(e / d)
        imp = ps[0] + ps[1] + ps[2] + ps[3]
        p1 = jnp.concatenate(ps, axis=1).astype(BF16)
        o_cmp = jnp.dot(vct_ref[0, g * HEAD_DIM:(g + 1) * HEAD_DIM, :], p1, preferred_element_type=F32)
        imps = jnp.dot(mt_ref[...], imp, preferred_element_type=F32, precision=HI)

        blk = lax.broadcasted_iota(jnp.int32, (n_slc, Q_BLOCK), 0)
        qp_s = q0 + lax.broadcasted_iota(jnp.int32, (n_slc, Q_BLOCK), 1)
        causal = blk * SLC_BLOCK <= qp_s
        qblk = qp_s // SLC_BLOCK
        forced = (blk == 0) | (blk == qblk) | (blk == qblk - 1)
        score = jnp.where(causal, jnp.where(forced, jnp.inf, imps), -jnp.inf)
        sc_ref[...] = score

        def rank_step(jh, cnt):
            for j in (2 * jh, 2 * jh + 1):
                row = sc_ref[pl.ds(j, 1), :]
                ge = jnp.where(row >= score, 1.0, 0.0)
                gt = jnp.where(row > score, 1.0, 0.0)
                cnt = cnt + jnp.where(blk > j, ge, gt)
            return cnt

        cnt = lax.fori_loop(0, jnp.minimum(i + 1, n_slc // 2), rank_step, jnp.zeros((n_slc, Q_BLOCK), F32))
        sel_ref[...] = jnp.where(cnt < N_SELECT, jnp.where(score > -jnp.inf, 0.0, NEG), NEG)

        init = (jnp.full((1, hq), NEG, F32), jnp.zeros((1, hq), F32), jnp.zeros((HEAD_DIM, hq), F32))

        def scores(k_ref, row0, n_rows):
            return jnp.dot(k_ref[0, pl.ds(pl.multiple_of(row0, KEY_TILE), n_rows), :], qr_p,
                           preferred_element_type=F32)

        def attend(s, vt_ref, t, bias, state):
            m, l, acc = state
            off = pl.multiple_of(t * KEY_TILE, KEY_TILE)
            ps, alphas, ms, ls = [], [], [], []
            for h in range(NSA_HPG):
                hs = slice(h * Q_BLOCK, (h + 1) * Q_BLOCK)
                sh = s[:, hs] + bias
                m_new = jnp.maximum(m[:, hs], jnp.max(sh, axis=0, keepdims=True))
                alpha = jnp.exp2(m[:, hs] - m_new)
                p = jnp.exp2(sh - m_new)
                ms.append(m_new)
                ls.append(alpha * l[:, hs] + jnp.sum(p, axis=0, keepdims=True))
                ps.append(p.astype(BF16))
                alphas.append(alpha)
            vt = vt_ref[0, g * HEAD_DIM:(g + 1) * HEAD_DIM, pl.ds(off, KEY_TILE)]
            pv = jnp.dot(vt, jnp.concatenate(ps, axis=1), preferred_element_type=F32)
            cat = lambda parts: jnp.concatenate(parts, axis=1)
            return cat(ms), cat(ls), acc * cat(alphas) + pv

        keys_per_step = KEY_TILE * ATT_UNROLL
        n_steps = (q0 + Q_BLOCK + keys_per_step - 1) // keys_per_step

        def slc_step(u, carry):
            s_next = scores(ks_ref, jnp.minimum(u + 1, n_steps - 1) * keys_per_step, keys_per_step)
            state = (ml_ref[0:1, :], ml_ref[1:2, :], acc_ref[...])
            for j in range(ATT_UNROLL):
                t = u * ATT_UNROLL + j
                rows = [jnp.broadcast_to(sel_ref[pl.ds(bpt * t + jj, 1), :], (SLC_BLOCK, Q_BLOCK))
                        for jj in range(bpt)]
                bias = jnp.where(t * KEY_TILE + sub_k <= qp_k, jnp.concatenate(rows, axis=0), NEG)
                state = attend(s_ref[u % 2, j * KEY_TILE:(j + 1) * KEY_TILE, :], vst_ref, t, bias, state)
            ml_ref[0:1, :], ml_ref[1:2, :], acc_ref[...] = state
            s_ref[(u + 1) % 2] = s_next
            return carry

        ml_ref[0:1, :], ml_ref[1:2, :], acc_ref[...] = init
        s_ref[0] = scores(ks_ref, 0, keys_per_step)
        lax.fori_loop(0, n_steps, slc_step, 0)
        o_slc = acc_ref[...] / ml_ref[1:2, :]

        state = init
        for k in range(WINDOW // KEY_TILE + 1):
            t = i // tpk - k
            tc = jnp.maximum(t, 0)
            kp = tc * KEY_TILE + sub_k
            ok = jnp.where(kp <= qp_k, qp_k - kp, WINDOW) < jnp.where(t >= 0, WINDOW, 0)
            state = attend(scores(kw_ref, tc * KEY_TILE, KEY_TILE), vwt_ref, tc, jnp.where(ok, 0.0, NEG), state)
        o_win = state[2] / state[1]

        def gate(jj):
            return jnp.concatenate([gs[(g * NSA_HPG + h) * 3 + jj:(g * NSA_HPG + h) * 3 + jj + 1, :]
                                    for h in range(NSA_HPG)], axis=1)

        o_t = gate(0) * o_cmp + gate(1) * o_slc + gate(2) * o_win
        ya = jnp.concatenate([o_t[:, 0:128], o_t[:, 128:256]], axis=0).T
        yb = jnp.concatenate([o_t[:, 256:384], o_t[:, 384:512]], axis=0).T
        o_ref[0, :, g * 256:g * 256 + 128] = ya
        o_ref[0, :, g * 256 + 128:g * 256 + 256] = yb


def _nsa_prompt(proj_a, gate_t, cos_t, sin_t, kc, vct, ksb, vst, kwb, vwt, b, t):
    nb = t // Q_BLOCK
    n_sub = t // CMP_STRIDE
    n_slc = t // SLC_BLOCK
    mt = jnp.asarray(_slc_cmp_matrix(n_slc, n_sub - 1, n_sub))
    seq = lambda w: pl.BlockSpec((1, t, w), lambda bi, i: (bi, 0, 0))
    seq_t = pl.BlockSpec((1, LANE, t), lambda bi, i: (bi, 0, 0))
    return pl.pallas_call(
        functools.partial(_nsa_body, n_sub, n_slc),
        grid=(b, nb),
        in_specs=[pl.BlockSpec((1, Q_BLOCK, NSA_WIDTH), lambda bi, i: (bi, i, A_Q // NSA_WIDTH)),
                  pl.BlockSpec((1, 3 * NSA_HEADS, Q_BLOCK), lambda bi, i: (bi, 0, i)),
                  pl.BlockSpec((ROT_DIM // 2, Q_BLOCK), lambda bi, i: (0, i)),
                  pl.BlockSpec((ROT_DIM // 2, Q_BLOCK), lambda bi, i: (0, i)),
                  pl.BlockSpec((1, n_sub, LANE), lambda bi, i: (bi, 0, 0)),
                  pl.BlockSpec((1, LANE, n_sub), lambda bi, i: (bi, 0, 0)),
                  seq(LANE), seq_t, seq(LANE), seq_t,
                  pl.BlockSpec((n_slc, n_sub), lambda bi, i: (0, 0))],
        out_specs=pl.BlockSpec((1, Q_BLOCK, NSA_WIDTH), lambda bi, i: (bi, i, 0)),
        out_shape=jax.ShapeDtypeStruct((b, t, NSA_WIDTH), F32),
        scratch_shapes=[pltpu.VMEM((n_slc, Q_BLOCK), F32), pltpu.VMEM((n_slc, Q_BLOCK), F32),
                        pltpu.VMEM((8, NSA_HPG * Q_BLOCK), F32), pltpu.VMEM((HEAD_DIM, NSA_HPG * Q_BLOCK), F32),
                        pltpu.VMEM((2, KEY_TILE * ATT_UNROLL, NSA_HPG * Q_BLOCK), F32)],
        compiler_params=_cparams(("arbitrary", "arbitrary")),
        name="nsa_prompt",
    )(proj_a.reshape(b, t, A_WIDTH), gate_t, cos_t, sin_t, kc, vct,
      ksb.reshape(b, t, LANE), vst, kwb.reshape(b, t, LANE), vwt, mt)


def _masked_softmax(s, mask):
    s = jnp.where(mask, s, -jnp.inf)
    m = jnp.max(s, axis=-1, keepdims=True)
    m = jnp.where(jnp.isfinite(m), m, 0.0)
    e = jnp.exp(s - m)
    return e / jnp.maximum(jnp.sum(e, axis=-1, keepdims=True), 1e-30)


def _rope_rows(x, pos):
    half = ROT_DIM // 2
    inv = ROPE_THETA ** (-jnp.arange(0, ROT_DIM, 2, dtype=F32) / ROT_DIM)
    ang = pos.astype(F32)[:, None] * inv[None, :]
    cos = jnp.cos(ang)[None, :, None, :]
    sin = jnp.sin(ang)[None, :, None, :]
    x1, x2 = x[..., :half], x[..., half:ROT_DIM]
    return jnp.concatenate([x1 * cos - x2 * sin, x2 * cos + x1 * sin, x[..., ROT_DIM:]], axis=-1)


def _nsa_decode(q, gates, kv_all, win_all, pe, cw):
    b, tq = q.shape[:2]
    tk = kv_all.shape[1]
    scale = HEAD_DIM ** -0.5
    qpos = PAST_LEN + jnp.arange(tq)
    qn = q.reshape(b, tq, NSA_KV, NSA_HPG, HEAD_DIM)
    qr = _rope_rows(q, qpos).reshape(b, tq, NSA_KV, NSA_HPG, HEAD_DIM)
    n_sub = tk // CMP_STRIDE
    n_cmp = n_sub - 1

    def compress(rows, s):
        sub = rows[:, :n_sub * CMP_STRIDE].reshape(b, n_sub, CMP_STRIDE, NSA_KV, HEAD_DIM)
        a = jnp.einsum('bnjgd,jde->bnge', sub, cw[s, :CMP_STRIDE])
        c = jnp.einsum('bnjgd,jde->bnge', sub, cw[s, CMP_STRIDE:])
        bias = jnp.einsum('jd,jde->e', pe[s], cw[s])
        return a[:, :n_cmp] + c[:, 1:] + bias

    kc = compress(kv_all[:, :, 0], 0)
    vc = compress(kv_all[:, :, 1], 1)
    cmp_end = CMP_STRIDE * jnp.arange(n_cmp) + CMP_BLOCK - 1
    s1 = jnp.einsum('bqghd,bngd->bqghn', qn, kc) * scale
    p1 = _masked_softmax(s1, (cmp_end[None, :] <= qpos[:, None])[None, :, None, None, :])
    o_cmp = jnp.einsum('bqghn,bngd->bqghd', p1, vc)
    n_slc = -(-tk // SLC_BLOCK)
    mt = jnp.asarray(_slc_cmp_matrix(n_slc, n_cmp, n_cmp))
    imp = jnp.einsum('bqgn,jn->bqgj', jnp.sum(p1, axis=3), mt, precision=HI)
    blk = jnp.arange(n_slc)
    qblk = qpos // SLC_BLOCK
    causal = (blk[None, :] * SLC_BLOCK <= qpos[:, None])[None, :, None, :]
    forced = ((blk[None, :] == 0) | (blk[None, :] == qblk[:, None]) | (blk[None, :] == qblk[:, None] - 1))[None, :, None, :]
    score = jnp.where(causal, jnp.where(forced, jnp.inf, imp), -jnp.inf)
    ahead = (score[..., None, :] > score[..., :, None]) | (
        (score[..., None, :] == score[..., :, None]) & (blk[None, :] < blk[:, None]))
    rank = jnp.sum(ahead, axis=-1)
    sel = (rank < N_SELECT) & (score > -jnp.inf)
    kpos = jnp.arange(tk)
    key_ok = sel[..., kpos // SLC_BLOCK] & (kpos[None, :] <= qpos[:, None])[None, :, None, :]
    s2 = jnp.einsum('bqghd,bkgd->bqghk', qr, kv_all[:, :, 2]) * scale
    p2 = _masked_softmax(s2, key_ok[:, :, :, None, :])
    o_slc = jnp.einsum('bqghk,bkgd->bqghd', p2, kv_all[:, :, 3])
    wpos = PAST_LEN - WINDOW + jnp.arange(win_all.shape[1])
    m3 = (wpos[None, :] <= qpos[:, None]) & (qpos[:, None] - wpos[None, :] < WINDOW)
    s3 = jnp.einsum('bqghd,bkgd->bqghk', qr, win_all[:, :, 0]) * scale
    p3 = _masked_softmax(s3, m3[None, :, None, None, :])
    o_win = jnp.einsum('bqghk,bkgd->bqghd', p3, win_all[:, :, 1])
    g = gates.reshape(b, tq, NSA_KV, NSA_HPG, 3)
    o = g[..., 0:1] * o_cmp + g[..., 1:2] * o_slc + g[..., 2:3] * o_win
    return o.reshape(b, tq, NSA_WIDTH)


N_PAGES = PAST_LEN // PAGE_SIZE
DEC_ROWS = 8
DEC_QROWS = NSA_HEADS * 4


def _nsa_dec_body(dec_t, pt_ref, *refs):
    pages = refs[:N_PAGES]
    kvn_ref, cw_ref, wn_ref, qn_ref, qr_ref, g_ref, w_ref, b_ref, mt_ref, o_ref, ck_ref, cv_ref = refs[N_PAGES:]
    del pt_ref
    qn, qr = qn_ref[0], qr_ref[0]
    nq = qn.shape[0]
    lane = lax.broadcasted_iota(jnp.int32, (nq, LANE), 1)
    qi = lax.broadcasted_iota(jnp.int32, (nq, LANE), 0) % dec_t
    qpos = PAST_LEN + qi
    n_sub = PAST_LEN // CMP_STRIDE

    for p in range(N_PAGES):
        ck_ref[p * PAGE_SIZE:(p + 1) * PAGE_SIZE, :] = pages[p][0, 0, :, 0:128]
        cv_ref[p * PAGE_SIZE:(p + 1) * PAGE_SIZE, :] = pages[p][0, 0, :, 128:256]
    parts = []
    for j in range(CMP_STRIDE):
        parts += [ck_ref[pl.ds(j, n_sub, stride=CMP_STRIDE), :], cv_ref[pl.ds(j, n_sub, stride=CMP_STRIDE), :]]
    r = _mm(jnp.concatenate(parts, axis=1), w_ref[...])
    bias = b_ref[...]
    kc = r[:, 0:128] + pltpu.roll(r[:, 128:256], n_sub - 1, 0) + bias[:, 0:128]
    vc = r[:, 256:384] + pltpu.roll(r[:, 384:512], n_sub - 1, 0) + bias[:, 128:256]

    def softmax_rows(scores, oks):
        scores = [jnp.where(ok, s, NEG) for s, ok in zip(scores, oks)]
        m = functools.reduce(jnp.maximum, [jnp.max(s, axis=1, keepdims=True) for s in scores])
        es = [jnp.where(ok, jnp.exp(s - m), 0.0) for s, ok in zip(scores, oks)]
        d = jnp.maximum(functools.reduce(jnp.add, [jnp.sum(e, axis=1, keepdims=True) for e in es]), 1e-30)
        return [e / d for e in es]

    cmp_ok = (CMP_STRIDE * lane + CMP_BLOCK - 1) <= qpos
    (p1,) = softmax_rows([_mm_nt(qn, kc)], [cmp_ok])
    o_cmp = _mm(p1, vc)
    gq = NSA_KV * dec_t
    imp = functools.reduce(jnp.add, [p1[h * gq:(h + 1) * gq] for h in range(NSA_HPG)])
    imps = _mm_hi(imp, mt_ref[...])
    blk = lane[0:gq]
    qp8 = qpos[0:gq]
    qblk = qp8 // SLC_BLOCK
    causal = blk * SLC_BLOCK <= qp8
    forced = (blk == 0) | (blk == qblk) | (blk == qblk - 1)
    score = jnp.where(causal, jnp.where(forced, jnp.inf, imps), -jnp.inf)
    n_slc = -(-(PAST_LEN + dec_t) // SLC_BLOCK)
    cnt = jnp.zeros((gq, LANE), F32)
    for j in range(n_slc):
        col = score[:, j:j + 1]
        cnt = cnt + jnp.where(blk > j, jnp.where(col >= score, 1.0, 0.0), jnp.where(col > score, 1.0, 0.0))
    sel = jnp.where(cnt < N_SELECT, jnp.where(score > -jnp.inf, 1.0, 0.0), 0.0)

    def sel_rows(j0):
        pick = jnp.where(lane[0:gq] < SLC_BLOCK, sel[:, j0:j0 + 1], sel[:, j0 + 1:j0 + 2])
        return jnp.concatenate([pick] * NSA_HPG, axis=0) > 0.5

    zeros_k = jnp.zeros((PAGE_SIZE - DEC_ROWS, LANE), F32)
    new_ok = (lane < dec_t) & (lane <= qi)
    kvn = kvn_ref[0]
    s_list = [_mm_nt(qr, pages[p][0, 0, :, 256:384]) for p in range(N_PAGES)]
    s_list.append(_mm_nt(qr, jnp.concatenate([kvn[:, 256:384], zeros_k], axis=0)))
    ok_list = [sel_rows(2 * p) for p in range(N_PAGES)]
    own = jnp.concatenate([sel[:, 2 * N_PAGES:2 * N_PAGES + 1]] * NSA_HPG, axis=0) > 0.5
    ok_list.append(new_ok & own)
    p_list = softmax_rows(s_list, ok_list)
    o_slc = functools.reduce(jnp.add, [_mm(p_list[p], pages[p][0, 0, :, 384:512]) for p in range(N_PAGES)])
    o_slc = o_slc + _mm(p_list[N_PAGES], jnp.concatenate([kvn[:, 384:512], zeros_k], axis=0))

    n_wt = WINDOW // PAGE_SIZE
    wn = wn_ref[0]
    s_list, ok_list = [], []
    for t in range(n_wt):
        s_list.append(_mm_nt(qr, cw_ref[0, 0, t * PAGE_SIZE:(t + 1) * PAGE_SIZE, 0:128]))
        ok_list.append(qpos - (PAST_LEN - WINDOW + t * PAGE_SIZE + lane) < WINDOW)
    s_list.append(_mm_nt(qr, jnp.concatenate([wn[:, 0:128], zeros_k], axis=0)))
    ok_list.append(new_ok)
    p_list = softmax_rows(s_list, ok_list)
    o_win = functools.reduce(jnp.add, [_mm(p_list[t], cw_ref[0, 0, t * PAGE_SIZE:(t + 1) * PAGE_SIZE, 128:256])
                                       for t in range(n_wt)])
    o_win = o_win + _mm(p_list[n_wt], jnp.concatenate([wn[:, 128:256], zeros_k], axis=0))
    g = g_ref[0]
    o_ref[0] = g[:, 0:128] * o_cmp + g[:, 128:256] * o_slc + g[:, 256:384] * o_win


def _nsa_decode_pallas(l, q, gates, cache_kv, page_table, kv_new, cache_win, win_new, pe, cw):
    b, tq = q.shape[:2]
    scale = HEAD_DIM ** -0.5
    eye = jnp.eye(NSA_KV, dtype=F32)

    def rows(x):
        x5 = x.reshape(b, tq, NSA_KV, NSA_HPG, HEAD_DIM).transpose(0, 3, 2, 1, 4)
        return jnp.einsum('bhgqd,gk->bhgqkd', x5, eye).reshape(b, DEC_QROWS, LANE).astype(BF16)

    qn = rows(q * scale)
    qr = rows(_rope_rows(q, PAST_LEN + jnp.arange(tq)) * scale)
    g5 = gates.reshape(b, tq, NSA_KV, NSA_HPG, 3).transpose(0, 3, 2, 1, 4)
    lane_g = jnp.repeat(eye, HEAD_DIM, axis=1)
    gate_b = jnp.einsum('bhgqj,gn->bhgqjn', g5, lane_g).reshape(b, DEC_QROWS, 3 * LANE)
    w_all, bias = _cmp_weights(pe, cw)
    n_cmp = PAST_LEN // CMP_STRIDE - 1
    n_slc = -(-(PAST_LEN + tq) // SLC_BLOCK)
    mt = np.zeros((LANE, LANE), np.float32)
    mt[:n_cmp, :n_slc] = _slc_cmp_matrix(n_slc, n_cmp, n_cmp).T
    pad_rows = lambda a: jnp.pad(a, ((0, 0), (0, DEC_ROWS - tq), (0, 0)))
    n_pool = cache_kv.shape[1]
    ckv = cache_kv.reshape(DEPTH, n_pool, PAGE_SIZE, 512)
    page_spec = lambda p: pl.BlockSpec((1, 1, PAGE_SIZE, 512), lambda bi, pt: (l, pt[bi, p], 0, 0))
    per_seq = lambda r, w: pl.BlockSpec((1, r, w), lambda bi, pt: (bi, 0, 0))
    const = lambda shape: pl.BlockSpec(shape, lambda bi, pt: (0,) * len(shape))
    grid_spec = pltpu.PrefetchScalarGridSpec(
        num_scalar_prefetch=1,
        grid=(b,),
        in_specs=[page_spec(p) for p in range(N_PAGES)]
                 + [per_seq(DEC_ROWS, 512),
                    pl.BlockSpec((1, 1, WINDOW, 256), lambda bi, pt: (l, bi, 0, 0)),
                    per_seq(DEC_ROWS, 256), per_seq(DEC_QROWS, LANE), per_seq(DEC_QROWS, LANE),
                    per_seq(DEC_QROWS, 3 * LANE), const((CMP_STRIDE * 256, 512)), const((1, 256)), const((LANE, LANE))],
        out_specs=per_seq(DEC_QROWS, LANE),
        scratch_shapes=[pltpu.VMEM((PAST_LEN, LANE), F32), pltpu.VMEM((PAST_LEN, LANE), F32)],
    )
    o = pl.pallas_call(
        functools.partial(_nsa_dec_body, tq),
        grid_spec=grid_spec,
        out_shape=jax.ShapeDtypeStruct((b, DEC_QROWS, LANE), F32),
        compiler_params=_cparams(("arbitrary",)),
        name="nsa_decode",
    )(page_table, *([ckv] * N_PAGES), pad_rows(kv_new), cache_win.reshape(DEPTH, b, WINDOW, 256), pad_rows(win_new),
      qn, qr, gate_b, w_all, bias, jnp.asarray(mt))
    o6 = o.reshape(b, NSA_HPG, NSA_KV, tq, NSA_KV, HEAD_DIM)
    return jnp.einsum('bhgqkd,gk->bqghd', o6, eye).reshape(b, tq, NSA_WIDTH)


def _rwkv_mix(cols, z, prev, s0, mu, w0, w_up, a0, a_up, k_k, k_a, r_k, ln_w):
    b, t = cols.shape[:2]
    shifted = jnp.concatenate([prev[:, None], cols[:, :-1]], axis=1)
    xx = cols + (shifted - cols) * mu
    sp = np.cumsum([RWKV_WIDTH, RWKV_WIDTH, RWKV_WIDTH, RWKV_LORA_W])
    r, k, v, wd, ad = jnp.split(xx, sp, axis=-1)
    w = -jax.nn.softplus(-(w0 + jnp.dot(jnp.tanh(wd), w_up, precision=HI))) - 0.5
    logw = -jnp.exp(w)
    a = jax.nn.sigmoid(a0 + jnp.dot(ad, a_up, precision=HI))

    def heads(u):
        return u.reshape(b, t, RWKV_HEADS, RWKV_HEAD)

    kk = heads(k * k_k)
    kk = kk / jnp.maximum(jnp.sqrt(jnp.sum(kk * kk, axis=-1, keepdims=True)), 1e-12)
    k = k * (1.0 + (a - 1.0) * k_a)
    r, k, v, logw, a = heads(r), heads(k), heads(v), heads(logw), heads(a)
    c = min(RWKV_CHUNK, t)
    nc = t // c

    def chunks(u):
        return u.reshape(b, nc, c, RWKV_HEADS, RWKV_HEAD).transpose(1, 0, 3, 2, 4)

    r, k, v, logw, kk, a = (chunks(u) for u in (r, k, v, logw, kk, a))
    cum = jnp.cumsum(logw, axis=3)
    w_in = jnp.exp(cum)
    w_ex = jnp.exp(cum - logw)
    w_end = w_in[..., -1:, :]
    alpha_t = -kk * w_ex
    beta_h = kk * a / w_in
    k_h = k / w_in
    r_t = r * w_in
    tri_s = jnp.tril(jnp.ones((c, c), F32), -1)
    tri_i = jnp.tril(jnp.ones((c, c), F32))
    mm = functools.partial(jnp.einsum, precision=HI)
    a_ab = mm('nbhtk,nbhsk->nbhts', alpha_t, beta_h) * tri_s
    a_ak = mm('nbhtk,nbhsk->nbhts', alpha_t, k_h) * tri_s
    a_rb = mm('nbhtk,nbhsk->nbhts', r_t, beta_h) * tri_i
    a_rk = mm('nbhtk,nbhsk->nbhts', r_t, k_h) * tri_i
    eye = jnp.eye(c, dtype=F32)
    tm = eye + a_ab
    pw = a_ab
    steps = 1
    while steps * 2 < c:
        pw = mm('nbhts,nbhsu->nbhtu', pw, pw)
        tm = mm('nbhts,nbhsu->nbhtu', tm, eye + pw)
        steps *= 2
    p_m = mm('nbhts,nbhsk->nbhtk', tm, alpha_t)
    q_m = mm('nbhts,nbhsv->nbhtv', tm, mm('nbhts,nbhsv->nbhtv', a_ak, v))
    b_t = beta_h * w_end
    k_t = k_h * w_end
    m_m = mm('nbhtk,nbhtj->nbhkj', p_m, b_t)
    n_m = mm('nbhtv,nbhtk->nbhvk', q_m, b_t) + mm('nbhtv,nbhtk->nbhvk', v, k_t)
    r_p = r_t + mm('nbhts,nbhsk->nbhtk', a_rb, p_m)
    o_p = mm('nbhts,nbhsv->nbhtv', a_rb, q_m) + mm('nbhts,nbhsv->nbhtv', a_rk, v)

    def step(s, inp):
        m_c, n_c, we_c, rp_c, op_c = inp
        o = mm('bhtk,bhvk->bhtv', rp_c, s) + op_c
        s = s * we_c + mm('bhvk,bhkj->bhvj', s, m_c) + n_c
        return s, o

    s_t, o = lax.scan(step, s0, (m_m, n_m, w_end, r_p, o_p))
    o = o.transpose(1, 0, 3, 2, 4).reshape(b, t, RWKV_HEADS, RWKV_HEAD)
    r, k, v = (u.transpose(1, 0, 3, 2, 4).reshape(b, t, RWKV_HEADS, RWKV_HEAD) for u in (r, k, v))
    mean = jnp.mean(o, axis=-1, keepdims=True)
    var = jnp.mean(jnp.square(o - mean), axis=-1, keepdims=True)
    o = (o - mean) * lax.rsqrt(var + GN_EPS) * ln_w.reshape(RWKV_HEADS, RWKV_HEAD)
    o = o + jnp.sum(r * k * r_k, axis=-1, keepdims=True) * v
    y = o.reshape(b, t, RWKV_WIDTH) * jax.nn.silu(z)
    return y, cols[:, -1], s_t


def _hgrn_mix(q, fz, i, z, s0, lb, ln_w):
    b, t = q.shape[:2]
    log_f = jnp.logaddexp(jnp.log(lb), jnp.log1p(-lb) + jax.nn.log_sigmoid(fz))
    kf = (1.0 - lb) * jax.nn.sigmoid(-fz)
    chunk = HGRN_CHUNK if t % HGRN_CHUNK == 0 else t
    nc = t // chunk

    def chunks(u):
        return u.reshape(b, nc, chunk, HGRN_HEADS, HGRN_HEAD).transpose(1, 0, 3, 2, 4)

    mask = jnp.tril(jnp.ones((chunk, chunk), dtype=bool))

    def step(s, inp):
        q_c, lf_c, k_c, i_c = inp
        cl = jnp.cumsum(lf_c, axis=2)
        dec = jnp.exp(jnp.where(mask[:, :, None], cl[:, :, :, None, :] - cl[:, :, None, :, :], -jnp.inf))
        att = jnp.einsum('bhtk,bhtsk,bhsk->bhts', q_c, dec, k_c)
        o = jnp.einsum('bhts,bhsv->bhtv', att, i_c) + jnp.einsum('bhtk,bhkv->bhtv', q_c * jnp.exp(cl), s)
        cl_end = cl[:, :, -1:]
        s = jnp.exp(cl_end[:, :, 0])[..., None] * s + jnp.einsum('bhsk,bhsv->bhkv', k_c * jnp.exp(cl_end - cl), i_c)
        return s, o

    s_t, o = lax.scan(step, s0, (chunks(q), chunks(log_f), chunks(kf), chunks(i)))
    o = o.transpose(1, 0, 3, 2, 4).reshape(b, t, HGRN_HEADS, HGRN_HEAD)
    o = o * lax.rsqrt(jnp.mean(o * o, axis=-1, keepdims=True) + RMS_EPS) * ln_w.reshape(HGRN_HEADS, HGRN_HEAD)
    y = o.reshape(b, t, HGRN_WIDTH) * jax.nn.silu(z)
    return y, s_t


def _mm(a, b):
    return jnp.dot(a.astype(BF16), b.astype(BF16), preferred_element_type=F32)


def _mm_nt(a, b):
    return lax.dot_general(a.astype(BF16), b.astype(BF16), (((1,), (1,)), ((), ())), preferred_element_type=F32)


def _mm_tn(a, b):
    return lax.dot_general(a.astype(BF16), b.astype(BF16), (((0,), (0,)), ((), ())), preferred_element_type=F32)


def _mm_hi(a, b):
    return jnp.dot(a, b, preferred_element_type=F32, precision=HI)


def _mm_ones(a, ones_bf16):
    hi = a.astype(BF16)
    lo = (a - hi.astype(F32)).astype(BF16)
    return (jnp.dot(hi, ones_bf16, preferred_element_type=F32) + jnp.dot(lo, ones_bf16, preferred_element_type=F32))


def _log_sigmoid(x):
    return jnp.minimum(x, 0.0) - jnp.log1p(jnp.exp(-jnp.abs(x)))


HGRN_SUB = 16
MIX_TILE = 512


def _hgrn_body(n_t, q_ref, f_ref, i_ref, z_ref, lbc_ref, lnw_ref, y_ref, s_ref, st_ref, o_ref):
    ti = pl.program_id(2)
    c, sc = HGRN_CHUNK, HGRN_SUB

    @pl.when(ti == 0)
    def _():
        st_ref[...] = jnp.zeros(st_ref.shape, F32)

    log_lb, log_1m_lb, one_m_lb = lbc_ref[0:1, :], lbc_ref[1:2, :], lbc_ref[2:3, :]
    tri = (lax.broadcasted_iota(jnp.int32, (c, c), 0) >= lax.broadcasted_iota(jnp.int32, (c, c), 1)).astype(F32)
    sub_s = lax.broadcasted_iota(jnp.int32, (sc, LANE), 0)

    def chunk(ci, carry):
        r0 = pl.multiple_of(ci * c, c)
        q = q_ref[0, pl.ds(r0, c), :]
        fz = f_ref[0, pl.ds(r0, c), :]
        iv = i_ref[0, pl.ds(r0, c), :]
        b_ = log_1m_lb + _log_sigmoid(fz)
        lf = jnp.maximum(log_lb, b_) + jnp.log1p(jnp.exp(-jnp.abs(log_lb - b_)))
        kf = one_m_lb * jax.nn.sigmoid(-fz)
        cl = _mm_hi(tri, lf)
        st = st_ref[...]
        o = _mm_nt(q * jnp.exp(cl), st)
        pieces = []
        for blk in range(c // sc):
            lo = blk * sc
            cl_r, q_r, k_r, i_r = cl[lo:lo + sc], q[lo:lo + sc], kf[lo:lo + sc], iv[lo:lo + sc]
            acc = o[lo:lo + sc]
            if blk > 0:
                bnd = cl[lo - 1:lo]
                att = _mm_nt(q_r * jnp.exp(cl_r - bnd), kf[:lo] * jnp.exp(bnd - cl[:lo]))
                acc = acc + _mm(att, iv[:lo])
            rows = []
            for t in range(sc):
                w = jnp.where(sub_s <= t, jnp.exp(cl_r[t:t + 1] - cl_r), 0.0)
                d = jnp.sum(q_r[t:t + 1] * w * k_r, axis=1, keepdims=True)
                rows.append(jnp.sum(d * i_r, axis=0, keepdims=True))
            pieces.append(acc + jnp.concatenate(rows, axis=0))
        o_ref[pl.ds(r0, c), :] = jnp.concatenate(pieces, axis=0)
        c_end = cl[c - 1:c]
        st_ref[...] = st * jnp.exp(c_end) + _mm_tn(iv, kf * jnp.exp(c_end - cl))
        return carry

    lax.fori_loop(0, o_ref.shape[0] // c, chunk, 0)
    o = o_ref[...]
    z = z_ref[0]
    y_ref[0] = o * lax.rsqrt(jnp.mean(o * o, axis=-1, keepdims=True) + RMS_EPS) * lnw_ref[...] * (z * jax.nn.sigmoid(z))

    @pl.when(ti == n_t - 1)
    def _():
        s_ref[0, 0] = st_ref[...].T


def _hgrn_prompt(proj_a3, lb, ln_w):
    b, t = proj_a3.shape[:2]
    tt = min(MIX_TILE, t)
    n_t = t // tt
    lbc = jnp.concatenate([jnp.log(lb)[None], jnp.log1p(-lb)[None], (1.0 - lb)[None],
                           jnp.zeros((5, HGRN_WIDTH), F32)], axis=0)
    col = lambda off: pl.BlockSpec((1, tt, LANE), lambda bi, h, i: (bi, i, off // LANE + h))
    return pl.pallas_call(
        functools.partial(_hgrn_body, n_t),
        grid=(b, HGRN_HEADS, n_t),
        in_specs=[col(A_HQ), col(A_HF), col(A_HI), col(A_HZ),
                  pl.BlockSpec((8, LANE), lambda bi, h, i: (0, h)),
                  pl.BlockSpec((1, LANE), lambda bi, h, i: (0, h))],
        out_specs=[pl.BlockSpec((1, tt, LANE), lambda bi, h, i: (bi, i, h)),
                   pl.BlockSpec((1, 1, HGRN_HEAD, HGRN_HEAD), lambda bi, h, i: (bi, h, 0, 0))],
        out_shape=[jax.ShapeDtypeStruct((b, t, HGRN_WIDTH), F32),
                   jax.ShapeDtypeStruct((b, HGRN_HEADS, HGRN_HEAD, HGRN_HEAD), F32)],
        scratch_shapes=[pltpu.VMEM((HGRN_HEAD, HGRN_HEAD), F32), pltpu.VMEM((tt, LANE), F32)],
        compiler_params=_cparams(("arbitrary", "arbitrary", "arbitrary")),
        name="hgrn_prompt",
    )(proj_a3, proj_a3, proj_a3, proj_a3, lbc, ln_w.reshape(1, HGRN_WIDTH))


def _rwkv_body(n_t, cols_ref, z_ref, mu_ref, w0_ref, wup_ref, a0_ref, aup_ref, kk_ref, ka_ref, rk_ref, lnw_ref, bd_ref,
               y_ref, s_ref, prev_ref, st_ref, r_s, k_s, v_s, lw_s, kk_s, a_s, o_s):
    ti = pl.program_id(1)
    tt = cols_ref.shape[1]
    c = RWKV_CHUNK
    n_pair = RWKV_HEADS // 2

    @pl.when(ti == 0)
    def _():
        prev_ref[...] = jnp.zeros(prev_ref.shape, F32)
        st_ref[...] = jnp.zeros(st_ref.shape, F32)

    x = cols_ref[0]
    first = lax.broadcasted_iota(jnp.int32, (tt, 1), 0) == 0
    x_prev = jnp.where(first, prev_ref[0:1, :], pltpu.roll(x, 1, 0))
    prev_ref[0:1, :] = x[tt - 1:tt, :]
    xx = x + (x_prev - x) * mu_ref[...]
    r, k, v = xx[:, 0:RWKV_WIDTH], xx[:, RWKV_WIDTH:2 * RWKV_WIDTH], xx[:, 2 * RWKV_WIDTH:3 * RWKV_WIDTH]
    lora = xx[:, 3 * RWKV_WIDTH:]
    nx = -(w0_ref[...] + _mm_hi(jnp.tanh(lora), wup_ref[...]))
    w_log = -(jnp.maximum(nx, 0.0) + jnp.log1p(jnp.exp(-jnp.abs(nx)))) - 0.5
    lw_s[...] = -jnp.exp(w_log)
    a = jax.nn.sigmoid(a0_ref[...] + _mm_hi(lora, aup_ref[...]))
    bd = bd_ref[...]
    kk_raw = k * kk_ref[...]
    kk_s[...] = kk_raw / jnp.maximum(jnp.sqrt(_mm_ones(kk_raw * kk_raw, bd)), 1e-12)
    k2 = k * (1.0 + (a - 1.0) * ka_ref[...])
    r_s[...] = r
    k_s[...] = k2
    v_s[...] = v
    a_s[...] = a
    bonus = _mm_ones(r * k2 * rk_ref[...], bd) * v

    rows = lax.broadcasted_iota(jnp.int32, (c, LANE), 0)
    lane_in = lax.broadcasted_iota(jnp.int32, (c, LANE), 1) % RWKV_HEAD
    strict = rows > lane_in
    incl = rows >= lane_in
    eye2 = (rows == lane_in).astype(F32)
    tri = (lax.broadcasted_iota(jnp.int32, (c, c), 0) >= lax.broadcasted_iota(jnp.int32, (c, c), 1)).astype(F32)
    lane1 = lax.broadcasted_iota(jnp.int32, (1, LANE), 1)
    m0, m1 = lane1 < RWKV_HEAD, lane1 >= RWKV_HEAD
    bdm = (lax.broadcasted_iota(jnp.int32, (LANE, LANE), 0) // RWKV_HEAD
           == lax.broadcasted_iota(jnp.int32, (LANE, LANE), 1) // RWKV_HEAD)

    def blockdiag(zz):
        zz = zz.astype(BF16)
        return jnp.concatenate([jnp.where(m0, zz, 0), jnp.where(m1, zz, 0)], axis=0)

    def chunk(ci, carry):
        r0 = pl.multiple_of(ci * c, c)
        ds = pl.ds(r0, c)
        lw = lw_s[ds, :]
        cum = _mm_hi(tri, lw)
        w_in, w_ex, inv_in = jnp.exp(cum), jnp.exp(cum - lw), jnp.exp(-cum)
        w_end = jnp.exp(cum[c - 1:c])
        kk_c, vv = kk_s[ds, :], v_s[ds, :]
        alpha = -kk_c * w_ex
        beta_h = kk_c * a_s[ds, :] * inv_in
        k_h = k_s[ds, :] * inv_in
        r_t = r_s[ds, :] * w_in
        beta_e, k_e = beta_h * w_end, k_h * w_end
        pairs = range(n_pair)
        sls = [slice(p * LANE, (p + 1) * LANE) for p in pairs]
        al = [alpha[:, sl] for sl in sls]
        rt = [r_t[:, sl] for sl in sls]
        vb = [blockdiag(vv[:, sl]) for sl in sls]
        aa = [_mm_nt(jnp.concatenate([al[p], rt[p]], axis=0),
                     jnp.concatenate([blockdiag(beta_h[:, sls[p]]), blockdiag(k_h[:, sls[p]])], axis=0))
              for p in pairs]
        a_ab = [jnp.where(strict, aa[p][0:c, 0:LANE], 0.0) for p in pairs]
        a_ak = [jnp.where(strict, aa[p][0:c, LANE:], 0.0) for p in pairs]
        a_rb = [jnp.where(incl, aa[p][c:, 0:LANE], 0.0) for p in pairs]
        a_rk = [jnp.where(incl, aa[p][c:, LANE:], 0.0) for p in pairs]
        akv = [_mm(a_ak[p], vb[p]) for p in pairs]
        rkv = [_mm(a_rk[p], vb[p]) for p in pairs]
        vk = [_mm_tn(vv[:, sls[p]], k_e[:, sls[p]]) for p in pairs]
        tm = [eye2 + a_ab[p] for p in pairs]
        pw = a_ab
        n = 1
        while 2 * n < c:
            pw = [_mm(pw[p], blockdiag(pw[p])) for p in pairs]
            tm = [tm[p] + _mm(tm[p], blockdiag(pw[p])) for p in pairs]
            n *= 2
        pq = [_mm(tm[p], jnp.concatenate([blockdiag(al[p]), blockdiag(akv[p])], axis=1)) for p in pairs]
        ro = [_mm(a_rb[p], jnp.concatenate([blockdiag(pq[p][:, 0:LANE]), blockdiag(pq[p][:, LANE:])], axis=1))
              for p in pairs]
        mn = [_mm_tn(pq[p], beta_e[:, sls[p]]) for p in pairs]
        outs = []
        for p in pairs:
            m_bd = jnp.where(bdm, mn[p][0:LANE], 0.0)
            n_f = jnp.where(bdm, mn[p][LANE:] + vk[p], 0.0)
            st = st_ref[p]
            outs.append(_mm_nt(rt[p] + ro[p][:, 0:LANE], blockdiag(st)) + ro[p][:, LANE:] + rkv[p])
            st_ref[p] = st * w_end[:, sls[p]] + _mm(st, m_bd) + n_f[0:RWKV_HEAD] + n_f[RWKV_HEAD:]
        o_s[ds, :] = jnp.concatenate(outs, axis=1)
        return carry

    lax.fori_loop(0, tt // c, chunk, 0)
    o = o_s[...]
    inv_n = 1.0 / RWKV_HEAD
    d = o - _mm_ones(o, bd) * inv_n
    var = _mm_ones(d * d, bd) * inv_n
    z = z_ref[0]
    y_ref[0] = (d * lax.rsqrt(var + GN_EPS) * lnw_ref[...] + bonus) * (z * jax.nn.sigmoid(z))

    @pl.when(ti == n_t - 1)
    def _():
        s_ref[0] = st_ref[...]


def _rwkv_prompt(rcols, proj_a3, mu, w0, w_up, a0, a_up, k_k, k_a, r_k, ln_w):
    b, t = rcols.shape[:2]
    tt = min(MIX_TILE, t)
    n_t = t // tt
    n_pair = RWKV_HEADS // 2
    zpad = jnp.zeros((RWKV_LORA_W, RWKV_WIDTH), F32)
    wup = jnp.concatenate([w_up, zpad], axis=0)
    aup = jnp.concatenate([zpad, a_up], axis=0)
    hid = np.arange(RWKV_WIDTH) // RWKV_HEAD
    bd = jnp.asarray((hid[:, None] == hid[None, :]).astype(np.float32)).astype(BF16)
    vec = lambda a: a.reshape(1, -1)
    full = lambda shape: pl.BlockSpec(shape, lambda bi, i: (0,) * len(shape))
    y, s = pl.pallas_call(
        functools.partial(_rwkv_body, n_t),
        grid=(b, n_t),
        in_specs=[pl.BlockSpec((1, tt, SHIFT_W), lambda bi, i: (bi, i, 0)),
                  pl.BlockSpec((1, tt, RWKV_WIDTH), lambda bi, i: (bi, i, A_RZ // RWKV_WIDTH)),
                  full((1, SHIFT_W)), full((1, RWKV_WIDTH)), full((LANE, RWKV_WIDTH)), full((1, RWKV_WIDTH)),
                  full((LANE, RWKV_WIDTH)), full((1, RWKV_WIDTH)), full((1, RWKV_WIDTH)), full((1, RWKV_WIDTH)),
                  full((1, RWKV_WIDTH)), full((RWKV_WIDTH, RWKV_WIDTH))],
        out_specs=[pl.BlockSpec((1, tt, RWKV_WIDTH), lambda bi, i: (bi, i, 0)),
                   pl.BlockSpec((1, n_pair, RWKV_HEAD, LANE), lambda bi, i: (bi, 0, 0, 0))],
        out_shape=[jax.ShapeDtypeStruct((b, t, RWKV_WIDTH), F32),
                   jax.ShapeDtypeStruct((b, n_pair, RWKV_HEAD, LANE), F32)],
        scratch_shapes=[pltpu.VMEM((8, SHIFT_W), F32), pltpu.VMEM((n_pair, RWKV_HEAD, LANE), F32)]
                       + [pltpu.VMEM((tt, RWKV_WIDTH), F32)] * 7,
        compiler_params=_cparams(("arbitrary", "arbitrary")),
        name="rwkv_prompt",
    )(rcols, proj_a3, vec(mu), vec(w0), wup, vec(a0), aup, vec(k_k), vec(k_a), vec(r_k), vec(ln_w), bd)
    s = s.reshape(b, n_pair, RWKV_HEAD, 2, RWKV_HEAD).transpose(0, 1, 3, 2, 4)
    return y, s.reshape(b, RWKV_HEADS, RWKV_HEAD, RWKV_HEAD)


def _merge_body(x_ref, yn_ref, nz_ref, yr_ref, yh_ref, mg_ref, wb_ref, wo_ref, o_ref):
    nz = nz_ref[...]
    branches = (yn_ref[...] * (nz * jax.nn.sigmoid(nz)), yr_ref[...], yh_ref[...])
    acc = jnp.zeros(o_ref.shape, F32)
    for n, y in enumerate(branches):
        t = jnp.dot(y.astype(BF16), wb_ref[n], preferred_element_type=F32)
        acc = acc + jax.nn.sigmoid(mg_ref[:, n * D_MODEL:(n + 1) * D_MODEL]) * t
    o_ref[...] = x_ref[...] + jnp.dot(acc.astype(BF16), wo_ref[...], preferred_element_type=F32)


def _merge(x2d, y_nsa, proj_a, y_rwkv, y_hgrn, mg, wb, wo, tm):
    m = x2d.shape[0]
    row = lambda w: pl.BlockSpec((tm, w), lambda i: (i, 0))
    return pl.pallas_call(
        _merge_body,
        grid=(m // tm,),
        in_specs=[row(D_MODEL), row(BRANCH_WIDTH),
                  pl.BlockSpec((tm, BRANCH_WIDTH), lambda i: (i, A_NZ // BRANCH_WIDTH)),
                  row(BRANCH_WIDTH), row(BRANCH_WIDTH), row(N_BRANCH * D_MODEL),
                  pl.BlockSpec((N_BRANCH, BRANCH_WIDTH, D_MODEL), lambda i: (0, 0, 0)),
                  pl.BlockSpec((D_MODEL, D_MODEL), lambda i: (0, 0))],
        out_specs=row(D_MODEL),
        out_shape=jax.ShapeDtypeStruct((m, D_MODEL), F32),
        compiler_params=_cparams(("arbitrary",)),
        name="merge",
    )(x2d, y_nsa, proj_a, y_rwkv, y_hgrn, mg, wb, wo)


def _split_w_in(w):
    o = np.concatenate([[0], np.cumsum(IN_SIZES)])
    seg = lambda n: w[:, o[n]:o[n + 1]]
    pad = jnp.zeros((w.shape[0], A_Q - A_GATE - IN_SIZES[2]), w.dtype)
    wa = jnp.concatenate([seg(1), seg(2), pad, seg(0), seg(3), seg(5), seg(6), seg(7), seg(8), seg(9)], axis=1)
    return wa.astype(BF16), seg(4).astype(BF16), seg(10).astype(BF16)


def _layer(x, l, past, prm, lb, tabs, tm):
    b, t = x.shape[:2]
    m = b * t
    x2d = x.reshape(m, D_MODEL)
    nw = prm['norm_w'][l].reshape(1, D_MODEL)
    wa, wb_cols, wc = prm['w_split'][l]
    proj_a = _inproj(x2d, nw, wa, tm, 1536)
    rcols = _inproj(x2d, nw, wb_cols, tm, SHIFT_W).reshape(b, t, SHIFT_W)
    mg = _inproj(x2d, nw, wc, tm, 1536)
    seg = lambda off, w: proj_a[:, off:off + w].reshape(b, t, w)
    c_tab, su_tab, sd_tab, cos_t, sin_t = tabs
    if past is None:
        kv_new, win_new, kcsrc, ksb, vst, kwb, vwt = _kvpost(proj_a, (c_tab, su_tab, sd_tab), b, t, min(tm, t), True)
        w_all, bias = _cmp_weights(prm['nsa_cmp_pe'][l], prm['nsa_cmp_w'][l])
        kc, vct = _compress(kcsrc, w_all, bias, b, t)
        gate_t = seg(A_GATE, 3 * NSA_HEADS).transpose(0, 2, 1)
        y_nsa = _nsa_prompt(proj_a, gate_t, cos_t, sin_t, kc, vct, ksb, vst, kwb, vwt, b, t)
        win_state = win_new.reshape(b, t, 2, NSA_KV, HEAD_DIM)[:, -min(WINDOW, t):]
    else:
        cache_kv, page_table, cache_win, s_r, prev, s_h = past
        kv_new, win_new = _kvpost(proj_a, (c_tab, su_tab, sd_tab), b, t, tm, False)
        win5 = win_new.reshape(b, t, 2, NSA_KV, HEAD_DIM)
        win_state = jnp.concatenate([cache_win[l][:, t:], win5], axis=1)
        gates = jax.nn.sigmoid(seg(A_GATE, 3 * NSA_HEADS)).reshape(b, t, NSA_HEADS, 3)
        q = seg(A_Q, NSA_WIDTH).reshape(b, t, NSA_HEADS, HEAD_DIM)
        y_nsa = _nsa_decode_pallas(l, q, gates, cache_kv, page_table, kv_new.reshape(b, t, 512), cache_win,
                                   win_new.reshape(b, t, 256), prm['nsa_cmp_pe'][l], prm['nsa_cmp_w'][l])
    rwkv_prm = (prm['rwkv_mu'][l], prm['rwkv_w0'][l], prm['rwkv_w_up'][l], prm['rwkv_a0'][l], prm['rwkv_a_up'][l],
                prm['rwkv_k_k'][l], prm['rwkv_k_a'][l], prm['rwkv_r_k'][l], prm['rwkv_ln_w'][l])
    if past is None:
        proj_a3 = proj_a.reshape(b, t, A_WIDTH)
        y_rwkv, s_r = _rwkv_prompt(rcols, proj_a3, *rwkv_prm)
        shift_state = rcols[:, -1]
        y_hgrn, s_h = _hgrn_prompt(proj_a3, lb, prm['hgrn_ln_w'][l])
    else:
        y_rwkv, shift_state, s_r = _rwkv_mix(rcols, seg(A_RZ, RWKV_WIDTH), prev, s_r, *rwkv_prm)
        y_hgrn, s_h = _hgrn_mix(seg(A_HQ, HGRN_WIDTH), seg(A_HF, HGRN_WIDTH), seg(A_HI, HGRN_WIDTH),
                                seg(A_HZ, HGRN_WIDTH), s_h, lb, prm['hgrn_ln_w'][l])
    x_new = _merge(x2d, y_nsa.reshape(m, NSA_WIDTH), proj_a, y_rwkv.reshape(m, RWKV_WIDTH), y_hgrn.reshape(m, HGRN_WIDTH),
                   mg, prm['w_branch_bf16'][l], prm['w_out_bf16'][l], min(tm, 512))
    kv_state = kv_new.reshape(b, t, 4, NSA_KV, HEAD_DIM)
    return x_new.reshape(b, t, D_MODEL), (kv_state, win_state, s_r, shift_state, s_h)


def kernel(x_prompt, x_sample, cache_kv, cache_win, state_rwkv, state_shift, state_hgrn, page_table,
           norm_w, w_in, nsa_cmp_pe, nsa_cmp_w, rwkv_mu, rwkv_w0, rwkv_w_up, rwkv_a0, rwkv_a_up,
           rwkv_k_k, rwkv_k_a, rwkv_r_k, rwkv_ln_w, hgrn_lb_logits, hgrn_ln_w, w_branch, w_out, norm_f):
    prm = {'norm_w': norm_w, 'nsa_cmp_pe': nsa_cmp_pe, 'nsa_cmp_w': nsa_cmp_w,
           'rwkv_mu': rwkv_mu, 'rwkv_w0': rwkv_w0, 'rwkv_w_up': rwkv_w_up, 'rwkv_a0': rwkv_a0,
           'rwkv_a_up': rwkv_a_up, 'rwkv_k_k': rwkv_k_k, 'rwkv_k_a': rwkv_k_a, 'rwkv_r_k': rwkv_r_k,
           'rwkv_ln_w': rwkv_ln_w, 'hgrn_ln_w': hgrn_ln_w,
           'w_split': [_split_w_in(w_in[l]) for l in range(DEPTH)],
           'w_branch_bf16': w_branch.astype(BF16), 'w_out_bf16': w_out.astype(BF16)}
    cs = jnp.cumsum(jax.nn.softmax(hgrn_lb_logits, axis=0), axis=0)
    lbs = cs - cs[0:1]
    n_dec, dec_t = x_sample.shape[:2]
    bp, tp = x_prompt.shape[:2]
    tabs_p = _rope_tables(jnp.arange(tp))
    tabs_s = _rope_tables(PAST_LEN + jnp.arange(n_dec * dec_t) % dec_t)
    tm_p = min(1024, bp * tp)
    tm_s = n_dec * dec_t
    xp, xs = x_prompt, x_sample
    outs = [[] for _ in range(10)]
    for l in range(DEPTH):
        xp, st_p = _layer(xp, l, None, prm, lbs[l], tabs_p, tm_p)
        past = (cache_kv, page_table, cache_win, state_rwkv[l], state_shift[l], state_hgrn[l])
        xs, st_s = _layer(xs, l, past, prm, lbs[l], tabs_s, tm_s)
        for n in range(5):
            outs[2 * n].append(st_p[n])
            outs[2 * n + 1].append(st_s[n])
    y_prompt = _final_norm(xp.reshape(bp * tp, D_MODEL), norm_f, tm_p).reshape(xp.shape)
    y_sample = _final_norm(xs.reshape(n_dec * dec_t, D_MODEL), norm_f, tm_s).reshape(xs.shape)
    return (y_prompt, y_sample) + tuple(jnp.stack(o) for o in outs)
```

```python
import functools

import jax
import jax.numpy as jnp
import numpy as np
from jax import lax
from jax.experimental import pallas as pl
from jax.experimental.pallas import tpu as pltpu

F32 = jnp.float32
BF16 = jnp.bfloat16
HI = lax.Precision.HIGHEST

D_MODEL = 1024
DEPTH = 4
PAST_LEN = 2048
PAGE_SIZE = 128
NSA_HEADS = 8
NSA_KV = 2
HEAD_DIM = 64
NSA_HPG = NSA_HEADS // NSA_KV
NSA_WIDTH = NSA_HEADS * HEAD_DIM
ROT_DIM = HEAD_DIM // 4
ROPE_THETA = 500000.0
CMP_BLOCK = 32
CMP_STRIDE = 16
SLC_BLOCK = 64
N_SELECT = 16
WINDOW = 512
Q_BLOCK = 128
RWKV_HEADS = 8
RWKV_HEAD = 64
RWKV_WIDTH = RWKV_HEADS * RWKV_HEAD
RWKV_LORA_W = 64
RWKV_LORA_A = 64
SHIFT_W = 3 * RWKV_WIDTH + RWKV_LORA_W + RWKV_LORA_A
HGRN_HEADS = 4
HGRN_HEAD = 128
HGRN_WIDTH = HGRN_HEADS * HGRN_HEAD
HGRN_CHUNK = 64
N_BRANCH = 3
BRANCH_WIDTH = 512
KV_COLS = 6 * NSA_KV * HEAD_DIM
IN_SIZES = (NSA_WIDTH, KV_COLS, 3 * NSA_HEADS, NSA_WIDTH, SHIFT_W, RWKV_WIDTH,
            HGRN_WIDTH, HGRN_WIDTH, HGRN_WIDTH, HGRN_WIDTH, N_BRANCH * D_MODEL)
RMS_EPS = 1e-6
GN_EPS = 64e-5

LANE = 128
VMEM_LIMIT = 56 * 1024 * 1024
NEG = -1e30
KEY_TILE = 512
ATT_UNROLL = 1
LOG2E = 1.4426950408889634
RWKV_CHUNK = 64

A_KV = 0
A_GATE = 768
A_Q = 1024
A_NZ = 1536
A_RZ = 2048
A_HQ = 2560
A_HF = 3072
A_HI = 3584
A_HZ = 4096
A_WIDTH = 4608


def _cparams(sem):
    return pltpu.CompilerParams(dimension_semantics=sem, vmem_limit_bytes=VMEM_LIMIT)


def _inproj_body(x_ref, nw_ref, w_ref, o_ref, h_ref):
    @pl.when(pl.program_id(1) == 0)
    def _():
        x = x_ref[...]
        ms = jnp.mean(x * x, axis=-1, keepdims=True)
        h_ref[...] = (x * lax.rsqrt(ms + RMS_EPS) * nw_ref[...]).astype(BF16)

    o_ref[...] = jnp.dot(h_ref[...], w_ref[...], preferred_element_type=F32)


def _inproj(x2d, nw, w, tm, tn):
    m, d = x2d.shape
    n = w.shape[1]
    return pl.pallas_call(
        _inproj_body,
        grid=(m // tm, n // tn),
        in_specs=[pl.BlockSpec((tm, d), lambda i, j: (i, 0)),
                  pl.BlockSpec((1, d), lambda i, j: (0, 0)),
                  pl.BlockSpec((d, tn), lambda i, j: (0, j))],
        out_specs=pl.BlockSpec((tm, tn), lambda i, j: (i, j)),
        out_shape=jax.ShapeDtypeStruct((m, n), F32),
        scratch_shapes=[pltpu.VMEM((tm, d), BF16)],
        compiler_params=_cparams(("arbitrary", "arbitrary")),
        name="inproj",
    )(x2d, nw, w)


def _rmsnorm_body(x_ref, w_ref, o_ref):
    x = x_ref[...]
    ms = jnp.mean(x * x, axis=-1, keepdims=True)
    o_ref[...] = x * lax.rsqrt(ms + RMS_EPS) * w_ref[...]


def _final_norm(x2d, w, tm):
    m, d = x2d.shape
    return pl.pallas_call(
        _rmsnorm_body,
        grid=(m // tm,),
        in_specs=[pl.BlockSpec((tm, d), lambda i: (i, 0)), pl.BlockSpec((1, d), lambda i: (0, 0))],
        out_specs=pl.BlockSpec((tm, d), lambda i: (i, 0)),
        out_shape=jax.ShapeDtypeStruct((m, d), F32),
        compiler_params=_cparams(("arbitrary",)),
        name="final_norm",
    )(x2d, w.reshape(1, d))


def _rope_tables(pos):
    half = ROT_DIM // 2
    inv = ROPE_THETA ** (-jnp.arange(0, ROT_DIM, 2, dtype=F32) / ROT_DIM)
    ang = pos.astype(F32)[:, None] * inv[None, :]
    cos, sin = jnp.cos(ang), jnp.sin(ang)
    n = pos.shape[0]
    ones = jnp.ones((n, HEAD_DIM - ROT_DIM), F32)
    zeros8 = jnp.zeros((n, half), F32)
    zeros = jnp.zeros((n, HEAD_DIM - ROT_DIM), F32)
    c = jnp.concatenate([cos, cos, ones], axis=1)
    s_up = jnp.concatenate([zeros8, sin, zeros], axis=1)
    s_dn = jnp.concatenate([-sin, zeros8, zeros], axis=1)
    tile = lambda a: jnp.concatenate([a] * NSA_KV, axis=1)
    return tile(c), tile(s_up), tile(s_dn), cos.T, sin.T


def _kvpost_body(with_attn, p_ref, c_ref, su_ref, sd_ref, kv_ref, win_ref, *extra):
    x = p_ref[...]
    c, su, sd = c_ref[...], su_ref[...], sd_ref[...]

    def rope(v):
        return v * c + pltpu.roll(v, ROT_DIM // 2, 1) * su + pltpu.roll(v, LANE - ROT_DIM // 2, 1) * sd

    k_slc = rope(x[:, 256:384])
    k_win = rope(x[:, 512:640])
    kv_ref[:, 0:256] = x[:, 0:256]
    kv_ref[:, 256:384] = k_slc
    kv_ref[:, 384:512] = x[:, 384:512]
    win_ref[:, 0:128] = k_win
    win_ref[:, 128:256] = x[:, 640:768]
    if with_attn:
        kc_ref, ks_ref, vst_ref, kw_ref, vwt_ref = extra
        kc_ref[...] = x[:, 0:256].astype(BF16)
        ks_ref[...] = k_slc.astype(BF16)
        vst_ref[0] = x[:, 384:512].T.astype(BF16)
        kw_ref[...] = k_win.astype(BF16)
        vwt_ref[0] = x[:, 640:768].T.astype(BF16)


def _kvpost(proj_a, tabs, b, t, tm, with_attn):
    m = b * t
    nt = tabs[0].shape[0] // tm
    tab_spec = pl.BlockSpec((tm, LANE), lambda i: (i % nt, 0))
    out_shape = [jax.ShapeDtypeStruct((m, 512), F32), jax.ShapeDtypeStruct((m, 256), F32)]
    out_specs = [pl.BlockSpec((tm, 512), lambda i: (i, 0)), pl.BlockSpec((tm, 256), lambda i: (i, 0))]
    if with_attn:
        out_shape += [jax.ShapeDtypeStruct((m, 256), BF16), jax.ShapeDtypeStruct((m, LANE), BF16),
                      jax.ShapeDtypeStruct((b, LANE, t), BF16), jax.ShapeDtypeStruct((m, LANE), BF16),
                      jax.ShapeDtypeStruct((b, LANE, t), BF16)]
        tspec = pl.BlockSpec((1, LANE, tm), lambda i: (i // nt, 0, i % nt))
        out_specs += [pl.BlockSpec((tm, 256), lambda i: (i, 0)), pl.BlockSpec((tm, LANE), lambda i: (i, 0)),
                      tspec, pl.BlockSpec((tm, LANE), lambda i: (i, 0)), tspec]
    return pl.pallas_call(
        functools.partial(_kvpost_body, with_attn),
        grid=(m // tm,),
        in_specs=[pl.BlockSpec((tm, KV_COLS), lambda i: (i, 0)), tab_spec, tab_spec, tab_spec],
        out_specs=out_specs,
        out_shape=out_shape,
        compiler_params=_cparams(("arbitrary",)),
        name="kvpost",
    )(proj_a, *tabs)


def _cmp_weights(pe, cw):
    eye = jnp.eye(NSA_KV, dtype=F32)
    cols = []
    for s in range(2):
        for part in range(2):
            w = cw[s, part * CMP_STRIDE:(part + 1) * CMP_STRIDE]
            blk = jnp.einsum('jde,gh->jgdhe', w, eye).reshape(CMP_STRIDE, LANE, LANE)
            full = jnp.zeros((CMP_STRIDE, 2, LANE, LANE), F32).at[:, s].set(blk)
            cols.append(full.reshape(CMP_STRIDE * 2 * LANE, LANE))
    w_all = jnp.concatenate(cols, axis=1).astype(BF16)
    bias = jnp.einsum('sjd,sjde->se', pe, cw)
    bias = jnp.concatenate([bias[0], bias[0], bias[1], bias[1]]).reshape(1, 2 * LANE)
    return w_all, bias


def _cmp_body(sub_ref, w_ref, b_ref, kc_ref, vct_ref):
    r = jnp.dot(sub_ref[0], w_ref[...], preferred_element_type=F32)
    n = r.shape[0]
    b = b_ref[...]
    kc = r[:, 0:128] + pltpu.roll(r[:, 128:256], n - 1, 0) + b[:, 0:128]
    vc = r[:, 256:384] + pltpu.roll(r[:, 384:512], n - 1, 0) + b[:, 128:256]
    kc_ref[0] = kc.astype(BF16)
    vct_ref[0] = vc.T.astype(BF16)


def _compress(kcsrc, w_all, bias, b, t):
    n_sub = t // CMP_STRIDE
    sub = kcsrc.reshape(b, n_sub, CMP_STRIDE * 256)
    return pl.pallas_call(
        _cmp_body,
        grid=(b,),
        in_specs=[pl.BlockSpec((1, n_sub, CMP_STRIDE * 256), lambda i: (i, 0, 0)),
                  pl.BlockSpec((CMP_STRIDE * 256, 512), lambda i: (0, 0)),
                  pl.BlockSpec((1, 256), lambda i: (0, 0))],
        out_specs=[pl.BlockSpec((1, n_sub, LANE), lambda i: (i, 0, 0)),
                   pl.BlockSpec((1, LANE, n_sub), lambda i: (i, 0, 0))],
        out_shape=[jax.ShapeDtypeStruct((b, n_sub, LANE), BF16), jax.ShapeDtypeStruct((b, LANE, n_sub), BF16)],
        compiler_params=_cparams(("arbitrary",)),
        name="compress",
    )(sub, w_all, bias)


def _slc_cmp_matrix(n_slc, n_cmp, n_cmp_pad):
    ratio = SLC_BLOCK // CMP_STRIDE
    span = CMP_BLOCK // CMP_STRIDE
    m = np.zeros((n_slc, n_cmp_pad), np.float32)
    for j in range(n_slc):
        for k in range(ratio + span - 1):
            n = ratio * j - (span - 1) + k
            if 0 <= n < n_cmp:
                m[j, n] = 1.0
    return m


def _nsa_body(n_cmp_pad, n_slc, q_ref, gt_ref, cos_ref, sin_ref, kc_ref, vct_ref, ks_ref, vst_ref,
              kw_ref, vwt_ref, mt_ref, o_ref, sc_ref, sel_ref, ml_ref, acc_ref, s_ref):
    i = pl.program_id(1)
    q0 = i * Q_BLOCK
    hq = NSA_HPG * Q_BLOCK
    q = q_ref[0]
    cos4 = jnp.concatenate([cos_ref[...]] * NSA_HPG, axis=1)
    sin4 = jnp.concatenate([sin_ref[...]] * NSA_HPG, axis=1)
    gs = jax.nn.sigmoid(gt_ref[0])
    lane = lax.broadcasted_iota(jnp.int32, (Q_BLOCK, Q_BLOCK), 1)
    sub = lax.broadcasted_iota(jnp.int32, (Q_BLOCK, Q_BLOCK), 0)
    sub_k = lax.broadcasted_iota(jnp.int32, (KEY_TILE, Q_BLOCK), 0)
    qp_k = q0 + lax.broadcasted_iota(jnp.int32, (KEY_TILE, Q_BLOCK), 1)
    tpk = KEY_TILE // Q_BLOCK
    bpt = KEY_TILE // SLC_BLOCK
    zeros_g = jnp.zeros((HEAD_DIM, hq), F32)
    half = ROT_DIM // 2

    for g in range(NSA_KV):
        xa = q[:, g * 256:g * 256 + 128].T
        xb = q[:, g * 256 + 128:g * 256 + 256].T
        qn = jnp.concatenate([xa[0:64], xa[64:128], xb[0:64], xb[64:128]], axis=1) * (HEAD_DIM ** -0.5 * LOG2E)
        x1, x2 = qn[0:half], qn[half:ROT_DIM]
        qr = jnp.concatenate([x1 * cos4 - x2 * sin4, x2 * cos4 + x1 * sin4, qn[ROT_DIM:]], axis=0)

        def pad(a):
            parts = [zeros_g] * NSA_KV
            parts[g] = a
            return jnp.concatenate(parts, axis=0).astype(BF16)

        qn_p, qr_p = pad(qn), pad(qr)

        def compressed(n_rows):
            s1 = jnp.dot(kc_ref[0, 0:n_rows, :], qn_p, preferred_element_type=F32)
            n_idx = lax.broadcasted_iota(jnp.int32, (n_rows, Q_BLOCK), 0)
            lane_c = lax.broadcasted_iota(jnp.int32, (n_rows, Q_BLOCK), 1)
            cmp_ok = (CMP_STRIDE * n_idx + CMP_BLOCK - 1) <= (q0 + lane_c)
            ps = []
            for h in range(NSA_HPG):
                s = jnp.where(cmp_ok, s1[:, h * Q_BLOCK:(h + 1) * Q_BLOCK], NEG)
                m = jnp.max(s, axis=0, keepdims=True)
                e = jnp.where(cmp_ok, jnp.exp2(s - m), 0.0)
                d = jnp.maximum(jnp.sum(e, axis=0, keepdims=True), 1e-30)
                ps.append(e / d)
            imp = ps[0] + ps[1] + ps[2] + ps[3]
            p1 = jnp.concatenate(ps, axis=1).astype(BF16)
            o_c = jnp.dot(vct_ref[0, g * HEAD_DIM:(g + 1) * HEAD_DIM, 0:n_rows], p1, preferred_element_type=F32)
            return o_c, _mm_ones_rhs(mt_ref[:, 0:n_rows], imp)

        half_rows = n_cmp_pad // 2
        if half_rows % LANE == 0:
            o_cmp, imps = lax.cond(CMP_STRIDE * half_rows + CMP_BLOCK - 1 > q0 + Q_BLOCK - 1,
                                   lambda: compressed(half_rows), lambda: compressed(n_cmp_pad))
        else:
            o_cmp, imps = compressed(n_cmp_pad)

        blk = lax.broadcasted_iota(jnp.int32, (n_slc, Q_BLOCK), 0)
        qp_s = q0 + lax.broadcasted_iota(jnp.int32, (n_slc, Q_BLOCK), 1)
        causal = blk * SLC_BLOCK <= qp_s
        qblk = qp_s // SLC_BLOCK
        forced = (blk == 0) | (blk == qblk) | (blk == qblk - 1)
        score = jnp.where(causal, jnp.where(forced, jnp.inf, imps), -jnp.inf)
        sc_ref[...] = score

        def rank_step(jh, cnt):
            for j in (2 * jh, 2 * jh + 1):
                row = sc_ref[pl.ds(j, 1), :]
                ge = jnp.where(row >= score, 1.0, 0.0)
                gt = jnp.where(row > score, 1.0, 0.0)
                cnt = cnt + jnp.where(blk > j, ge, gt)
            return cnt

        cnt = lax.fori_loop(0, jnp.minimum(i + 1, n_slc // 2), rank_step, jnp.zeros((n_slc, Q_BLOCK), F32))
        sel_ref[...] = jnp.where(cnt < N_SELECT, jnp.where(score > -jnp.inf, 0.0, NEG), NEG)

        init = (jnp.full((1, hq), NEG, F32), jnp.zeros((1, hq), F32), jnp.zeros((HEAD_DIM, hq), F32))

        def scores(k_ref, row0, n_rows):
            return jnp.dot(k_ref[0, pl.ds(pl.multiple_of(row0, KEY_TILE), n_rows), :], qr_p,
                           preferred_element_type=F32)

        def attend(s, vt_ref, t, bias, state):
            m, l, acc = state
            off = pl.multiple_of(t * KEY_TILE, KEY_TILE)
            ps, alphas, ms, ls = [], [], [], []
            for h in range(NSA_HPG):
                hs = slice(h * Q_BLOCK, (h + 1) * Q_BLOCK)
                sh = s[:, hs] + bias
                m_new = jnp.maximum(m[:, hs], jnp.max(sh, axis=0, keepdims=True))
                alpha = jnp.exp2(m[:, hs] - m_new)
                p = jnp.exp2(sh - m_new)
                ms.append(m_new)
                ls.append(alpha * l[:, hs] + jnp.sum(p, axis=0, keepdims=True))
                ps.append(p.astype(BF16))
                alphas.append(alpha)
            vt = vt_ref[0, g * HEAD_DIM:(g + 1) * HEAD_DIM, pl.ds(off, KEY_TILE)]
            pv = jnp.dot(vt, jnp.concatenate(ps, axis=1), preferred_element_type=F32)
            cat = lambda parts: jnp.concatenate(parts, axis=1)
            return cat(ms), cat(ls), acc * cat(alphas) + pv

        keys_per_step = KEY_TILE * ATT_UNROLL
        n_steps = (q0 + Q_BLOCK + keys_per_step - 1) // keys_per_step

        def slc_step(u, carry):
            s_next = scores(ks_ref, jnp.minimum(u + 1, n_steps - 1) * keys_per_step, keys_per_step)
            state = (ml_ref[0:1, :], ml_ref[1:2, :], acc_ref[...])
            for j in range(ATT_UNROLL):
                t = u * ATT_UNROLL + j
                rows = [jnp.broadcast_to(sel_ref[pl.ds(bpt * t + jj, 1), :], (SLC_BLOCK, Q_BLOCK))
                        for jj in range(bpt)]
                bias = jnp.where(t * KEY_TILE + sub_k <= qp_k, jnp.concatenate(rows, axis=0), NEG)
                state = attend(s_ref[u % 2, j * KEY_TILE:(j + 1) * KEY_TILE, :], vst_ref, t, bias, state)
            ml_ref[0:1, :], ml_ref[1:2, :], acc_ref[...] = state
            s_ref[(u + 1) % 2] = s_next
            return carry

        ml_ref[0:1, :], ml_ref[1:2, :], acc_ref[...] = init
        s_ref[0] = scores(ks_ref, 0, keys_per_step)
        lax.fori_loop(0, n_steps, slc_step, 0)
        o_slc = acc_ref[...] / ml_ref[1:2, :]

        state = init
        for k in range(WINDOW // KEY_TILE + 1):
            t = i // tpk - k
            tc = jnp.maximum(t, 0)
            kp = tc * KEY_TILE + sub_k
            ok = jnp.where(kp <= qp_k, qp_k - kp, WINDOW) < jnp.where(t >= 0, WINDOW, 0)
            state = attend(scores(kw_ref, tc * KEY_TILE, KEY_TILE), vwt_ref, tc, jnp.where(ok, 0.0, NEG), state)
        o_win = state[2] / state[1]

        def gate(jj):
            return jnp.concatenate([gs[(g * NSA_HPG + h) * 3 + jj:(g * NSA_HPG + h) * 3 + jj + 1, :]
                                    for h in range(NSA_HPG)], axis=1)

        o_t = gate(0) * o_cmp + gate(1) * o_slc + gate(2) * o_win
        ya = jnp.concatenate([o_t[:, 0:128], o_t[:, 128:256]], axis=0).T
        yb = jnp.concatenate([o_t[:, 256:384], o_t[:, 384:512]], axis=0).T
        o_ref[0, :, g * 256:g * 256 + 128] = ya
        o_ref[0, :, g * 256 + 128:g * 256 + 256] = yb


def _nsa_prompt(proj_a, gate_t, cos_t, sin_t, kc, vct, ksb, vst, kwb, vwt, b, t):
    nb = t // Q_BLOCK
    n_sub = t // CMP_STRIDE
    n_slc = t // SLC_BLOCK
    mt = jnp.asarray(_slc_cmp_matrix(n_slc, n_sub - 1, n_sub)).astype(BF16)
    seq = lambda w: pl.BlockSpec((1, t, w), lambda bi, i: (bi, 0, 0))
    seq_t = pl.BlockSpec((1, LANE, t), lambda bi, i: (bi, 0, 0))
    return pl.pallas_call(
        functools.partial(_nsa_body, n_sub, n_slc),
        grid=(b, nb),
        in_specs=[pl.BlockSpec((1, Q_BLOCK, NSA_WIDTH), lambda bi, i: (bi, i, A_Q // NSA_WIDTH)),
                  pl.BlockSpec((1, 3 * NSA_HEADS, Q_BLOCK), lambda bi, i: (bi, 0, i)),
                  pl.BlockSpec((ROT_DIM // 2, Q_BLOCK), lambda bi, i: (0, i)),
                  pl.BlockSpec((ROT_DIM // 2, Q_BLOCK), lambda bi, i: (0, i)),
                  pl.BlockSpec((1, n_sub, LANE), lambda bi, i: (bi, 0, 0)),
                  pl.BlockSpec((1, LANE, n_sub), lambda bi, i: (bi, 0, 0)),
                  seq(LANE), seq_t, seq(LANE), seq_t,
                  pl.BlockSpec((n_slc, n_sub), lambda bi, i: (0, 0))],
        out_specs=pl.BlockSpec((1, Q_BLOCK, NSA_WIDTH), lambda bi, i: (bi, i, 0)),
        out_shape=jax.ShapeDtypeStruct((b, t, NSA_WIDTH), F32),
        scratch_shapes=[pltpu.VMEM((n_slc, Q_BLOCK), F32), pltpu.VMEM((n_slc, Q_BLOCK), F32),
                        pltpu.VMEM((8, NSA_HPG * Q_BLOCK), F32), pltpu.VMEM((HEAD_DIM, NSA_HPG * Q_BLOCK), F32),
                        pltpu.VMEM((2, KEY_TILE * ATT_UNROLL, NSA_HPG * Q_BLOCK), F32)],
        compiler_params=_cparams(("arbitrary", "arbitrary")),
        name="nsa_prompt",
    )(proj_a.reshape(b, t, A_WIDTH), gate_t, cos_t, sin_t, kc, vct,
      ksb.reshape(b, t, LANE), vst, kwb.reshape(b, t, LANE), vwt, mt)


def _masked_softmax(s, mask):
    s = jnp.where(mask, s, -jnp.inf)
    m = jnp.max(s, axis=-1, keepdims=True)
    m = jnp.where(jnp.isfinite(m), m, 0.0)
    e = jnp.exp(s - m)
    return e / jnp.maximum(jnp.sum(e, axis=-1, keepdims=True), 1e-30)


def _rope_rows(x, pos):
    half = ROT_DIM // 2
    inv = ROPE_THETA ** (-jnp.arange(0, ROT_DIM, 2, dtype=F32) / ROT_DIM)
    ang = pos.astype(F32)[:, None] * inv[None, :]
    cos = jnp.cos(ang)[None, :, None, :]
    sin = jnp.sin(ang)[None, :, None, :]
    x1, x2 = x[..., :half], x[..., half:ROT_DIM]
    return jnp.concatenate([x1 * cos - x2 * sin, x2 * cos + x1 * sin, x[..., ROT_DIM:]], axis=-1)


def _nsa_decode(q, gates, kv_all, win_all, pe, cw):
    b, tq = q.shape[:2]
    tk = kv_all.shape[1]
    scale = HEAD_DIM ** -0.5
    qpos = PAST_LEN + jnp.arange(tq)
    qn = q.reshape(b, tq, NSA_KV, NSA_HPG, HEAD_DIM)
    qr = _rope_rows(q, qpos).reshape(b, tq, NSA_KV, NSA_HPG, HEAD_DIM)
    n_sub = tk // CMP_STRIDE
    n_cmp = n_sub - 1

    def compress(rows, s):
        sub = rows[:, :n_sub * CMP_STRIDE].reshape(b, n_sub, CMP_STRIDE, NSA_KV, HEAD_DIM)
        a = jnp.einsum('bnjgd,jde->bnge', sub, cw[s, :CMP_STRIDE])
        c = jnp.einsum('bnjgd,jde->bnge', sub, cw[s, CMP_STRIDE:])
        bias = jnp.einsum('jd,jde->e', pe[s], cw[s])
        return a[:, :n_cmp] + c[:, 1:] + bias

    kc = compress(kv_all[:, :, 0], 0)
    vc = compress(kv_all[:, :, 1], 1)
    cmp_end = CMP_STRIDE * jnp.arange(n_cmp) + CMP_BLOCK - 1
    s1 = jnp.einsum('bqghd,bngd->bqghn', qn, kc) * scale
    p1 = _masked_softmax(s1, (cmp_end[None, :] <= qpos[:, None])[None, :, None, None, :])
    o_cmp = jnp.einsum('bqghn,bngd->bqghd', p1, vc)
    n_slc = -(-tk // SLC_BLOCK)
    mt = jnp.asarray(_slc_cmp_matrix(n_slc, n_cmp, n_cmp))
    imp = jnp.einsum('bqgn,jn->bqgj', jnp.sum(p1, axis=3), mt, precision=HI)
    blk = jnp.arange(n_slc)
    qblk = qpos // SLC_BLOCK
    causal = (blk[None, :] * SLC_BLOCK <= qpos[:, None])[None, :, None, :]
    forced = ((blk[None, :] == 0) | (blk[None, :] == qblk[:, None]) | (blk[None, :] == qblk[:, None] - 1))[None, :, None, :]
    score = jnp.where(causal, jnp.where(forced, jnp.inf, imp), -jnp.inf)
    ahead = (score[..., None, :] > score[..., :, None]) | (
        (score[..., None, :] == score[..., :, None]) & (blk[None, :] < blk[:, None]))
    rank = jnp.sum(ahead, axis=-1)
    sel = (rank < N_SELECT) & (score > -jnp.inf)
    kpos = jnp.arange(tk)
    key_ok = sel[..., kpos // SLC_BLOCK] & (kpos[None, :] <= qpos[:, None])[None, :, None, :]
    s2 = jnp.einsum('bqghd,bkgd->bqghk', qr, kv_all[:, :, 2]) * scale
    p2 = _masked_softmax(s2, key_ok[:, :, :, None, :])
    o_slc = jnp.einsum('bqghk,bkgd->bqghd', p2, kv_all[:, :, 3])
    wpos = PAST_LEN - WINDOW + jnp.arange(win_all.shape[1])
    m3 = (wpos[None, :] <= qpos[:, None]) & (qpos[:, None] - wpos[None, :] < WINDOW)
    s3 = jnp.einsum('bqghd,bkgd->bqghk', qr, win_all[:, :, 0]) * scale
    p3 = _masked_softmax(s3, m3[None, :, None, None, :])
    o_win = jnp.einsum('bqghk,bkgd->bqghd', p3, win_all[:, :, 1])
    g = gates.reshape(b, tq, NSA_KV, NSA_HPG, 3)
    o = g[..., 0:1] * o_cmp + g[..., 1:2] * o_slc + g[..., 2:3] * o_win
    return o.reshape(b, tq, NSA_WIDTH)


N_PAGES = PAST_LEN // PAGE_SIZE
DEC_ROWS = 8
DEC_QROWS = NSA_HEADS * 4


def _nsa_dec_body(dec_t, pt_ref, *refs):
    pages = refs[:N_PAGES]
    kvn_ref, cw_ref, wn_ref, qn_ref, qr_ref, g_ref, w_ref, b_ref, mt_ref, o_ref, ck_ref, cv_ref = refs[N_PAGES:]
    del pt_ref
    qn, qr = qn_ref[0], qr_ref[0]
    nq = qn.shape[0]
    lane = lax.broadcasted_iota(jnp.int32, (nq, LANE), 1)
    qi = lax.broadcasted_iota(jnp.int32, (nq, LANE), 0) % dec_t
    qpos = PAST_LEN + qi
    n_sub = PAST_LEN // CMP_STRIDE

    for p in range(N_PAGES):
        ck_ref[p * PAGE_SIZE:(p + 1) * PAGE_SIZE, :] = pages[p][0, 0, :, 0:128]
        cv_ref[p * PAGE_SIZE:(p + 1) * PAGE_SIZE, :] = pages[p][0, 0, :, 128:256]
    parts = []
    for j in range(CMP_STRIDE):
        parts += [ck_ref[pl.ds(j, n_sub, stride=CMP_STRIDE), :], cv_ref[pl.ds(j, n_sub, stride=CMP_STRIDE), :]]
    r = _mm(jnp.concatenate(parts, axis=1), w_ref[...])
    bias = b_ref[...]
    kc = r[:, 0:128] + pltpu.roll(r[:, 128:256], n_sub - 1, 0) + bias[:, 0:128]
    vc = r[:, 256:384] + pltpu.roll(r[:, 384:512], n_sub - 1, 0) + bias[:, 128:256]

    def softmax_rows(scores, oks):
        scores = [jnp.where(ok, s, NEG) for s, ok in zip(scores, oks)]
        m = functools.reduce(jnp.maximum, [jnp.max(s, axis=1, keepdims=True) for s in scores])
        es = [jnp.where(ok, jnp.exp(s - m), 0.0) for s, ok in zip(scores, oks)]
        d = jnp.maximum(functools.reduce(jnp.add, [jnp.sum(e, axis=1, keepdims=True) for e in es]), 1e-30)
        return [e / d for e in es]

    cmp_ok = (CMP_STRIDE * lane + CMP_BLOCK - 1) <= qpos
    (p1,) = softmax_rows([_mm_nt(qn, kc)], [cmp_ok])
    o_cmp = _mm(p1, vc)
    gq = NSA_KV * dec_t
    imp = functools.reduce(jnp.add, [p1[h * gq:(h + 1) * gq] for h in range(NSA_HPG)])
    imps = _mm_hi(imp, mt_ref[...])
    blk = lane[0:gq]
    qp8 = qpos[0:gq]
    qblk = qp8 // SLC_BLOCK
    causal = blk * SLC_BLOCK <= qp8
    forced = (blk == 0) | (blk == qblk) | (blk == qblk - 1)
    score = jnp.where(causal, jnp.where(forced, jnp.inf, imps), -jnp.inf)
    n_slc = -(-(PAST_LEN + dec_t) // SLC_BLOCK)
    cnt = jnp.zeros((gq, LANE), F32)
    for j in range(n_slc):
        col = score[:, j:j + 1]
        cnt = cnt + jnp.where(blk > j, jnp.where(col >= score, 1.0, 0.0), jnp.where(col > score, 1.0, 0.0))
    sel = jnp.where(cnt < N_SELECT, jnp.where(score > -jnp.inf, 1.0, 0.0), 0.0)

    def sel_rows(j0):
        pick = jnp.where(lane[0:gq] < SLC_BLOCK, sel[:, j0:j0 + 1], sel[:, j0 + 1:j0 + 2])
        return jnp.concatenate([pick] * NSA_HPG, axis=0) > 0.5

    zeros_k = jnp.zeros((PAGE_SIZE - DEC_ROWS, LANE), F32)
    new_ok = (lane < dec_t) & (lane <= qi)
    kvn = kvn_ref[0]
    s_list = [_mm_nt(qr, pages[p][0, 0, :, 256:384]) for p in range(N_PAGES)]
    s_list.append(_mm_nt(qr, jnp.concatenate([kvn[:, 256:384], zeros_k], axis=0)))
    ok_list = [sel_rows(2 * p) for p in range(N_PAGES)]
    own = jnp.concatenate([sel[:, 2 * N_PAGES:2 * N_PAGES + 1]] * NSA_HPG, axis=0) > 0.5
    ok_list.append(new_ok & own)
    p_list = softmax_rows(s_list, ok_list)
    o_slc = functools.reduce(jnp.add, [_mm(p_list[p], pages[p][0, 0, :, 384:512]) for p in range(N_PAGES)])
    o_slc = o_slc + _mm(p_list[N_PAGES], jnp.concatenate([kvn[:, 384:512], zeros_k], axis=0))

    n_wt = WINDOW // PAGE_SIZE
    wn = wn_ref[0]
    s_list, ok_list = [], []
    for t in range(n_wt):
        s_list.append(_mm_nt(qr, cw_ref[0, 0, t * PAGE_SIZE:(t + 1) * PAGE_SIZE, 0:128]))
        ok_list.append(qpos - (PAST_LEN - WINDOW + t * PAGE_SIZE + lane) < WINDOW)
    s_list.append(_mm_nt(qr, jnp.concatenate([wn[:, 0:128], zeros_k], axis=0)))
    ok_list.append(new_ok)
    p_list = softmax_rows(s_list, ok_list)
    o_win = functools.reduce(jnp.add, [_mm(p_list[t], cw_ref[0, 0, t * PAGE_SIZE:(t + 1) * PAGE_SIZE, 128:256])
                                       for t in range(n_wt)])
    o_win = o_win + _mm(p_list[n_wt], jnp.concatenate([wn[:, 128:256], zeros_k], axis=0))
    g = g_ref[0]
    o_ref[0] = g[:, 0:128] * o_cmp + g[:, 128:256] * o_slc + g[:, 256:384] * o_win


def _nsa_decode_pallas(l, q, gates, cache_kv, page_table, kv_new, cache_win, win_new, pe, cw):
    b, tq = q.shape[:2]
    scale = HEAD_DIM ** -0.5
    eye = jnp.eye(NSA_KV, dtype=F32)

    def rows(x):
        x5 = x.reshape(b, tq, NSA_KV, NSA_HPG, HEAD_DIM).transpose(0, 3, 2, 1, 4)
        return jnp.einsum('bhgqd,gk->bhgqkd', x5, eye).reshape(b, DEC_QROWS, LANE).astype(BF16)

    qn = rows(q * scale)
    qr = rows(_rope_rows(q, PAST_LEN + jnp.arange(tq)) * scale)
    g5 = gates.reshape(b, tq, NSA_KV, NSA_HPG, 3).transpose(0, 3, 2, 1, 4)
    lane_g = jnp.repeat(eye, HEAD_DIM, axis=1)
    gate_b = jnp.einsum('bhgqj,gn->bhgqjn', g5, lane_g).reshape(b, DEC_QROWS, 3 * LANE)
    w_all, bias = _cmp_weights(pe, cw)
    n_cmp = PAST_LEN // CMP_STRIDE - 1
    n_slc = -(-(PAST_LEN + tq) // SLC_BLOCK)
    mt = np.zeros((LANE, LANE), np.float32)
    mt[:n_cmp, :n_slc] = _slc_cmp_matrix(n_slc, n_cmp, n_cmp).T
    pad_rows = lambda a: jnp.pad(a, ((0, 0), (0, DEC_ROWS - tq), (0, 0)))
    n_pool = cache_kv.shape[1]
    ckv = cache_kv.reshape(DEPTH, n_pool, PAGE_SIZE, 512)
    page_spec = lambda p: pl.BlockSpec((1, 1, PAGE_SIZE, 512), lambda bi, pt: (l, pt[bi, p], 0, 0))
    per_seq = lambda r, w: pl.BlockSpec((1, r, w), lambda bi, pt: (bi, 0, 0))
    const = lambda shape: pl.BlockSpec(shape, lambda bi, pt: (0,) * len(shape))
    grid_spec = pltpu.PrefetchScalarGridSpec(
        num_scalar_prefetch=1,
        grid=(b,),
        in_specs=[page_spec(p) for p in range(N_PAGES)]
                 + [per_seq(DEC_ROWS, 512),
                    pl.BlockSpec((1, 1, WINDOW, 256), lambda bi, pt: (l, bi, 0, 0)),
                    per_seq(DEC_ROWS, 256), per_seq(DEC_QROWS, LANE), per_seq(DEC_QROWS, LANE),
                    per_seq(DEC_QROWS, 3 * LANE), const((CMP_STRIDE * 256, 512)), const((1, 256)), const((LANE, LANE))],
        out_specs=per_seq(DEC_QROWS, LANE),
        scratch_shapes=[pltpu.VMEM((PAST_LEN, LANE), F32), pltpu.VMEM((PAST_LEN, LANE), F32)],
    )
    o = pl.pallas_call(
        functools.partial(_nsa_dec_body, tq),
        grid_spec=grid_spec,
        out_shape=jax.ShapeDtypeStruct((b, DEC_QROWS, LANE), F32),
        compiler_params=_cparams(("arbitrary",)),
        name="nsa_decode",
    )(page_table, *([ckv] * N_PAGES), pad_rows(kv_new), cache_win.reshape(DEPTH, b, WINDOW, 256), pad_rows(win_new),
      qn, qr, gate_b, w_all, bias, jnp.asarray(mt))
    o6 = o.reshape(b, NSA_HPG, NSA_KV, tq, NSA_KV, HEAD_DIM)
    return jnp.einsum('bhgqkd,gk->bqghd', o6, eye).reshape(b, tq, NSA_WIDTH)


def _rwkv_mix(cols, z, prev, s0, mu, w0, w_up, a0, a_up, k_k, k_a, r_k, ln_w):
    b, t = cols.shape[:2]
    shifted = jnp.concatenate([prev[:, None], cols[:, :-1]], axis=1)
    xx = cols + (shifted - cols) * mu
    sp = np.cumsum([RWKV_WIDTH, RWKV_WIDTH, RWKV_WIDTH, RWKV_LORA_W])
    r, k, v, wd, ad = jnp.split(xx, sp, axis=-1)
    w = -jax.nn.softplus(-(w0 + jnp.dot(jnp.tanh(wd), w_up, precision=HI))) - 0.5
    logw = -jnp.exp(w)
    a = jax.nn.sigmoid(a0 + jnp.dot(ad, a_up, precision=HI))

    def heads(u):
        return u.reshape(b, t, RWKV_HEADS, RWKV_HEAD)

    kk = heads(k * k_k)
    kk = kk / jnp.maximum(jnp.sqrt(jnp.sum(kk * kk, axis=-1, keepdims=True)), 1e-12)
    k = k * (1.0 + (a - 1.0) * k_a)
    r, k, v, logw, a = heads(r), heads(k), heads(v), heads(logw), heads(a)
    c = min(RWKV_CHUNK, t)
    nc = t // c

    def chunks(u):
        return u.reshape(b, nc, c, RWKV_HEADS, RWKV_HEAD).transpose(1, 0, 3, 2, 4)

    r, k, v, logw, kk, a = (chunks(u) for u in (r, k, v, logw, kk, a))
    cum = jnp.cumsum(logw, axis=3)
    w_in = jnp.exp(cum)
    w_ex = jnp.exp(cum - logw)
    w_end = w_in[..., -1:, :]
    alpha_t = -kk * w_ex
    beta_h = kk * a / w_in
    k_h = k / w_in
    r_t = r * w_in
    tri_s = jnp.tril(jnp.ones((c, c), F32), -1)
    tri_i = jnp.tril(jnp.ones((c, c), F32))
    mm = functools.partial(jnp.einsum, precision=HI)
    a_ab = mm('nbhtk,nbhsk->nbhts', alpha_t, beta_h) * tri_s
    a_ak = mm('nbhtk,nbhsk->nbhts', alpha_t, k_h) * tri_s
    a_rb = mm('nbhtk,nbhsk->nbhts', r_t, beta_h) * tri_i
    a_rk = mm('nbhtk,nbhsk->nbhts', r_t, k_h) * tri_i
    eye = jnp.eye(c, dtype=F32)
    tm = eye + a_ab
    pw = a_ab
    steps = 1
    while steps * 2 < c:
        pw = mm('nbhts,nbhsu->nbhtu', pw, pw)
        tm = mm('nbhts,nbhsu->nbhtu', tm, eye + pw)
        steps *= 2
    p_m = mm('nbhts,nbhsk->nbhtk', tm, alpha_t)
    q_m = mm('nbhts,nbhsv->nbhtv', tm, mm('nbhts,nbhsv->nbhtv', a_ak, v))
    b_t = beta_h * w_end
    k_t = k_h * w_end
    m_m = mm('nbhtk,nbhtj->nbhkj', p_m, b_t)
    n_m = mm('nbhtv,nbhtk->nbhvk', q_m, b_t) + mm('nbhtv,nbhtk->nbhvk', v, k_t)
    r_p = r_t + mm('nbhts,nbhsk->nbhtk', a_rb, p_m)
    o_p = mm('nbhts,nbhsv->nbhtv', a_rb, q_m) + mm('nbhts,nbhsv->nbhtv', a_rk, v)

    def step(s, inp):
        m_c, n_c, we_c, rp_c, op_c = inp
        o = mm('bhtk,bhvk->bhtv', rp_c, s) + op_c
        s = s * we_c + mm('bhvk,bhkj->bhvj', s, m_c) + n_c
        return s, o

    s_t, o = lax.scan(step, s0, (m_m, n_m, w_end, r_p, o_p))
    o = o.transpose(1, 0, 3, 2, 4).reshape(b, t, RWKV_HEADS, RWKV_HEAD)
    r, k, v = (u.transpose(1, 0, 3, 2, 4).reshape(b, t, RWKV_HEADS, RWKV_HEAD) for u in (r, k, v))
    mean = jnp.mean(o, axis=-1, keepdims=True)
    var = jnp.mean(jnp.square(o - mean), axis=-1, keepdims=True)
    o = (o - mean) * lax.rsqrt(var + GN_EPS) * ln_w.reshape(RWKV_HEADS, RWKV_HEAD)
    o = o + jnp.sum(r * k * r_k, axis=-1, keepdims=True) * v
    y = o.reshape(b, t, RWKV_WIDTH) * jax.nn.silu(z)
    return y, cols[:, -1], s_t


def _hgrn_mix(q, fz, i, z, s0, lb, ln_w):
    b, t = q.shape[:2]
    log_f = jnp.logaddexp(jnp.log(lb), jnp.log1p(-lb) + jax.nn.log_sigmoid(fz))
    kf = (1.0 - lb) * jax.nn.sigmoid(-fz)
    chunk = HGRN_CHUNK if t % HGRN_CHUNK == 0 else t
    nc = t // chunk

    def chunks(u):
        return u.reshape(b, nc, chunk, HGRN_HEADS, HGRN_HEAD).transpose(1, 0, 3, 2, 4)

    mask = jnp.tril(jnp.ones((chunk, chunk), dtype=bool))

    def step(s, inp):
        q_c, lf_c, k_c, i_c = inp
        cl = jnp.cumsum(lf_c, axis=2)
        dec = jnp.exp(jnp.where(mask[:, :, None], cl[:, :, :, None, :] - cl[:, :, None, :, :], -jnp.inf))
        att = jnp.einsum('bhtk,bhtsk,bhsk->bhts', q_c, dec, k_c)
        o = jnp.einsum('bhts,bhsv->bhtv', att, i_c) + jnp.einsum('bhtk,bhkv->bhtv', q_c * jnp.exp(cl), s)
        cl_end = cl[:, :, -1:]
        s = jnp.exp(cl_end[:, :, 0])[..., None] * s + jnp.einsum('bhsk,bhsv->bhkv', k_c * jnp.exp(cl_end - cl), i_c)
        return s, o

    s_t, o = lax.scan(step, s0, (chunks(q), chunks(log_f), chunks(kf), chunks(i)))
    o = o.transpose(1, 0, 3, 2, 4).reshape(b, t, HGRN_HEADS, HGRN_HEAD)
    o = o * lax.rsqrt(jnp.mean(o * o, axis=-1, keepdims=True) + RMS_EPS) * ln_w.reshape(HGRN_HEADS, HGRN_HEAD)
    y = o.reshape(b, t, HGRN_WIDTH) * jax.nn.silu(z)
    return y, s_t


def _mm(a, b):
    return jnp.dot(a.astype(BF16), b.astype(BF16), preferred_element_type=F32)


def _mm_nt(a, b):
    return lax.dot_general(a.astype(BF16), b.astype(BF16), (((1,), (1,)), ((), ())), preferred_element_type=F32)


def _mm_tn(a, b):
    return lax.dot_general(a.astype(BF16), b.astype(BF16), (((0,), (0,)), ((), ())), preferred_element_type=F32)


def _mm_hi(a, b):
    return jnp.dot(a, b, preferred_element_type=F32, precision=HI)


def _mm_ones_rhs(ones_bf16, b):
    hi = b.astype(BF16)
    lo = (b - hi.astype(F32)).astype(BF16)
    return (jnp.dot(ones_bf16, hi, preferred_element_type=F32) + jnp.dot(ones_bf16, lo, preferred_element_type=F32))


def _mm_ones(a, ones_bf16):
    hi = a.astype(BF16)
    lo = (a - hi.astype(F32)).astype(BF16)
    return (jnp.dot(hi, ones_bf16, preferred_element_type=F32) + jnp.dot(lo, ones_bf16, preferred_element_type=F32))


def _log_sigmoid(x):
    return jnp.minimum(x, 0.0) - jnp.log1p(jnp.exp(-jnp.abs(x)))


HGRN_SUB = 16
MIX_TILE = 512


def _hgrn_body(n_t, q_ref, f_ref, i_ref, z_ref, lbc_ref, lnw_ref, y_ref, s_ref, st_ref, o_ref):
    ti = pl.program_id(1)
    c, sc = HGRN_CHUNK, HGRN_SUB
    heads = range(HGRN_HEADS)
    hsl = [slice(h * HGRN_HEAD, (h + 1) * HGRN_HEAD) for h in heads]

    @pl.when(ti == 0)
    def _():
        st_ref[...] = jnp.zeros(st_ref.shape, F32)

    log_lb, log_1m_lb, one_m_lb = lbc_ref[0:1, :], lbc_ref[1:2, :], lbc_ref[2:3, :]
    tri = (lax.broadcasted_iota(jnp.int32, (c, c), 0) >= lax.broadcasted_iota(jnp.int32, (c, c), 1)).astype(F32)
    sub_s = lax.broadcasted_iota(jnp.int32, (sc, HGRN_WIDTH), 0)

    def chunk(ci, carry):
        r0 = pl.multiple_of(ci * c, c)
        q = q_ref[0, pl.ds(r0, c), :]
        fz = f_ref[0, pl.ds(r0, c), :]
        iv = i_ref[0, pl.ds(r0, c), :]
        b_ = log_1m_lb + _log_sigmoid(fz)
        lf = jnp.maximum(log_lb, b_) + jnp.log1p(jnp.exp(-jnp.abs(log_lb - b_)))
        kf = one_m_lb * jax.nn.sigmoid(-fz)
        cl = _mm_hi(tri, lf)
        c_end = cl[c - 1:c]
        q_dec = q * jnp.exp(cl)
        k_end = kf * jnp.exp(c_end - cl)
        w_end = jnp.exp(c_end)
        st = [st_ref[h] for h in heads]
        o = [_mm_nt(q_dec[:, hsl[h]], st[h]) for h in heads]
        upd = [_mm_tn(iv[:, hsl[h]], k_end[:, hsl[h]]) for h in heads]
        for h in heads:
            st_ref[h] = st[h] * w_end[:, hsl[h]] + upd[h]
        pieces = []
        for blk in range(c // sc):
            lo = blk * sc
            cl_r, q_r, k_r, i_r = cl[lo:lo + sc], q[lo:lo + sc], kf[lo:lo + sc], iv[lo:lo + sc]
            acc = [o[h][lo:lo + sc] for h in heads]
            if blk > 0:
                bnd = cl[lo - 1:lo]
                q_b = q_r * jnp.exp(cl_r - bnd)
                k_b = kf[:lo] * jnp.exp(bnd - cl[:lo])
                att = [_mm_nt(q_b[:, hsl[h]], k_b[:, hsl[h]]) for h in heads]
                acc = [acc[h] + _mm(att[h], iv[:lo, hsl[h]]) for h in heads]
            rows = []
            for t in range(sc):
                w = jnp.where(sub_s <= t, jnp.exp(cl_r[t:t + 1] - cl_r), 0.0)
                e = q_r[t:t + 1] * w * k_r
                d = jnp.concatenate([jnp.broadcast_to(jnp.sum(e[:, hsl[h]], axis=1, keepdims=True), (sc, HGRN_HEAD))
                                     for h in heads], axis=1)
                rows.append(jnp.sum(d * i_r, axis=0, keepdims=True))
            pieces.append(jnp.concatenate(acc, axis=1) + jnp.concatenate(rows, axis=0))
        o_ref[pl.ds(r0, c), :] = jnp.concatenate(pieces, axis=0)
        return carry

    lax.fori_loop(0, o_ref.shape[0] // c, chunk, 0)
    z = z_ref[0]
    gate = lnw_ref[...] * (z * jax.nn.sigmoid(z))
    for h in heads:
        o = o_ref[:, hsl[h]]
        y_ref[0, :, hsl[h]] = o * lax.rsqrt(jnp.mean(o * o, axis=-1, keepdims=True) + RMS_EPS) * gate[:, hsl[h]]

    @pl.when(ti == n_t - 1)
    def _():
        for h in heads:
            s_ref[0, h] = st_ref[h].T


def _hgrn_prompt(proj_a3, lb, ln_w):
    b, t = proj_a3.shape[:2]
    tt = min(MIX_TILE, t)
    n_t = t // tt
    lbc = jnp.concatenate([jnp.log(lb)[None], jnp.log1p(-lb)[None], (1.0 - lb)[None],
                           jnp.zeros((5, HGRN_WIDTH), F32)], axis=0)
    col = lambda off: pl.BlockSpec((1, tt, HGRN_WIDTH), lambda bi, i: (bi, i, off // HGRN_WIDTH))
    return pl.pallas_call(
        functools.partial(_hgrn_body, n_t),
        grid=(b, n_t),
        in_specs=[col(A_HQ), col(A_HF), col(A_HI), col(A_HZ),
                  pl.BlockSpec((8, HGRN_WIDTH), lambda bi, i: (0, 0)),
                  pl.BlockSpec((1, HGRN_WIDTH), lambda bi, i: (0, 0))],
        out_specs=[pl.BlockSpec((1, tt, HGRN_WIDTH), lambda bi, i: (bi, i, 0)),
                   pl.BlockSpec((1, HGRN_HEADS, HGRN_HEAD, HGRN_HEAD), lambda bi, i: (bi, 0, 0, 0))],
        out_shape=[jax.ShapeDtypeStruct((b, t, HGRN_WIDTH), F32),
                   jax.ShapeDtypeStruct((b, HGRN_HEADS, HGRN_HEAD, HGRN_HEAD), F32)],
        scratch_shapes=[pltpu.VMEM((HGRN_HEADS, HGRN_HEAD, HGRN_HEAD), F32), pltpu.VMEM((tt, HGRN_WIDTH), F32)],
        compiler_params=_cparams(("arbitrary", "arbitrary")),
        name="hgrn_prompt",
    )(proj_a3, proj_a3, proj_a3, proj_a3, lbc, ln_w.reshape(1, HGRN_WIDTH))


def _rwkv_body(n_t, cols_ref, z_ref, mu_ref, w0_ref, wup_ref, a0_ref, aup_ref, kk_ref, ka_ref, rk_ref, lnw_ref, bd_ref,
               y_ref, s_ref, prev_ref, st_ref, r_s, k_s, v_s, lw_s, kk_s, a_s, o_s):
    ti = pl.program_id(1)
    tt = cols_ref.shape[1]
    c = RWKV_CHUNK
    n_pair = RWKV_HEADS // 2

    @pl.when(ti == 0)
    def _():
        prev_ref[...] = jnp.zeros(prev_ref.shape, F32)
        st_ref[...] = jnp.zeros(st_ref.shape, F32)

    x = cols_ref[0]
    first = lax.broadcasted_iota(jnp.int32, (tt, 1), 0) == 0
    x_prev = jnp.where(first, prev_ref[0:1, :], pltpu.roll(x, 1, 0))
    prev_ref[0:1, :] = x[tt - 1:tt, :]
    xx = x + (x_prev - x) * mu_ref[...]
    r, k, v = xx[:, 0:RWKV_WIDTH], xx[:, RWKV_WIDTH:2 * RWKV_WIDTH], xx[:, 2 * RWKV_WIDTH:3 * RWKV_WIDTH]
    lora = xx[:, 3 * RWKV_WIDTH:]
    nx = -(w0_ref[...] + _mm_hi(jnp.tanh(lora), wup_ref[...]))
    w_log = -(jnp.maximum(nx, 0.0) + jnp.log1p(jnp.exp(-jnp.abs(nx)))) - 0.5
    lw_s[...] = -jnp.exp(w_log)
    a = jax.nn.sigmoid(a0_ref[...] + _mm_hi(lora, aup_ref[...]))
    bd = bd_ref[...]
    kk_raw = k * kk_ref[...]
    kk_s[...] = kk_raw / jnp.maximum(jnp.sqrt(_mm_ones(kk_raw * kk_raw, bd)), 1e-12)
    k2 = k * (1.0 + (a - 1.0) * ka_ref[...])
    r_s[...] = r
    k_s[...] = k2
    v_s[...] = v
    a_s[...] = a
    bonus = _mm_ones(r * k2 * rk_ref[...], bd) * v

    rows = lax.broadcasted_iota(jnp.int32, (c, LANE), 0)
    lane_in = lax.broadcasted_iota(jnp.int32, (c, LANE), 1) % RWKV_HEAD
    strict = rows > lane_in
    incl = rows >= lane_in
    eye2 = (rows == lane_in).astype(F32)
    tri = (lax.broadcasted_iota(jnp.int32, (c, c), 0) >= lax.broadcasted_iota(jnp.int32, (c, c), 1)).astype(F32)
    lane1 = lax.broadcasted_iota(jnp.int32, (1, LANE), 1)
    m0, m1 = lane1 < RWKV_HEAD, lane1 >= RWKV_HEAD
    bdm = (lax.broadcasted_iota(jnp.int32, (LANE, LANE), 0) // RWKV_HEAD
           == lax.broadcasted_iota(jnp.int32, (LANE, LANE), 1) // RWKV_HEAD)

    def blockdiag(zz):
        zz = zz.astype(BF16)
        return jnp.concatenate([jnp.where(m0, zz, 0), jnp.where(m1, zz, 0)], axis=0)

    def chunk(ci, carry):
        r0 = pl.multiple_of(ci * c, c)
        ds = pl.ds(r0, c)
        lw = lw_s[ds, :]
        cum = _mm_hi(tri, lw)
        w_in, w_ex, inv_in = jnp.exp(cum), jnp.exp(cum - lw), jnp.exp(-cum)
        w_end = jnp.exp(cum[c - 1:c])
        kk_c, vv = kk_s[ds, :], v_s[ds, :]
        alpha = -kk_c * w_ex
        beta_h = kk_c * a_s[ds, :] * inv_in
        k_h = k_s[ds, :] * inv_in
        r_t = r_s[ds, :] * w_in
        beta_e, k_e = beta_h * w_end, k_h * w_end
        pairs = range(n_pair)
        sls = [slice(p * LANE, (p + 1) * LANE) for p in pairs]
        al = [alpha[:, sl] for sl in sls]
        rt = [r_t[:, sl] for sl in sls]
        vb = [blockdiag(vv[:, sl]) for sl in sls]
        aa = [_mm_nt(jnp.concatenate([al[p], rt[p]], axis=0),
                     jnp.concatenate([blockdiag(beta_h[:, sls[p]]), blockdiag(k_h[:, sls[p]])], axis=0))
              for p in pairs]
        a_ab = [jnp.where(strict, aa[p][0:c, 0:LANE], 0.0) for p in pairs]
        a_ak = [jnp.where(strict, aa[p][0:c, LANE:], 0.0) for p in pairs]
        a_rb = [jnp.where(incl, aa[p][c:, 0:LANE], 0.0) for p in pairs]
        a_rk = [jnp.where(incl, aa[p][c:, LANE:], 0.0) for p in pairs]
        akv = [_mm(a_ak[p], vb[p]) for p in pairs]
        rkv = [_mm(a_rk[p], vb[p]) for p in pairs]
        vk = [_mm_tn(vv[:, sls[p]], k_e[:, sls[p]]) for p in pairs]
        tm = [eye2 + a_ab[p] for p in pairs]
        pw = a_ab
        n = 1
        while 2 * n < c:
            pw = [_mm(pw[p], blockdiag(pw[p])) for p in pairs]
            tm = [tm[p] + _mm(tm[p], blockdiag(pw[p])) for p in pairs]
            n *= 2
        pq = [_mm(tm[p], jnp.concatenate([blockdiag(al[p]), blockdiag(akv[p])], axis=1)) for p in pairs]
        ro = [_mm(a_rb[p], jnp.concatenate([blockdiag(pq[p][:, 0:LANE]), blockdiag(pq[p][:, LANE:])], axis=1))
              for p in pairs]
        mn = [_mm_tn(pq[p], beta_e[:, sls[p]]) for p in pairs]
        outs = []
        for p in pairs:
            m_bd = jnp.where(bdm, mn[p][0:LANE], 0.0)
            n_f = jnp.where(bdm, mn[p][LANE:] + vk[p], 0.0)
            st = st_ref[p]
            outs.append(_mm_nt(rt[p] + ro[p][:, 0:LANE], blockdiag(st)) + ro[p][:, LANE:] + rkv[p])
            st_ref[p] = st * w_end[:, sls[p]] + _mm(st, m_bd) + n_f[0:RWKV_HEAD] + n_f[RWKV_HEAD:]
        o_s[ds, :] = jnp.concatenate(outs, axis=1)
        return carry

    lax.fori_loop(0, tt // c, chunk, 0)
    o = o_s[...]
    inv_n = 1.0 / RWKV_HEAD
    d = o - _mm_ones(o, bd) * inv_n
    var = _mm_ones(d * d, bd) * inv_n
    z = z_ref[0]
    y_ref[0] = (d * lax.rsqrt(var + GN_EPS) * lnw_ref[...] + bonus) * (z * jax.nn.sigmoid(z))

    @pl.when(ti == n_t - 1)
    def _():
        s_ref[0] = st_ref[...]


def _rwkv_prompt(rcols, proj_a3, mu, w0, w_up, a0, a_up, k_k, k_a, r_k, ln_w):
    b, t = rcols.shape[:2]
    tt = min(MIX_TILE, t)
    n_t = t // tt
    n_pair = RWKV_HEADS // 2
    zpad = jnp.zeros((RWKV_LORA_W, RWKV_WIDTH), F32)
    wup = jnp.concatenate([w_up, zpad], axis=0)
    aup = jnp.concatenate([zpad, a_up], axis=0)
    hid = np.arange(RWKV_WIDTH) // RWKV_HEAD
    bd = jnp.asarray((hid[:, None] == hid[None, :]).astype(np.float32)).astype(BF16)
    vec = lambda a: a.reshape(1, -1)
    full = lambda shape: pl.BlockSpec(shape, lambda bi, i: (0,) * len(shape))
    y, s = pl.pallas_call(
        functools.partial(_rwkv_body, n_t),
        grid=(b, n_t),
        in_specs=[pl.BlockSpec((1, tt, SHIFT_W), lambda bi, i: (bi, i, 0)),
                  pl.BlockSpec((1, tt, RWKV_WIDTH), lambda bi, i: (bi, i, A_RZ // RWKV_WIDTH)),
                  full((1, SHIFT_W)), full((1, RWKV_WIDTH)), full((LANE, RWKV_WIDTH)), full((1, RWKV_WIDTH)),
                  full((LANE, RWKV_WIDTH)), full((1, RWKV_WIDTH)), full((1, RWKV_WIDTH)), full((1, RWKV_WIDTH)),
                  full((1, RWKV_WIDTH)), full((RWKV_WIDTH, RWKV_WIDTH))],
        out_specs=[pl.BlockSpec((1, tt, RWKV_WIDTH), lambda bi, i: (bi, i, 0)),
                   pl.BlockSpec((1, n_pair, RWKV_HEAD, LANE), lambda bi, i: (bi, 0, 0, 0))],
        out_shape=[jax.ShapeDtypeStruct((b, t, RWKV_WIDTH), F32),
                   jax.ShapeDtypeStruct((b, n_pair, RWKV_HEAD, LANE), F32)],
        scratch_shapes=[pltpu.VMEM((8, SHIFT_W), F32), pltpu.VMEM((n_pair, RWKV_HEAD, LANE), F32)]
                       + [pltpu.VMEM((tt, RWKV_WIDTH), F32)] * 7,
        compiler_params=_cparams(("arbitrary", "arbitrary")),
        name="rwkv_prompt",
    )(rcols, proj_a3, vec(mu), vec(w0), wup, vec(a0), aup, vec(k_k), vec(k_a), vec(r_k), vec(ln_w), bd)
    s = s.reshape(b, n_pair, RWKV_HEAD, 2, RWKV_HEAD).transpose(0, 1, 3, 2, 4)
    return y, s.reshape(b, RWKV_HEADS, RWKV_HEAD, RWKV_HEAD)


def _merge_body(x_ref, yn_ref, nz_ref, yr_ref, yh_ref, mg_ref, wb_ref, wo_ref, o_ref):
    nz = nz_ref[...]
    branches = (yn_ref[...] * (nz * jax.nn.sigmoid(nz)), yr_ref[...], yh_ref[...])
    acc = jnp.zeros(o_ref.shape, F32)
    for n, y in enumerate(branches):
        t = jnp.dot(y.astype(BF16), wb_ref[n], preferred_element_type=F32)
        acc = acc + jax.nn.sigmoid(mg_ref[:, n * D_MODEL:(n + 1) * D_MODEL]) * t
    o_ref[...] = x_ref[...] + jnp.dot(acc.astype(BF16), wo_ref[...], preferred_element_type=F32)


def _merge(x2d, y_nsa, proj_a, y_rwkv, y_hgrn, mg, wb, wo, tm):
    m = x2d.shape[0]
    row = lambda w: pl.BlockSpec((tm, w), lambda i: (i, 0))
    return pl.pallas_call(
        _merge_body,
        grid=(m // tm,),
        in_specs=[row(D_MODEL), row(BRANCH_WIDTH),
                  pl.BlockSpec((tm, BRANCH_WIDTH), lambda i: (i, A_NZ // BRANCH_WIDTH)),
                  row(BRANCH_WIDTH), row(BRANCH_WIDTH), row(N_BRANCH * D_MODEL),
                  pl.BlockSpec((N_BRANCH, BRANCH_WIDTH, D_MODEL), lambda i: (0, 0, 0)),
                  pl.BlockSpec((D_MODEL, D_MODEL), lambda i: (0, 0))],
        out_specs=row(D_MODEL),
        out_shape=jax.ShapeDtypeStruct((m, D_MODEL), F32),
        compiler_params=_cparams(("arbitrary",)),
        name="merge",
    )(x2d, y_nsa, proj_a, y_rwkv, y_hgrn, mg, wb, wo)


def _split_w_in(w):
    o = np.concatenate([[0], np.cumsum(IN_SIZES)])
    seg = lambda n: w[:, o[n]:o[n + 1]]
    pad = jnp.zeros((w.shape[0], A_Q - A_GATE - IN_SIZES[2]), w.dtype)
    wa = jnp.concatenate([seg(1), seg(2), pad, seg(0), seg(3), seg(5), seg(6), seg(7), seg(8), seg(9)], axis=1)
    return wa.astype(BF16), seg(4).astype(BF16), seg(10).astype(BF16)


def _layer(x, l, past, prm, lb, tabs, tm):
    b, t = x.shape[:2]
    m = b * t
    x2d = x.reshape(m, D_MODEL)
    nw = prm['norm_w'][l].reshape(1, D_MODEL)
    wa, wb_cols, wc = prm['w_split'][l]
    proj_a = _inproj(x2d, nw, wa, tm, 1536)
    rcols = _inproj(x2d, nw, wb_cols, tm, SHIFT_W).reshape(b, t, SHIFT_W)
    mg = _inproj(x2d, nw, wc, tm, 1536)
    seg = lambda off, w: proj_a[:, off:off + w].reshape(b, t, w)
    c_tab, su_tab, sd_tab, cos_t, sin_t = tabs
    if past is None:
        kv_new, win_new, kcsrc, ksb, vst, kwb, vwt = _kvpost(proj_a, (c_tab, su_tab, sd_tab), b, t, min(tm, t), True)
        w_all, bias = _cmp_weights(prm['nsa_cmp_pe'][l], prm['nsa_cmp_w'][l])
        kc, vct = _compress(kcsrc, w_all, bias, b, t)
        gate_t = seg(A_GATE, 3 * NSA_HEADS).transpose(0, 2, 1)
        y_nsa = _nsa_prompt(proj_a, gate_t, cos_t, sin_t, kc, vct, ksb, vst, kwb, vwt, b, t)
        win_state = win_new.reshape(b, t, 2, NSA_KV, HEAD_DIM)[:, -min(WINDOW, t):]
    else:
        cache_kv, page_table, cache_win, s_r, prev, s_h = past
        kv_new, win_new = _kvpost(proj_a, (c_tab, su_tab, sd_tab), b, t, tm, False)
        win5 = win_new.reshape(b, t, 2, NSA_KV, HEAD_DIM)
        win_state = jnp.concatenate([cache_win[l][:, t:], win5], axis=1)
        gates = jax.nn.sigmoid(seg(A_GATE, 3 * NSA_HEADS)).reshape(b, t, NSA_HEADS, 3)
        q = seg(A_Q, NSA_WIDTH).reshape(b, t, NSA_HEADS, HEAD_DIM)
        y_nsa = _nsa_decode_pallas(l, q, gates, cache_kv, page_table, kv_new.reshape(b, t, 512), cache_win,
                                   win_new.reshape(b, t, 256), prm['nsa_cmp_pe'][l], prm['nsa_cmp_w'][l])
    rwkv_prm = (prm['rwkv_mu'][l], prm['rwkv_w0'][l], prm['rwkv_w_up'][l], prm['rwkv_a0'][l], prm['rwkv_a_up'][l],
                prm['rwkv_k_k'][l], prm['rwkv_k_a'][l], prm['rwkv_r_k'][l], prm['rwkv_ln_w'][l])
    if past is None:
        proj_a3 = proj_a.reshape(b, t, A_WIDTH)
        y_rwkv, s_r = _rwkv_prompt(rcols, proj_a3, *rwkv_prm)
        shift_state = rcols[:, -1]
        y_hgrn, s_h = _hgrn_prompt(proj_a3, lb, prm['hgrn_ln_w'][l])
    else:
        y_rwkv, shift_state, s_r = _rwkv_mix(rcols, seg(A_RZ, RWKV_WIDTH), prev, s_r, *rwkv_prm)
        y_hgrn, s_h = _hgrn_mix(seg(A_HQ, HGRN_WIDTH), seg(A_HF, HGRN_WIDTH), seg(A_HI, HGRN_WIDTH),
                                seg(A_HZ, HGRN_WIDTH), s_h, lb, prm['hgrn_ln_w'][l])
    x_new = _merge(x2d, y_nsa.reshape(m, NSA_WIDTH), proj_a, y_rwkv.reshape(m, RWKV_WIDTH), y_hgrn.reshape(m, HGRN_WIDTH),
                   mg, prm['w_branch_bf16'][l], prm['w_out_bf16'][l], min(tm, 512))
    kv_state = kv_new.reshape(b, t, 4, NSA_KV, HEAD_DIM)
    return x_new.reshape(b, t, D_MODEL), (kv_state, win_state, s_r, shift_state, s_h)


def kernel(x_prompt, x_sample, cache_kv, cache_win, state_rwkv, state_shift, state_hgrn, page_table,
           norm_w, w_in, nsa_cmp_pe, nsa_cmp_w, rwkv_mu, rwkv_w0, rwkv_w_up, rwkv_a0, rwkv_a_up,
           rwkv_k_k, rwkv_k_a, rwkv_r_k, rwkv_ln_w, hgrn_lb_logits, hgrn_ln_w, w_branch, w_out, norm_f):
    prm = {'norm_w': norm_w, 'nsa_cmp_pe': nsa_cmp_pe, 'nsa_cmp_w': nsa_cmp_w,
           'rwkv_mu': rwkv_mu, 'rwkv_w0': rwkv_w0, 'rwkv_w_up': rwkv_w_up, 'rwkv_a0': rwkv_a0,
           'rwkv_a_up': rwkv_a_up, 'rwkv_k_k': rwkv_k_k, 'rwkv_k_a': rwkv_k_a, 'rwkv_r_k': rwkv_r_k,
           'rwkv_ln_w': rwkv_ln_w, 'hgrn_ln_w': hgrn_ln_w,
           'w_split': [_split_w_in(w_in[l]) for l in range(DEPTH)],
           'w_branch_bf16': w_branch.astype(BF16), 'w_out_bf16': w_out.astype(BF16)}
    cs = jnp.cumsum(jax.nn.softmax(hgrn_lb_logits, axis=0), axis=0)
    lbs = cs - cs[0:1]
    n_dec, dec_t = x_sample.shape[:2]
    bp, tp = x_prompt.shape[:2]
    tabs_p = _rope_tables(jnp.arange(tp))
    tabs_s = _rope_tables(PAST_LEN + jnp.arange(n_dec * dec_t) % dec_t)
    tm_p = min(1024, bp * tp)
    tm_s = n_dec * dec_t
    xp, xs = x_prompt, x_sample
    outs = [[] for _ in range(10)]
    for l in range(DEPTH):
        xp, st_p = _layer(xp, l, None, prm, lbs[l], tabs_p, tm_p)
        past = (cache_kv, page_table, cache_win, state_rwkv[l], state_shift[l], state_hgrn[l])
        xs, st_s = _layer(xs, l, past, prm, lbs[l], tabs_s, tm_s)
        for n in range(5):
            outs[2 * n].append(st_p[n])
            outs[2 * n + 1].append(st_s[n])
    y_prompt = _final_norm(xp.reshape(bp * tp, D_MODEL), norm_f, tm_p).reshape(xp.shape)
    y_sample = _final_norm(xs.reshape(n_dec * dec_t, D_MODEL), norm_f, tm_s).reshape(xs.shape)
    return (y_prompt, y_sample) + tuple(jnp.stack(o) for o in outs)
```

```python
import functools

import jax
import jax.numpy as jnp
import numpy as np
from jax import lax
from jax.experimental import pallas as pl
from jax.experimental.pallas import tpu as pltpu

F32 = jnp.float32
BF16 = jnp.bfloat16
HI = lax.Precision.HIGHEST

D_MODEL = 1024
DEPTH = 4
PAST_LEN = 2048
PAGE_SIZE = 128
NSA_HEADS = 8
NSA_KV = 2
HEAD_DIM = 64
NSA_HPG = NSA_HEADS // NSA_KV
NSA_WIDTH = NSA_HEADS * HEAD_DIM
ROT_DIM = HEAD_DIM // 4
ROPE_THETA = 500000.0
CMP_BLOCK = 32
CMP_STRIDE = 16
SLC_BLOCK = 64
N_SELECT = 16
WINDOW = 512
Q_BLOCK = 128
RWKV_HEADS = 8
RWKV_HEAD = 64
RWKV_WIDTH = RWKV_HEADS * RWKV_HEAD
RWKV_LORA_W = 64
RWKV_LORA_A = 64
SHIFT_W = 3 * RWKV_WIDTH + RWKV_LORA_W + RWKV_LORA_A
HGRN_HEADS = 4
HGRN_HEAD = 128
HGRN_WIDTH = HGRN_HEADS * HGRN_HEAD
HGRN_CHUNK = 64
N_BRANCH = 3
BRANCH_WIDTH = 512
KV_COLS = 6 * NSA_KV * HEAD_DIM
IN_SIZES = (NSA_WIDTH, KV_COLS, 3 * NSA_HEADS, NSA_WIDTH, SHIFT_W, RWKV_WIDTH,
            HGRN_WIDTH, HGRN_WIDTH, HGRN_WIDTH, HGRN_WIDTH, N_BRANCH * D_MODEL)
RMS_EPS = 1e-6
GN_EPS = 64e-5

LANE = 128
VMEM_LIMIT = 56 * 1024 * 1024
NEG = -1e30
KEY_TILE = 512
LOG2E = 1.4426950408889634
RWKV_CHUNK = 64

A_KV = 0
A_GATE = 768
A_Q = 1024
A_NZ = 1536
A_RZ = 2048
A_HQ = 2560
A_HF = 3072
A_HI = 3584
A_HZ = 4096
A_WIDTH = 4608


def _cparams(sem):
    return pltpu.CompilerParams(dimension_semantics=sem, vmem_limit_bytes=VMEM_LIMIT)


def _inproj_body(x_ref, nw_ref, w_ref, o_ref, h_ref):
    @pl.when(pl.program_id(1) == 0)
    def _():
        x = x_ref[...]
        ms = jnp.mean(x * x, axis=-1, keepdims=True)
        h_ref[...] = (x * lax.rsqrt(ms + RMS_EPS) * nw_ref[...]).astype(BF16)

    o_ref[...] = jnp.dot(h_ref[...], w_ref[...], preferred_element_type=F32)


def _inproj(x2d, nw, w, tm, tn):
    m, d = x2d.shape
    n = w.shape[1]
    return pl.pallas_call(
        _inproj_body,
        grid=(m // tm, n // tn),
        in_specs=[pl.BlockSpec((tm, d), lambda i, j: (i, 0)),
                  pl.BlockSpec((1, d), lambda i, j: (0, 0)),
                  pl.BlockSpec((d, tn), lambda i, j: (0, j))],
        out_specs=pl.BlockSpec((tm, tn), lambda i, j: (i, j)),
        out_shape=jax.ShapeDtypeStruct((m, n), F32),
        scratch_shapes=[pltpu.VMEM((tm, d), BF16)],
        compiler_params=_cparams(("arbitrary", "arbitrary")),
        name="inproj",
    )(x2d, nw, w)


def _rmsnorm_body(x_ref, w_ref, o_ref):
    x = x_ref[...]
    ms = jnp.mean(x * x, axis=-1, keepdims=True)
    o_ref[...] = x * lax.rsqrt(ms + RMS_EPS) * w_ref[...]


def _final_norm(x2d, w, tm):
    m, d = x2d.shape
    return pl.pallas_call(
        _rmsnorm_body,
        grid=(m // tm,),
        in_specs=[pl.BlockSpec((tm, d), lambda i: (i, 0)), pl.BlockSpec((1, d), lambda i: (0, 0))],
        out_specs=pl.BlockSpec((tm, d), lambda i: (i, 0)),
        out_shape=jax.ShapeDtypeStruct((m, d), F32),
        compiler_params=_cparams(("arbitrary",)),
        name="final_norm",
    )(x2d, w.reshape(1, d))


def _rope_tables(pos):
    half = ROT_DIM // 2
    inv = ROPE_THETA ** (-jnp.arange(0, ROT_DIM, 2, dtype=F32) / ROT_DIM)
    ang = pos.astype(F32)[:, None] * inv[None, :]
    cos, sin = jnp.cos(ang), jnp.sin(ang)
    n = pos.shape[0]
    ones = jnp.ones((n, HEAD_DIM - ROT_DIM), F32)
    zeros8 = jnp.zeros((n, half), F32)
    zeros = jnp.zeros((n, HEAD_DIM - ROT_DIM), F32)
    c = jnp.concatenate([cos, cos, ones], axis=1)
    s_up = jnp.concatenate([zeros8, sin, zeros], axis=1)
    s_dn = jnp.concatenate([-sin, zeros8, zeros], axis=1)
    tile = lambda a: jnp.concatenate([a] * NSA_KV, axis=1)
    return tile(c), tile(s_up), tile(s_dn), cos.T, sin.T


def _kvpost_body(with_attn, p_ref, c_ref, su_ref, sd_ref, kv_ref, win_ref, *extra):
    x = p_ref[...]
    c, su, sd = c_ref[...], su_ref[...], sd_ref[...]

    def rope(v):
        return v * c + pltpu.roll(v, ROT_DIM // 2, 1) * su + pltpu.roll(v, LANE - ROT_DIM // 2, 1) * sd

    k_slc = rope(x[:, 256:384])
    k_win = rope(x[:, 512:640])
    kv_ref[:, 0:256] = x[:, 0:256]
    kv_ref[:, 256:384] = k_slc
    kv_ref[:, 384:512] = x[:, 384:512]
    win_ref[:, 0:128] = k_win
    win_ref[:, 128:256] = x[:, 640:768]
    if with_attn:
        kc_ref, ks_ref, vst_ref, kw_ref, vwt_ref = extra
        kc_ref[...] = x[:, 0:256].astype(BF16)
        ks_ref[...] = k_slc.astype(BF16)
        vst_ref[0] = x[:, 384:512].T.astype(BF16)
        kw_ref[...] = k_win.astype(BF16)
        vwt_ref[0] = x[:, 640:768].T.astype(BF16)


def _kvpost(proj_a, tabs, b, t, tm, with_attn):
    m = b * t
    nt = tabs[0].shape[0] // tm
    tab_spec = pl.BlockSpec((tm, LANE), lambda i: (i % nt, 0))
    out_shape = [jax.ShapeDtypeStruct((m, 512), F32), jax.ShapeDtypeStruct((m, 256), F32)]
    out_specs = [pl.BlockSpec((tm, 512), lambda i: (i, 0)), pl.BlockSpec((tm, 256), lambda i: (i, 0))]
    if with_attn:
        out_shape += [jax.ShapeDtypeStruct((m, 256), BF16), jax.ShapeDtypeStruct((m, LANE), BF16),
                      jax.ShapeDtypeStruct((b, LANE, t), BF16), jax.ShapeDtypeStruct((m, LANE), BF16),
                      jax.ShapeDtypeStruct((b, LANE, t), BF16)]
        tspec = pl.BlockSpec((1, LANE, tm), lambda i: (i // nt, 0, i % nt))
        out_specs += [pl.BlockSpec((tm, 256), lambda i: (i, 0)), pl.BlockSpec((tm, LANE), lambda i: (i, 0)),
                      tspec, pl.BlockSpec((tm, LANE), lambda i: (i, 0)), tspec]
    return pl.pallas_call(
        functools.partial(_kvpost_body, with_attn),
        grid=(m // tm,),
        in_specs=[pl.BlockSpec((tm, KV_COLS), lambda i: (i, 0)), tab_spec, tab_spec, tab_spec],
        out_specs=out_specs,
        out_shape=out_shape,
        compiler_params=_cparams(("arbitrary",)),
        name="kvpost",
    )(proj_a, *tabs)


def _cmp_weights(pe, cw):
    eye = jnp.eye(NSA_KV, dtype=F32)
    cols = []
    for s in range(2):
        for part in range(2):
            w = cw[s, part * CMP_STRIDE:(part + 1) * CMP_STRIDE]
            blk = jnp.einsum('jde,gh->jgdhe', w, eye).reshape(CMP_STRIDE, LANE, LANE)
            full = jnp.zeros((CMP_STRIDE, 2, LANE, LANE), F32).at[:, s].set(blk)
            cols.append(full.reshape(CMP_STRIDE * 2 * LANE, LANE))
    w_all = jnp.concatenate(cols, axis=1).astype(BF16)
    bias = jnp.einsum('sjd,sjde->se', pe, cw)
    bias = jnp.concatenate([bias[0], bias[0], bias[1], bias[1]]).reshape(1, 2 * LANE)
    return w_all, bias


def _cmp_body(sub_ref, w_ref, b_ref, kc_ref, vct_ref):
    r = jnp.dot(sub_ref[0], w_ref[...], preferred_element_type=F32)
    n = r.shape[0]
    b = b_ref[...]
    kc = r[:, 0:128] + pltpu.roll(r[:, 128:256], n - 1, 0) + b[:, 0:128]
    vc = r[:, 256:384] + pltpu.roll(r[:, 384:512], n - 1, 0) + b[:, 128:256]
    kc_ref[0] = kc.astype(BF16)
    vct_ref[0] = vc.T.astype(BF16)


def _compress(kcsrc, w_all, bias, b, t):
    n_sub = t // CMP_STRIDE
    sub = kcsrc.reshape(b, n_sub, CMP_STRIDE * 256)
    return pl.pallas_call(
        _cmp_body,
        grid=(b,),
        in_specs=[pl.BlockSpec((1, n_sub, CMP_STRIDE * 256), lambda i: (i, 0, 0)),
                  pl.BlockSpec((CMP_STRIDE * 256, 512), lambda i: (0, 0)),
                  pl.BlockSpec((1, 256), lambda i: (0, 0))],
        out_specs=[pl.BlockSpec((1, n_sub, LANE), lambda i: (i, 0, 0)),
                   pl.BlockSpec((1, LANE, n_sub), lambda i: (i, 0, 0))],
        out_shape=[jax.ShapeDtypeStruct((b, n_sub, LANE), BF16), jax.ShapeDtypeStruct((b, LANE, n_sub), BF16)],
        compiler_params=_cparams(("arbitrary",)),
        name="compress",
    )(sub, w_all, bias)


def _slc_cmp_matrix(n_slc, n_cmp, n_cmp_pad):
    ratio = SLC_BLOCK // CMP_STRIDE
    span = CMP_BLOCK // CMP_STRIDE
    m = np.zeros((n_slc, n_cmp_pad), np.float32)
    for j in range(n_slc):
        for k in range(ratio + span - 1):
            n = ratio * j - (span - 1) + k
            if 0 <= n < n_cmp:
                m[j, n] = 1.0
    return m


def _nsa_body(n_cmp_pad, n_slc, q_ref, gt_ref, cos_ref, sin_ref, kc_ref, vct_ref, ks_ref, vst_ref,
              kw_ref, vwt_ref, mt_ref, o_ref, sc_ref, sel_ref, ml_ref, acc_ref, s_ref, p_ref):
    i = pl.program_id(1)
    q0 = i * Q_BLOCK
    hq = NSA_HPG * Q_BLOCK
    q = q_ref[0]
    cos4 = jnp.concatenate([cos_ref[...]] * NSA_HPG, axis=1)
    sin4 = jnp.concatenate([sin_ref[...]] * NSA_HPG, axis=1)
    gs = jax.nn.sigmoid(gt_ref[0])
    lane = lax.broadcasted_iota(jnp.int32, (Q_BLOCK, Q_BLOCK), 1)
    sub = lax.broadcasted_iota(jnp.int32, (Q_BLOCK, Q_BLOCK), 0)
    sub_k = lax.broadcasted_iota(jnp.int32, (KEY_TILE, Q_BLOCK), 0)
    qp_k = q0 + lax.broadcasted_iota(jnp.int32, (KEY_TILE, Q_BLOCK), 1)
    tpk = KEY_TILE // Q_BLOCK
    bpt = KEY_TILE // SLC_BLOCK
    zeros_g = jnp.zeros((HEAD_DIM, hq), F32)
    half = ROT_DIM // 2

    for g in range(NSA_KV):
        xa = q[:, g * 256:g * 256 + 128].T
        xb = q[:, g * 256 + 128:g * 256 + 256].T
        qn = jnp.concatenate([xa[0:64], xa[64:128], xb[0:64], xb[64:128]], axis=1) * (HEAD_DIM ** -0.5 * LOG2E)
        x1, x2 = qn[0:half], qn[half:ROT_DIM]
        qr = jnp.concatenate([x1 * cos4 - x2 * sin4, x2 * cos4 + x1 * sin4, qn[ROT_DIM:]], axis=0)

        def pad(a):
            parts = [zeros_g] * NSA_KV
            parts[g] = a
            return jnp.concatenate(parts, axis=0).astype(BF16)

        qn_p, qr_p = pad(qn), pad(qr)

        def compressed(n_rows):
            s1 = jnp.dot(kc_ref[0, 0:n_rows, :], qn_p, preferred_element_type=F32)
            n_idx = lax.broadcasted_iota(jnp.int32, (n_rows, Q_BLOCK), 0)
            lane_c = lax.broadcasted_iota(jnp.int32, (n_rows, Q_BLOCK), 1)
            cmp_ok = (CMP_STRIDE * n_idx + CMP_BLOCK - 1) <= (q0 + lane_c)
            ps = []
            for h in range(NSA_HPG):
                s = jnp.where(cmp_ok, s1[:, h * Q_BLOCK:(h + 1) * Q_BLOCK], NEG)
                m = jnp.max(s, axis=0, keepdims=True)
                e = jnp.where(cmp_ok, jnp.exp2(s - m), 0.0)
                d = jnp.maximum(jnp.sum(e, axis=0, keepdims=True), 1e-30)
                ps.append(e / d)
            imp = ps[0] + ps[1] + ps[2] + ps[3]
            p1 = jnp.concatenate(ps, axis=1).astype(BF16)
            o_c = jnp.dot(vct_ref[0, g * HEAD_DIM:(g + 1) * HEAD_DIM, 0:n_rows], p1, preferred_element_type=F32)
            return o_c, _mm_ones_rhs(mt_ref[:, 0:n_rows], imp)

        half_rows = n_cmp_pad // 2
        if half_rows % LANE == 0:
            o_cmp, imps = lax.cond(CMP_STRIDE * half_rows + CMP_BLOCK - 1 > q0 + Q_BLOCK - 1,
                                   lambda: compressed(half_rows), lambda: compressed(n_cmp_pad))
        else:
            o_cmp, imps = compressed(n_cmp_pad)

        blk = lax.broadcasted_iota(jnp.int32, (n_slc, Q_BLOCK), 0)
        qp_s = q0 + lax.broadcasted_iota(jnp.int32, (n_slc, Q_BLOCK), 1)
        causal = blk * SLC_BLOCK <= qp_s
        qblk = qp_s // SLC_BLOCK
        forced = (blk == 0) | (blk == qblk) | (blk == qblk - 1)
        score = jnp.where(causal, jnp.where(forced, jnp.inf, imps), -jnp.inf)
        sc_ref[...] = score

        def rank_step(jh, cnt):
            for j in (2 * jh, 2 * jh + 1):
                row = sc_ref[pl.ds(j, 1), :]
                ge = jnp.where(row >= score, 1.0, 0.0)
                gt = jnp.where(row > score, 1.0, 0.0)
                cnt = cnt + jnp.where(blk > j, ge, gt)
            return cnt

        cnt = lax.fori_loop(0, jnp.minimum(i + 1, n_slc // 2), rank_step, jnp.zeros((n_slc, Q_BLOCK), F32))
        sel_ref[...] = jnp.where(cnt < N_SELECT, jnp.where(score > -jnp.inf, 0.0, NEG), NEG)

        init = (jnp.full((1, hq), NEG, F32), jnp.zeros((1, hq), F32), jnp.zeros((HEAD_DIM, hq), F32))

        def scores(k_ref, row0, n_rows):
            return jnp.dot(k_ref[0, pl.ds(pl.multiple_of(row0, KEY_TILE), n_rows), :], qr_p,
                           preferred_element_type=F32)

        def softmax_tile(s, bias, m, l):
            ps, alphas, ms, ls = [], [], [], []
            for h in range(NSA_HPG):
                hs = slice(h * Q_BLOCK, (h + 1) * Q_BLOCK)
                sh = s[:, hs] + bias
                m_new = jnp.maximum(m[:, hs], jnp.max(sh, axis=0, keepdims=True))
                alpha = jnp.exp2(m[:, hs] - m_new)
                p = jnp.exp2(sh - m_new)
                ms.append(m_new)
                ls.append(alpha * l[:, hs] + jnp.sum(p, axis=0, keepdims=True))
                ps.append(p.astype(BF16))
                alphas.append(alpha)
            cat = lambda parts: jnp.concatenate(parts, axis=1)
            return cat(ms), cat(ls), cat(alphas), cat(ps)

        def values_t(vt_ref, t):
            return vt_ref[0, g * HEAD_DIM:(g + 1) * HEAD_DIM, pl.ds(pl.multiple_of(t * KEY_TILE, KEY_TILE), KEY_TILE)]

        def attend(s, vt_ref, t, bias, state):
            m, l, acc = state
            m, l, alpha, p = softmax_tile(s, bias, m, l)
            return m, l, acc * alpha + jnp.dot(values_t(vt_ref, t), p, preferred_element_type=F32)

        n_pairs = (q0 + Q_BLOCK + 2 * KEY_TILE - 1) // (2 * KEY_TILE)
        last = 2 * n_pairs - 1

        def slc_tile(u, slot):
            s_next = scores(ks_ref, jnp.minimum(u + 1, last) * KEY_TILE, KEY_TILE)
            pv = jnp.dot(values_t(vst_ref, jnp.maximum(u - 1, 0)), p_ref[1 - slot], preferred_element_type=F32)
            acc_ref[...] = acc_ref[...] * ml_ref[2:3, :] + jnp.where(u > 0, pv, 0.0)
            rows = [jnp.broadcast_to(sel_ref[pl.ds(bpt * u + jj, 1), :], (SLC_BLOCK, Q_BLOCK)) for jj in range(bpt)]
            bias = jnp.where(u * KEY_TILE + sub_k <= qp_k, jnp.concatenate(rows, axis=0), NEG)
            m, l, alpha, p = softmax_tile(s_ref[slot], bias, ml_ref[0:1, :], ml_ref[1:2, :])
            ml_ref[0:1, :], ml_ref[1:2, :], ml_ref[2:3, :] = m, l, alpha
            p_ref[slot] = p
            s_ref[1 - slot] = s_next

        def slc_step(k, carry):
            slc_tile(2 * k, 0)
            slc_tile(2 * k + 1, 1)
            return carry

        ml_ref[0:1, :], ml_ref[1:2, :], acc_ref[...] = init
        ml_ref[2:3, :] = jnp.ones((1, hq), F32)
        s_ref[0] = scores(ks_ref, 0, KEY_TILE)
        lax.fori_loop(0, n_pairs, slc_step, 0)
        pv = jnp.dot(values_t(vst_ref, last), p_ref[1], preferred_element_type=F32)
        o_slc = (acc_ref[...] * ml_ref[2:3, :] + pv) / ml_ref[1:2, :]

        state = init
        for k in range(WINDOW // KEY_TILE + 1):
            t = i // tpk - k
            tc = jnp.maximum(t, 0)
            kp = tc * KEY_TILE + sub_k
            ok = jnp.where(kp <= qp_k, qp_k - kp, WINDOW) < jnp.where(t >= 0, WINDOW, 0)
            state = attend(scores(kw_ref, tc * KEY_TILE, KEY_TILE), vwt_ref, tc, jnp.where(ok, 0.0, NEG), state)
        o_win = state[2] / state[1]

        def gate(jj):
            return jnp.concatenate([gs[(g * NSA_HPG + h) * 3 + jj:(g * NSA_HPG + h) * 3 + jj + 1, :]
                                    for h in range(NSA_HPG)], axis=1)

        o_t = gate(0) * o_cmp + gate(1) * o_slc + gate(2) * o_win
        ya = jnp.concatenate([o_t[:, 0:128], o_t[:, 128:256]], axis=0).T
        yb = jnp.concatenate([o_t[:, 256:384], o_t[:, 384:512]], axis=0).T
        o_ref[0, :, g * 256:g * 256 + 128] = ya
        o_ref[0, :, g * 256 + 128:g * 256 + 256] = yb


def _nsa_prompt(proj_a, gate_t, cos_t, sin_t, kc, vct, ksb, vst, kwb, vwt, b, t):
    nb = t // Q_BLOCK
    n_sub = t // CMP_STRIDE
    n_slc = t // SLC_BLOCK
    mt = jnp.asarray(_slc_cmp_matrix(n_slc, n_sub - 1, n_sub)).astype(BF16)
    seq = lambda w: pl.BlockSpec((1, t, w), lambda bi, i: (bi, 0, 0))
    seq_t = pl.BlockSpec((1, LANE, t), lambda bi, i: (bi, 0, 0))
    return pl.pallas_call(
        functools.partial(_nsa_body, n_sub, n_slc),
        grid=(b, nb),
        in_specs=[pl.BlockSpec((1, Q_BLOCK, NSA_WIDTH), lambda bi, i: (bi, i, A_Q // NSA_WIDTH)),
                  pl.BlockSpec((1, 3 * NSA_HEADS, Q_BLOCK), lambda bi, i: (bi, 0, i)),
                  pl.BlockSpec((ROT_DIM // 2, Q_BLOCK), lambda bi, i: (0, i)),
                  pl.BlockSpec((ROT_DIM // 2, Q_BLOCK), lambda bi, i: (0, i)),
                  pl.BlockSpec((1, n_sub, LANE), lambda bi, i: (bi, 0, 0)),
                  pl.BlockSpec((1, LANE, n_sub), lambda bi, i: (bi, 0, 0)),
                  seq(LANE), seq_t, seq(LANE), seq_t,
                  pl.BlockSpec((n_slc, n_sub), lambda bi, i: (0, 0))],
        out_specs=pl.BlockSpec((1, Q_BLOCK, NSA_WIDTH), lambda bi, i: (bi, i, 0)),
        out_shape=jax.ShapeDtypeStruct((b, t, NSA_WIDTH), F32),
        scratch_shapes=[pltpu.VMEM((n_slc, Q_BLOCK), F32), pltpu.VMEM((n_slc, Q_BLOCK), F32),
                        pltpu.VMEM((8, NSA_HPG * Q_BLOCK), F32), pltpu.VMEM((HEAD_DIM, NSA_HPG * Q_BLOCK), F32),
                        pltpu.VMEM((2, KEY_TILE, NSA_HPG * Q_BLOCK), F32),
                        pltpu.VMEM((2, KEY_TILE, NSA_HPG * Q_BLOCK), BF16)],
        compiler_params=_cparams(("arbitrary", "arbitrary")),
        name="nsa_prompt",
    )(proj_a.reshape(b, t, A_WIDTH), gate_t, cos_t, sin_t, kc, vct,
      ksb.reshape(b, t, LANE), vst, kwb.reshape(b, t, LANE), vwt, mt)


def _masked_softmax(s, mask):
    s = jnp.where(mask, s, -jnp.inf)
    m = jnp.max(s, axis=-1, keepdims=True)
    m = jnp.where(jnp.isfinite(m), m, 0.0)
    e = jnp.exp(s - m)
    return e / jnp.maximum(jnp.sum(e, axis=-1, keepdims=True), 1e-30)


def _rope_rows(x, pos):
    half = ROT_DIM // 2
    inv = ROPE_THETA ** (-jnp.arange(0, ROT_DIM, 2, dtype=F32) / ROT_DIM)
    ang = pos.astype(F32)[:, None] * inv[None, :]
    cos = jnp.cos(ang)[None, :, None, :]
    sin = jnp.sin(ang)[None, :, None, :]
    x1, x2 = x[..., :half], x[..., half:ROT_DIM]
    return jnp.concatenate([x1 * cos - x2 * sin, x2 * cos + x1 * sin, x[..., ROT_DIM:]], axis=-1)


def _nsa_decode(q, gates, kv_all, win_all, pe, cw):
    b, tq = q.shape[:2]
    tk = kv_all.shape[1]
    scale = HEAD_DIM ** -0.5
    qpos = PAST_LEN + jnp.arange(tq)
    qn = q.reshape(b, tq, NSA_KV, NSA_HPG, HEAD_DIM)
    qr = _rope_rows(q, qpos).reshape(b, tq, NSA_KV, NSA_HPG, HEAD_DIM)
    n_sub = tk // CMP_STRIDE
    n_cmp = n_sub - 1

    def compress(rows, s):
        sub = rows[:, :n_sub * CMP_STRIDE].reshape(b, n_sub, CMP_STRIDE, NSA_KV, HEAD_DIM)
        a = jnp.einsum('bnjgd,jde->bnge', sub, cw[s, :CMP_STRIDE])
        c = jnp.einsum('bnjgd,jde->bnge', sub, cw[s, CMP_STRIDE:])
        bias = jnp.einsum('jd,jde->e', pe[s], cw[s])
        return a[:, :n_cmp] + c[:, 1:] + bias

    kc = compress(kv_all[:, :, 0], 0)
    vc = compress(kv_all[:, :, 1], 1)
    cmp_end = CMP_STRIDE * jnp.arange(n_cmp) + CMP_BLOCK - 1
    s1 = jnp.einsum('bqghd,bngd->bqghn', qn, kc) * scale
    p1 = _masked_softmax(s1, (cmp_end[None, :] <= qpos[:, None])[None, :, None, None, :])
    o_cmp = jnp.einsum('bqghn,bngd->bqghd', p1, vc)
    n_slc = -(-tk // SLC_BLOCK)
    mt = jnp.asarray(_slc_cmp_matrix(n_slc, n_cmp, n_cmp))
    imp = jnp.einsum('bqgn,jn->bqgj', jnp.sum(p1, axis=3), mt, precision=HI)
    blk = jnp.arange(n_slc)
    qblk = qpos // SLC_BLOCK
    causal = (blk[None, :] * SLC_BLOCK <= qpos[:, None])[None, :, None, :]
    forced = ((blk[None, :] == 0) | (blk[None, :] == qblk[:, None]) | (blk[None, :] == qblk[:, None] - 1))[None, :, None, :]
    score = jnp.where(causal, jnp.where(forced, jnp.inf, imp), -jnp.inf)
    ahead = (score[..., None, :] > score[..., :, None]) | (
        (score[..., None, :] == score[..., :, None]) & (blk[None, :] < blk[:, None]))
    rank = jnp.sum(ahead, axis=-1)
    sel = (rank < N_SELECT) & (score > -jnp.inf)
    kpos = jnp.arange(tk)
    key_ok = sel[..., kpos // SLC_BLOCK] & (kpos[None, :] <= qpos[:, None])[None, :, None, :]
    s2 = jnp.einsum('bqghd,bkgd->bqghk', qr, kv_all[:, :, 2]) * scale
    p2 = _masked_softmax(s2, key_ok[:, :, :, None, :])
    o_slc = jnp.einsum('bqghk,bkgd->bqghd', p2, kv_all[:, :, 3])
    wpos = PAST_LEN - WINDOW + jnp.arange(win_all.shape[1])
    m3 = (wpos[None, :] <= qpos[:, None]) & (qpos[:, None] - wpos[None, :] < WINDOW)
    s3 = jnp.einsum('bqghd,bkgd->bqghk', qr, win_all[:, :, 0]) * scale
    p3 = _masked_softmax(s3, m3[None, :, None, None, :])
    o_win = jnp.einsum('bqghk,bkgd->bqghd', p3, win_all[:, :, 1])
    g = gates.reshape(b, tq, NSA_KV, NSA_HPG, 3)
    o = g[..., 0:1] * o_cmp + g[..., 1:2] * o_slc + g[..., 2:3] * o_win
    return o.reshape(b, tq, NSA_WIDTH)


N_PAGES = PAST_LEN // PAGE_SIZE
DEC_ROWS = 8
DEC_QROWS = NSA_HEADS * 4


DEC_SEQS = 2


def _nsa_dec_body(dec_t, pt_ref, *refs):
    del pt_ref
    for sq in range(DEC_SEQS):
        _nsa_dec_one(dec_t, sq, refs[sq * N_PAGES:(sq + 1) * N_PAGES], *refs[DEC_SEQS * N_PAGES:])


def _nsa_dec_one(dec_t, sq, pages, kvn_ref, cw_ref, wn_ref, qn_ref, qr_ref, g_ref, w_ref, b_ref, mt_ref, o_ref,
                 ck_ref, cv_ref):
    qn, qr = qn_ref[sq], qr_ref[sq]
    nq = qn.shape[0]
    lane = lax.broadcasted_iota(jnp.int32, (nq, LANE), 1)
    qi = lax.broadcasted_iota(jnp.int32, (nq, LANE), 0) % dec_t
    qpos = PAST_LEN + qi
    n_sub = PAST_LEN // CMP_STRIDE

    for p in range(N_PAGES):
        ck_ref[sq, p * PAGE_SIZE:(p + 1) * PAGE_SIZE, :] = pages[p][0, 0, :, 0:128]
        cv_ref[sq, p * PAGE_SIZE:(p + 1) * PAGE_SIZE, :] = pages[p][0, 0, :, 128:256]
    parts = []
    for j in range(CMP_STRIDE):
        parts += [ck_ref[sq, pl.ds(j, n_sub, stride=CMP_STRIDE), :], cv_ref[sq, pl.ds(j, n_sub, stride=CMP_STRIDE), :]]
    r = _mm(jnp.concatenate(parts, axis=1), w_ref[...])
    bias = b_ref[...]
    kc = r[:, 0:128] + pltpu.roll(r[:, 128:256], n_sub - 1, 0) + bias[:, 0:128]
    vc = r[:, 256:384] + pltpu.roll(r[:, 384:512], n_sub - 1, 0) + bias[:, 128:256]

    def softmax_rows(scores, oks):
        scores = [jnp.where(ok, s, NEG) for s, ok in zip(scores, oks)]
        m = functools.reduce(jnp.maximum, [jnp.max(s, axis=1, keepdims=True) for s in scores])
        es = [jnp.where(ok, jnp.exp(s - m), 0.0) for s, ok in zip(scores, oks)]
        d = jnp.maximum(functools.reduce(jnp.add, [jnp.sum(e, axis=1, keepdims=True) for e in es]), 1e-30)
        return [e / d for e in es]

    cmp_ok = (CMP_STRIDE * lane + CMP_BLOCK - 1) <= qpos
    (p1,) = softmax_rows([_mm_nt(qn, kc)], [cmp_ok])
    o_cmp = _mm(p1, vc)
    gq = NSA_KV * dec_t
    imp = functools.reduce(jnp.add, [p1[h * gq:(h + 1) * gq] for h in range(NSA_HPG)])
    imps = _mm_hi(imp, mt_ref[...])
    blk = lane[0:gq]
    qp8 = qpos[0:gq]
    qblk = qp8 // SLC_BLOCK
    causal = blk * SLC_BLOCK <= qp8
    forced = (blk == 0) | (blk == qblk) | (blk == qblk - 1)
    score = jnp.where(causal, jnp.where(forced, jnp.inf, imps), -jnp.inf)
    n_slc = -(-(PAST_LEN + dec_t) // SLC_BLOCK)
    cnt = jnp.zeros((gq, LANE), F32)
    for j in range(n_slc):
        col = score[:, j:j + 1]
        cnt = cnt + jnp.where(blk > j, jnp.where(col >= score, 1.0, 0.0), jnp.where(col > score, 1.0, 0.0))
    sel = jnp.where(cnt < N_SELECT, jnp.where(score > -jnp.inf, 1.0, 0.0), 0.0)

    def sel_rows(j0):
        pick = jnp.where(lane[0:gq] < SLC_BLOCK, sel[:, j0:j0 + 1], sel[:, j0 + 1:j0 + 2])
        return jnp.concatenate([pick] * NSA_HPG, axis=0) > 0.5

    zeros_k = jnp.zeros((PAGE_SIZE - DEC_ROWS, LANE), F32)
    new_ok = (lane < dec_t) & (lane <= qi)
    kvn = kvn_ref[sq]
    s_list = [_mm_nt(qr, pages[p][0, 0, :, 256:384]) for p in range(N_PAGES)]
    s_list.append(_mm_nt(qr, jnp.concatenate([kvn[:, 256:384], zeros_k], axis=0)))
    ok_list = [sel_rows(2 * p) for p in range(N_PAGES)]
    own = jnp.concatenate([sel[:, 2 * N_PAGES:2 * N_PAGES + 1]] * NSA_HPG, axis=0) > 0.5
    ok_list.append(new_ok & own)
    p_list = softmax_rows(s_list, ok_list)
    o_slc = functools.reduce(jnp.add, [_mm(p_list[p], pages[p][0, 0, :, 384:512]) for p in range(N_PAGES)])
    o_slc = o_slc + _mm(p_list[N_PAGES], jnp.concatenate([kvn[:, 384:512], zeros_k], axis=0))

    n_wt = WINDOW // PAGE_SIZE
    wn = wn_ref[sq]
    s_list, ok_list = [], []
    for t in range(n_wt):
        s_list.append(_mm_nt(qr, cw_ref[0, sq, t * PAGE_SIZE:(t + 1) * PAGE_SIZE, 0:128]))
        ok_list.append(qpos - (PAST_LEN - WINDOW + t * PAGE_SIZE + lane) < WINDOW)
    s_list.append(_mm_nt(qr, jnp.concatenate([wn[:, 0:128], zeros_k], axis=0)))
    ok_list.append(new_ok)
    p_list = softmax_rows(s_list, ok_list)
    o_win = functools.reduce(jnp.add, [_mm(p_list[t], cw_ref[0, sq, t * PAGE_SIZE:(t + 1) * PAGE_SIZE, 128:256])
                                       for t in range(n_wt)])
    o_win = o_win + _mm(p_list[n_wt], jnp.concatenate([wn[:, 128:256], zeros_k], axis=0))
    g = g_ref[sq]
    o_ref[sq] = g[:, 0:128] * o_cmp + g[:, 128:256] * o_slc + g[:, 256:384] * o_win


def _nsa_decode_pallas(l, q, gates, cache_kv, page_table, kv_new, cache_win, win_new, pe, cw):
    b, tq = q.shape[:2]
    scale = HEAD_DIM ** -0.5
    eye = jnp.eye(NSA_KV, dtype=F32)

    def rows(x):
        x5 = x.reshape(b, tq, NSA_KV, NSA_HPG, HEAD_DIM).transpose(0, 3, 2, 1, 4)
        return jnp.einsum('bhgqd,gk->bhgqkd', x5, eye).reshape(b, DEC_QROWS, LANE).astype(BF16)

    qn = rows(q * scale)
    qr = rows(_rope_rows(q, PAST_LEN + jnp.arange(tq)) * scale)
    g5 = gates.reshape(b, tq, NSA_KV, NSA_HPG, 3).transpose(0, 3, 2, 1, 4)
    lane_g = jnp.repeat(eye, HEAD_DIM, axis=1)
    gate_b = jnp.einsum('bhgqj,gn->bhgqjn', g5, lane_g).reshape(b, DEC_QROWS, 3 * LANE)
    w_all, bias = _cmp_weights(pe, cw)
    n_cmp = PAST_LEN // CMP_STRIDE - 1
    n_slc = -(-(PAST_LEN + tq) // SLC_BLOCK)
    mt = np.zeros((LANE, LANE), np.float32)
    mt[:n_cmp, :n_slc] = _slc_cmp_matrix(n_slc, n_cmp, n_cmp).T
    pad_rows = lambda a: jnp.pad(a, ((0, 0), (0, DEC_ROWS - tq), (0, 0)))
    n_pool = cache_kv.shape[1]
    ckv = cache_kv.reshape(DEPTH, n_pool, PAGE_SIZE, 512)
    page_spec = lambda sq, p: pl.BlockSpec((1, 1, PAGE_SIZE, 512),
                                           lambda bi, pt: (l, pt[bi * DEC_SEQS + sq, p], 0, 0))
    per_seq = lambda r, w: pl.BlockSpec((DEC_SEQS, r, w), lambda bi, pt: (bi, 0, 0))
    const = lambda shape: pl.BlockSpec(shape, lambda bi, pt: (0,) * len(shape))
    grid_spec = pltpu.PrefetchScalarGridSpec(
        num_scalar_prefetch=1,
        grid=(b // DEC_SEQS,),
        in_specs=[page_spec(sq, p) for sq in range(DEC_SEQS) for p in range(N_PAGES)]
                 + [per_seq(DEC_ROWS, 512),
                    pl.BlockSpec((1, DEC_SEQS, WINDOW, 256), lambda bi, pt: (l, bi, 0, 0)),
                    per_seq(DEC_ROWS, 256), per_seq(DEC_QROWS, LANE), per_seq(DEC_QROWS, LANE),
                    per_seq(DEC_QROWS, 3 * LANE), const((CMP_STRIDE * 256, 512)), const((1, 256)), const((LANE, LANE))],
        out_specs=per_seq(DEC_QROWS, LANE),
        scratch_shapes=[pltpu.VMEM((DEC_SEQS, PAST_LEN, LANE), F32), pltpu.VMEM((DEC_SEQS, PAST_LEN, LANE), F32)],
    )
    o = pl.pallas_call(
        functools.partial(_nsa_dec_body, tq),
        grid_spec=grid_spec,
        out_shape=jax.ShapeDtypeStruct((b, DEC_QROWS, LANE), F32),
        compiler_params=_cparams(("arbitrary",)),
        name="nsa_decode",
    )(page_table, *([ckv] * (DEC_SEQS * N_PAGES)), pad_rows(kv_new), cache_win.reshape(DEPTH, b, WINDOW, 256), pad_rows(win_new),
      qn, qr, gate_b, w_all, bias, jnp.asarray(mt))
    o6 = o.reshape(b, NSA_HPG, NSA_KV, tq, NSA_KV, HEAD_DIM)
    return jnp.einsum('bhgqkd,gk->bqghd', o6, eye).reshape(b, tq, NSA_WIDTH)


def _rwkv_mix(cols, z, prev, s0, mu, w0, w_up, a0, a_up, k_k, k_a, r_k, ln_w):
    b, t = cols.shape[:2]
    shifted = jnp.concatenate([prev[:, None], cols[:, :-1]], axis=1)
    xx = cols + (shifted - cols) * mu
    sp = np.cumsum([RWKV_WIDTH, RWKV_WIDTH, RWKV_WIDTH, RWKV_LORA_W])
    r, k, v, wd, ad = jnp.split(xx, sp, axis=-1)
    w = -jax.nn.softplus(-(w0 + jnp.dot(jnp.tanh(wd), w_up, precision=HI))) - 0.5
    logw = -jnp.exp(w)
    a = jax.nn.sigmoid(a0 + jnp.dot(ad, a_up, precision=HI))

    def heads(u):
        return u.reshape(b, t, RWKV_HEADS, RWKV_HEAD)

    kk = heads(k * k_k)
    kk = kk / jnp.maximum(jnp.sqrt(jnp.sum(kk * kk, axis=-1, keepdims=True)), 1e-12)
    k = k * (1.0 + (a - 1.0) * k_a)
    r, k, v, logw, a = heads(r), heads(k), heads(v), heads(logw), heads(a)
    c = min(RWKV_CHUNK, t)
    nc = t // c

    def chunks(u):
        return u.reshape(b, nc, c, RWKV_HEADS, RWKV_HEAD).transpose(1, 0, 3, 2, 4)

    r, k, v, logw, kk, a = (chunks(u) for u in (r, k, v, logw, kk, a))
    cum = jnp.cumsum(logw, axis=3)
    w_in = jnp.exp(cum)
    w_ex = jnp.exp(cum - logw)
    w_end = w_in[..., -1:, :]
    alpha_t = -kk * w_ex
    beta_h = kk * a / w_in
    k_h = k / w_in
    r_t = r * w_in
    tri_s = jnp.tril(jnp.ones((c, c), F32), -1)
    tri_i = jnp.tril(jnp.ones((c, c), F32))
    mm = functools.partial(jnp.einsum, precision=HI)
    a_ab = mm('nbhtk,nbhsk->nbhts', alpha_t, beta_h) * tri_s
    a_ak = mm('nbhtk,nbhsk->nbhts', alpha_t, k_h) * tri_s
    a_rb = mm('nbhtk,nbhsk->nbhts', r_t, beta_h) * tri_i
    a_rk = mm('nbhtk,nbhsk->nbhts', r_t, k_h) * tri_i
    eye = jnp.eye(c, dtype=F32)
    tm = eye + a_ab
    pw = a_ab
    steps = 1
    while steps * 2 < c:
        pw = mm('nbhts,nbhsu->nbhtu', pw, pw)
        tm = mm('nbhts,nbhsu->nbhtu', tm, eye + pw)
        steps *= 2
    p_m = mm('nbhts,nbhsk->nbhtk', tm, alpha_t)
    q_m = mm('nbhts,nbhsv->nbhtv', tm, mm('nbhts,nbhsv->nbhtv', a_ak, v))
    b_t = beta_h * w_end
    k_t = k_h * w_end
    m_m = mm('nbhtk,nbhtj->nbhkj', p_m, b_t)
    n_m = mm('nbhtv,nbhtk->nbhvk', q_m, b_t) + mm('nbhtv,nbhtk->nbhvk', v, k_t)
    r_p = r_t + mm('nbhts,nbhsk->nbhtk', a_rb, p_m)
    o_p = mm('nbhts,nbhsv->nbhtv', a_rb, q_m) + mm('nbhts,nbhsv->nbhtv', a_rk, v)

    def step(s, inp):
        m_c, n_c, we_c, rp_c, op_c = inp
        o = mm('bhtk,bhvk->bhtv', rp_c, s) + op_c
        s = s * we_c + mm('bhvk,bhkj->bhvj', s, m_c) + n_c
        return s, o

    s_t, o = lax.scan(step, s0, (m_m, n_m, w_end, r_p, o_p))
    o = o.transpose(1, 0, 3, 2, 4).reshape(b, t, RWKV_HEADS, RWKV_HEAD)
    r, k, v = (u.transpose(1, 0, 3, 2, 4).reshape(b, t, RWKV_HEADS, RWKV_HEAD) for u in (r, k, v))
    mean = jnp.mean(o, axis=-1, keepdims=True)
    var = jnp.mean(jnp.square(o - mean), axis=-1, keepdims=True)
    o = (o - mean) * lax.rsqrt(var + GN_EPS) * ln_w.reshape(RWKV_HEADS, RWKV_HEAD)
    o = o + jnp.sum(r * k * r_k, axis=-1, keepdims=True) * v
    y = o.reshape(b, t, RWKV_WIDTH) * jax.nn.silu(z)
    return y, cols[:, -1], s_t


def _hgrn_mix(q, fz, i, z, s0, lb, ln_w):
    b, t = q.shape[:2]
    log_f = jnp.logaddexp(jnp.log(lb), jnp.log1p(-lb) + jax.nn.log_sigmoid(fz))
    kf = (1.0 - lb) * jax.nn.sigmoid(-fz)
    chunk = HGRN_CHUNK if t % HGRN_CHUNK == 0 else t
    nc = t // chunk

    def chunks(u):
        return u.reshape(b, nc, chunk, HGRN_HEADS, HGRN_HEAD).transpose(1, 0, 3, 2, 4)

    mask = jnp.tril(jnp.ones((chunk, chunk), dtype=bool))

    def step(s, inp):
        q_c, lf_c, k_c, i_c = inp
        cl = jnp.cumsum(lf_c, axis=2)
        dec = jnp.exp(jnp.where(mask[:, :, None], cl[:, :, :, None, :] - cl[:, :, None, :, :], -jnp.inf))
        att = jnp.einsum('bhtk,bhtsk,bhsk->bhts', q_c, dec, k_c)
        o = jnp.einsum('bhts,bhsv->bhtv', att, i_c) + jnp.einsum('bhtk,bhkv->bhtv', q_c * jnp.exp(cl), s)
        cl_end = cl[:, :, -1:]
        s = jnp.exp(cl_end[:, :, 0])[..., None] * s + jnp.einsum('bhsk,bhsv->bhkv', k_c * jnp.exp(cl_end - cl), i_c)
        return s, o

    s_t, o = lax.scan(step, s0, (chunks(q), chunks(log_f), chunks(kf), chunks(i)))
    o = o.transpose(1, 0, 3, 2, 4).reshape(b, t, HGRN_HEADS, HGRN_HEAD)
    o = o * lax.rsqrt(jnp.mean(o * o, axis=-1, keepdims=True) + RMS_EPS) * ln_w.reshape(HGRN_HEADS, HGRN_HEAD)
    y = o.reshape(b, t, HGRN_WIDTH) * jax.nn.silu(z)
    return y, s_t


def _mm(a, b):
    return jnp.dot(a.astype(BF16), b.astype(BF16), preferred_element_type=F32)


def _mm_nt(a, b):
    return lax.dot_general(a.astype(BF16), b.astype(BF16), (((1,), (1,)), ((), ())), preferred_element_type=F32)


def _mm_tn(a, b):
    return lax.dot_general(a.astype(BF16), b.astype(BF16), (((0,), (0,)), ((), ())), preferred_element_type=F32)


def _mm_hi(a, b):
    return jnp.dot(a, b, preferred_element_type=F32, precision=HI)


def _mm_ones_rhs(ones_bf16, b):
    hi = b.astype(BF16)
    lo = (b - hi.astype(F32)).astype(BF16)
    return (jnp.dot(ones_bf16, hi, preferred_element_type=F32) + jnp.dot(ones_bf16, lo, preferred_element_type=F32))


def _mm_ones(a, ones_bf16):
    hi = a.astype(BF16)
    lo = (a - hi.astype(F32)).astype(BF16)
    return (jnp.dot(hi, ones_bf16, preferred_element_type=F32) + jnp.dot(lo, ones_bf16, preferred_element_type=F32))


def _log_sigmoid(x):
    return jnp.minimum(x, 0.0) - jnp.log1p(jnp.exp(-jnp.abs(x)))


HGRN_SUB = 16
MIX_TILE = 512


def _hgrn_body(n_t, q_ref, f_ref, i_ref, z_ref, lbc_ref, lnw_ref, y_ref, s_ref, st_ref, o_ref):
    ti = pl.program_id(1)
    c, sc = HGRN_CHUNK, HGRN_SUB
    heads = range(HGRN_HEADS)
    hsl = [slice(h * HGRN_HEAD, (h + 1) * HGRN_HEAD) for h in heads]

    @pl.when(ti == 0)
    def _():
        st_ref[...] = jnp.zeros(st_ref.shape, F32)

    log_lb, log_1m_lb, one_m_lb = lbc_ref[0:1, :], lbc_ref[1:2, :], lbc_ref[2:3, :]
    tri = (lax.broadcasted_iota(jnp.int32, (c, c), 0) >= lax.broadcasted_iota(jnp.int32, (c, c), 1)).astype(F32)
    sub_s = lax.broadcasted_iota(jnp.int32, (sc, HGRN_WIDTH), 0)

    def chunk(ci, carry):
        r0 = pl.multiple_of(ci * c, c)
        q = q_ref[0, pl.ds(r0, c), :]
        fz = f_ref[0, pl.ds(r0, c), :]
        iv = i_ref[0, pl.ds(r0, c), :]
        b_ = log_1m_lb + _log_sigmoid(fz)
        lf = jnp.maximum(log_lb, b_) + jnp.log1p(jnp.exp(-jnp.abs(log_lb - b_)))
        kf = one_m_lb * jax.nn.sigmoid(-fz)
        cl = _mm_hi(tri, lf)
        c_end = cl[c - 1:c]
        q_dec = q * jnp.exp(cl)
        k_end = kf * jnp.exp(c_end - cl)
        w_end = jnp.exp(c_end)
        st = [st_ref[h] for h in heads]
        o = [_mm_nt(q_dec[:, hsl[h]], st[h]) for h in heads]
        upd = [_mm_tn(iv[:, hsl[h]], k_end[:, hsl[h]]) for h in heads]
        for h in heads:
            st_ref[h] = st[h] * w_end[:, hsl[h]] + upd[h]
        pieces = []
        for blk in range(c // sc):
            lo = blk * sc
            cl_r, q_r, k_r, i_r = cl[lo:lo + sc], q[lo:lo + sc], kf[lo:lo + sc], iv[lo:lo + sc]
            acc = [o[h][lo:lo + sc] for h in heads]
            if blk > 0:
                bnd = cl[lo - 1:lo]
                q_b = q_r * jnp.exp(cl_r - bnd)
                k_b = kf[:lo] * jnp.exp(bnd - cl[:lo])
                att = [_mm_nt(q_b[:, hsl[h]], k_b[:, hsl[h]]) for h in heads]
                acc = [acc[h] + _mm(att[h], iv[:lo, hsl[h]]) for h in heads]
            rows = []
            for t in range(sc):
                w = jnp.where(sub_s <= t, jnp.exp(cl_r[t:t + 1] - cl_r), 0.0)
                e = q_r[t:t + 1] * w * k_r
                d = jnp.concatenate([jnp.broadcast_to(jnp.sum(e[:, hsl[h]], axis=1, keepdims=True), (sc, HGRN_HEAD))
                                     for h in heads], axis=1)
                rows.append(jnp.sum(d * i_r, axis=0, keepdims=True))
            pieces.append(jnp.concatenate(acc, axis=1) + jnp.concatenate(rows, axis=0))
        o_ref[pl.ds(r0, c), :] = jnp.concatenate(pieces, axis=0)
        return carry

    lax.fori_loop(0, o_ref.shape[0] // c, chunk, 0)
    z = z_ref[0]
    gate = lnw_ref[...] * (z * jax.nn.sigmoid(z))
    for h in heads:
        o = o_ref[:, hsl[h]]
        y_ref[0, :, hsl[h]] = o * lax.rsqrt(jnp.mean(o * o, axis=-1, keepdims=True) + RMS_EPS) * gate[:, hsl[h]]

    @pl.when(ti == n_t - 1)
    def _():
        for h in heads:
            s_ref[0, h] = st_ref[h].T


def _hgrn_prompt(proj_a3, lb, ln_w):
    b, t = proj_a3.shape[:2]
    tt = min(MIX_TILE, t)
    n_t = t // tt
    lbc = jnp.concatenate([jnp.log(lb)[None], jnp.log1p(-lb)[None], (1.0 - lb)[None],
                           jnp.zeros((5, HGRN_WIDTH), F32)], axis=0)
    col = lambda off: pl.BlockSpec((1, tt, HGRN_WIDTH), lambda bi, i: (bi, i, off // HGRN_WIDTH))
    return pl.pallas_call(
        functools.partial(_hgrn_body, n_t),
        grid=(b, n_t),
        in_specs=[col(A_HQ), col(A_HF), col(A_HI), col(A_HZ),
                  pl.BlockSpec((8, HGRN_WIDTH), lambda bi, i: (0, 0)),
                  pl.BlockSpec((1, HGRN_WIDTH), lambda bi, i: (0, 0))],
        out_specs=[pl.BlockSpec((1, tt, HGRN_WIDTH), lambda bi, i: (bi, i, 0)),
                   pl.BlockSpec((1, HGRN_HEADS, HGRN_HEAD, HGRN_HEAD), lambda bi, i: (bi, 0, 0, 0))],
        out_shape=[jax.ShapeDtypeStruct((b, t, HGRN_WIDTH), F32),
                   jax.ShapeDtypeStruct((b, HGRN_HEADS, HGRN_HEAD, HGRN_HEAD), F32)],
        scratch_shapes=[pltpu.VMEM((HGRN_HEADS, HGRN_HEAD, HGRN_HEAD), F32), pltpu.VMEM((tt, HGRN_WIDTH), F32)],
        compiler_params=_cparams(("arbitrary", "arbitrary")),
        name="hgrn_prompt",
    )(proj_a3, proj_a3, proj_a3, proj_a3, lbc, ln_w.reshape(1, HGRN_WIDTH))


def _rwkv_body(n_t, cols_ref, z_ref, mu_ref, w0_ref, wup_ref, a0_ref, aup_ref, kk_ref, ka_ref, rk_ref, lnw_ref, bd_ref,
               y_ref, s_ref, prev_ref, st_ref, r_s, k_s, v_s, lw_s, kk_s, a_s, o_s):
    ti = pl.program_id(1)
    tt = cols_ref.shape[1]
    c = RWKV_CHUNK
    n_pair = RWKV_HEADS // 2

    @pl.when(ti == 0)
    def _():
        prev_ref[...] = jnp.zeros(prev_ref.shape, F32)
        st_ref[...] = jnp.zeros(st_ref.shape, F32)

    x = cols_ref[0]
    first = lax.broadcasted_iota(jnp.int32, (tt, 1), 0) == 0
    x_prev = jnp.where(first, prev_ref[0:1, :], pltpu.roll(x, 1, 0))
    prev_ref[0:1, :] = x[tt - 1:tt, :]
    xx = x + (x_prev - x) * mu_ref[...]
    r, k, v = xx[:, 0:RWKV_WIDTH], xx[:, RWKV_WIDTH:2 * RWKV_WIDTH], xx[:, 2 * RWKV_WIDTH:3 * RWKV_WIDTH]
    lora = xx[:, 3 * RWKV_WIDTH:]
    nx = -(w0_ref[...] + _mm_hi(jnp.tanh(lora), wup_ref[...]))
    w_log = -(jnp.maximum(nx, 0.0) + jnp.log1p(jnp.exp(-jnp.abs(nx)))) - 0.5
    lw_s[...] = -jnp.exp(w_log)
    a = jax.nn.sigmoid(a0_ref[...] + _mm_hi(lora, aup_ref[...]))
    bd = bd_ref[...]
    kk_raw = k * kk_ref[...]
    kk_s[...] = kk_raw / jnp.maximum(jnp.sqrt(_mm_ones(kk_raw * kk_raw, bd)), 1e-12)
    k2 = k * (1.0 + (a - 1.0) * ka_ref[...])
    r_s[...] = r
    k_s[...] = k2
    v_s[...] = v
    a_s[...] = a
    bonus = _mm_ones(r * k2 * rk_ref[...], bd) * v

    rows = lax.broadcasted_iota(jnp.int32, (c, LANE), 0)
    lane_in = lax.broadcasted_iota(jnp.int32, (c, LANE), 1) % RWKV_HEAD
    strict = rows > lane_in
    incl = rows >= lane_in
    eye2 = (rows == lane_in).astype(F32)
    tri = (lax.broadcasted_iota(jnp.int32, (c, c), 0) >= lax.broadcasted_iota(jnp.int32, (c, c), 1)).astype(F32)
    lane1 = lax.broadcasted_iota(jnp.int32, (1, LANE), 1)
    m0, m1 = lane1 < RWKV_HEAD, lane1 >= RWKV_HEAD
    bdm = (lax.broadcasted_iota(jnp.int32, (LANE, LANE), 0) // RWKV_HEAD
           == lax.broadcasted_iota(jnp.int32, (LANE, LANE), 1) // RWKV_HEAD)

    def blockdiag(zz):
        zz = zz.astype(BF16)
        return jnp.concatenate([jnp.where(m0, zz, 0), jnp.where(m1, zz, 0)], axis=0)

    def chunk(ci, carry):
        r0 = pl.multiple_of(ci * c, c)
        ds = pl.ds(r0, c)
        lw = lw_s[ds, :]
        cum = _mm_hi(tri, lw)
        w_in, w_ex, inv_in = jnp.exp(cum), jnp.exp(cum - lw), jnp.exp(-cum)
        w_end = jnp.exp(cum[c - 1:c])
        kk_c, vv = kk_s[ds, :], v_s[ds, :]
        alpha = -kk_c * w_ex
        beta_h = kk_c * a_s[ds, :] * inv_in
        k_h = k_s[ds, :] * inv_in
        r_t = r_s[ds, :] * w_in
        beta_e, k_e = beta_h * w_end, k_h * w_end
        pairs = range(n_pair)
        sls = [slice(p * LANE, (p + 1) * LANE) for p in pairs]
        al = [alpha[:, sl] for sl in sls]
        rt = [r_t[:, sl] for sl in sls]
        vb = [blockdiag(vv[:, sl]) for sl in sls]
        aa = [_mm_nt(jnp.concatenate([al[p], rt[p]], axis=0),
                     jnp.concatenate([blockdiag(beta_h[:, sls[p]]), blockdiag(k_h[:, sls[p]])], axis=0))
              for p in pairs]
        a_ab = [jnp.where(strict, aa[p][0:c, 0:LANE], 0.0) for p in pairs]
        a_ak = [jnp.where(strict, aa[p][0:c, LANE:], 0.0) for p in pairs]
        a_rb = [jnp.where(incl, aa[p][c:, 0:LANE], 0.0) for p in pairs]
        a_rk = [jnp.where(incl, aa[p][c:, LANE:], 0.0) for p in pairs]
        akv = [_mm(a_ak[p], vb[p]) for p in pairs]
        rkv = [_mm(a_rk[p], vb[p]) for p in pairs]
        vk = [_mm_tn(vv[:, sls[p]], k_e[:, sls[p]]) for p in pairs]
        tm = [eye2 + a_ab[p] for p in pairs]
        pw = a_ab
        n = 1
        while 2 * n < c:
            pw = [_mm(pw[p], blockdiag(pw[p])) for p in pairs]
            tm = [tm[p] + _mm(tm[p], blockdiag(pw[p])) for p in pairs]
            n *= 2
        pq = [_mm(tm[p], jnp.concatenate([blockdiag(al[p]), blockdiag(akv[p])], axis=1)) for p in pairs]
        ro = [_mm(a_rb[p], jnp.concatenate([blockdiag(pq[p][:, 0:LANE]), blockdiag(pq[p][:, LANE:])], axis=1))
              for p in pairs]
        mn = [_mm_tn(pq[p], beta_e[:, sls[p]]) for p in pairs]
        outs = []
        for p in pairs:
            m_bd = jnp.where(bdm, mn[p][0:LANE], 0.0)
            n_f = jnp.where(bdm, mn[p][LANE:] + vk[p], 0.0)
            st = st_ref[p]
            outs.append(_mm_nt(rt[p] + ro[p][:, 0:LANE], blockdiag(st)) + ro[p][:, LANE:] + rkv[p])
            st_ref[p] = st * w_end[:, sls[p]] + _mm(st, m_bd) + n_f[0:RWKV_HEAD] + n_f[RWKV_HEAD:]
        o_s[ds, :] = jnp.concatenate(outs, axis=1)
        return carry

    lax.fori_loop(0, tt // c, chunk, 0)
    o = o_s[...]
    inv_n = 1.0 / RWKV_HEAD
    d = o - _mm_ones(o, bd) * inv_n
    var = _mm_ones(d * d, bd) * inv_n
    z = z_ref[0]
    y_ref[0] = (d * lax.rsqrt(var + GN_EPS) * lnw_ref[...] + bonus) * (z * jax.nn.sigmoid(z))

    @pl.when(ti == n_t - 1)
    def _():
        s_ref[0] = st_ref[...]


def _rwkv_prompt(rcols, proj_a3, mu, w0, w_up, a0, a_up, k_k, k_a, r_k, ln_w):
    b, t = rcols.shape[:2]
    tt = min(MIX_TILE, t)
    n_t = t // tt
    n_pair = RWKV_HEADS // 2
    zpad = jnp.zeros((RWKV_LORA_W, RWKV_WIDTH), F32)
    wup = jnp.concatenate([w_up, zpad], axis=0)
    aup = jnp.concatenate([zpad, a_up], axis=0)
    hid = np.arange(RWKV_WIDTH) // RWKV_HEAD
    bd = jnp.asarray((hid[:, None] == hid[None, :]).astype(np.float32)).astype(BF16)
    vec = lambda a: a.reshape(1, -1)
    full = lambda shape: pl.BlockSpec(shape, lambda bi, i: (0,) * len(shape))
    y, s = pl.pallas_call(
        functools.partial(_rwkv_body, n_t),
        grid=(b, n_t),
        in_specs=[pl.BlockSpec((1, tt, SHIFT_W), lambda bi, i: (bi, i, 0)),
                  pl.BlockSpec((1, tt, RWKV_WIDTH), lambda bi, i: (bi, i, A_RZ // RWKV_WIDTH)),
                  full((1, SHIFT_W)), full((1, RWKV_WIDTH)), full((LANE, RWKV_WIDTH)), full((1, RWKV_WIDTH)),
                  full((LANE, RWKV_WIDTH)), full((1, RWKV_WIDTH)), full((1, RWKV_WIDTH)), full((1, RWKV_WIDTH)),
                  full((1, RWKV_WIDTH)), full((RWKV_WIDTH, RWKV_WIDTH))],
        out_specs=[pl.BlockSpec((1, tt, RWKV_WIDTH), lambda bi, i: (bi, i, 0)),
                   pl.BlockSpec((1, n_pair, RWKV_HEAD, LANE), lambda bi, i: (bi, 0, 0, 0))],
        out_shape=[jax.ShapeDtypeStruct((b, t, RWKV_WIDTH), F32),
                   jax.ShapeDtypeStruct((b, n_pair, RWKV_HEAD, LANE), F32)],
        scratch_shapes=[pltpu.VMEM((8, SHIFT_W), F32), pltpu.VMEM((n_pair, RWKV_HEAD, LANE), F32)]
                       + [pltpu.VMEM((tt, RWKV_WIDTH), F32)] * 7,
        compiler_params=_cparams(("arbitrary", "arbitrary")),
        name="rwkv_prompt",
    )(rcols, proj_a3, vec(mu), vec(w0), wup, vec(a0), aup, vec(k_k), vec(k_a), vec(r_k), vec(ln_w), bd)
    s = s.reshape(b, n_pair, RWKV_HEAD, 2, RWKV_HEAD).transpose(0, 1, 3, 2, 4)
    return y, s.reshape(b, RWKV_HEADS, RWKV_HEAD, RWKV_HEAD)


def _merge_body(x_ref, yn_ref, nz_ref, yr_ref, yh_ref, mg_ref, wb_ref, wo_ref, o_ref):
    nz = nz_ref[...]
    branches = (yn_ref[...] * (nz * jax.nn.sigmoid(nz)), yr_ref[...], yh_ref[...])
    acc = jnp.zeros(o_ref.shape, F32)
    for n, y in enumerate(branches):
        t = jnp.dot(y.astype(BF16), wb_ref[n], preferred_element_type=F32)
        acc = acc + jax.nn.sigmoid(mg_ref[:, n * D_MODEL:(n + 1) * D_MODEL]) * t
    o_ref[...] = x_ref[...] + jnp.dot(acc.astype(BF16), wo_ref[...], preferred_element_type=F32)


def _merge(x2d, y_nsa, proj_a, y_rwkv, y_hgrn, mg, wb, wo, tm):
    m = x2d.shape[0]
    row = lambda w: pl.BlockSpec((tm, w), lambda i: (i, 0))
    return pl.pallas_call(
        _merge_body,
        grid=(m // tm,),
        in_specs=[row(D_MODEL), row(BRANCH_WIDTH),
                  pl.BlockSpec((tm, BRANCH_WIDTH), lambda i: (i, A_NZ // BRANCH_WIDTH)),
                  row(BRANCH_WIDTH), row(BRANCH_WIDTH), row(N_BRANCH * D_MODEL),
                  pl.BlockSpec((N_BRANCH, BRANCH_WIDTH, D_MODEL), lambda i: (0, 0, 0)),
                  pl.BlockSpec((D_MODEL, D_MODEL), lambda i: (0, 0))],
        out_specs=row(D_MODEL),
        out_shape=jax.ShapeDtypeStruct((m, D_MODEL), F32),
        compiler_params=_cparams(("arbitrary",)),
        name="merge",
    )(x2d, y_nsa, proj_a, y_rwkv, y_hgrn, mg, wb, wo)


def _split_w_in(w):
    o = np.concatenate([[0], np.cumsum(IN_SIZES)])
    seg = lambda n: w[:, o[n]:o[n + 1]]
    pad = jnp.zeros((w.shape[0], A_Q - A_GATE - IN_SIZES[2]), w.dtype)
    wa = jnp.concatenate([seg(1), seg(2), pad, seg(0), seg(3), seg(5), seg(6), seg(7), seg(8), seg(9)], axis=1)
    return wa.astype(BF16), seg(4).astype(BF16), seg(10).astype(BF16)


def _layer(x, l, past, prm, lb, tabs, tm):
    b, t = x.shape[:2]
    m = b * t
    x2d = x.reshape(m, D_MODEL)
    nw = prm['norm_w'][l].reshape(1, D_MODEL)
    wa, wb_cols, wc = prm['w_split'][l]
    proj_a = _inproj(x2d, nw, wa, tm, 1536)
    rcols = _inproj(x2d, nw, wb_cols, tm, SHIFT_W).reshape(b, t, SHIFT_W)
    mg = _inproj(x2d, nw, wc, tm, 1536)
    seg = lambda off, w: proj_a[:, off:off + w].reshape(b, t, w)
    c_tab, su_tab, sd_tab, cos_t, sin_t = tabs
    if past is None:
        kv_new, win_new, kcsrc, ksb, vst, kwb, vwt = _kvpost(proj_a, (c_tab, su_tab, sd_tab), b, t, min(tm, t), True)
        w_all, bias = _cmp_weights(prm['nsa_cmp_pe'][l], prm['nsa_cmp_w'][l])
        kc, vct = _compress(kcsrc, w_all, bias, b, t)
        gate_t = seg(A_GATE, 3 * NSA_HEADS).transpose(0, 2, 1)
        y_nsa = _nsa_prompt(proj_a, gate_t, cos_t, sin_t, kc, vct, ksb, vst, kwb, vwt, b, t)
        win_state = win_new.reshape(b, t, 2, NSA_KV, HEAD_DIM)[:, -min(WINDOW, t):]
    else:
        cache_kv, page_table, cache_win, s_r, prev, s_h = past
        kv_new, win_new = _kvpost(proj_a, (c_tab, su_tab, sd_tab), b, t, tm, False)
        win5 = win_new.reshape(b, t, 2, NSA_KV, HEAD_DIM)
        win_state = jnp.concatenate([cache_win[l][:, t:], win5], axis=1)
        gates = jax.nn.sigmoid(seg(A_GATE, 3 * NSA_HEADS)).reshape(b, t, NSA_HEADS, 3)
        q = seg(A_Q, NSA_WIDTH).reshape(b, t, NSA_HEADS, HEAD_DIM)
        y_nsa = _nsa_decode_pallas(l, q, gates, cache_kv, page_table, kv_new.reshape(b, t, 512), cache_win,
                                   win_new.reshape(b, t, 256), prm['nsa_cmp_pe'][l], prm['nsa_cmp_w'][l])
    rwkv_prm = (prm['rwkv_mu'][l], prm['rwkv_w0'][l], prm['rwkv_w_up'][l], prm['rwkv_a0'][l], prm['rwkv_a_up'][l],
                prm['rwkv_k_k'][l], prm['rwkv_k_a'][l], prm['rwkv_r_k'][l], prm['rwkv_ln_w'][l])
    if past is None:
        proj_a3 = proj_a.reshape(b, t, A_WIDTH)
        y_rwkv, s_r = _rwkv_prompt(rcols, proj_a3, *rwkv_prm)
        shift_state = rcols[:, -1]
        y_hgrn, s_h = _hgrn_prompt(proj_a3, lb, prm['hgrn_ln_w'][l])
    else:
        y_rwkv, shift_state, s_r = _rwkv_mix(rcols, seg(A_RZ, RWKV_WIDTH), prev, s_r, *rwkv_prm)
        y_hgrn, s_h = _hgrn_mix(seg(A_HQ, HGRN_WIDTH), seg(A_HF, HGRN_WIDTH), seg(A_HI, HGRN_WIDTH),
                                seg(A_HZ, HGRN_WIDTH), s_h, lb, prm['hgrn_ln_w'][l])
    x_new = _merge(x2d, y_nsa.reshape(m, NSA_WIDTH), proj_a, y_rwkv.reshape(m, RWKV_WIDTH), y_hgrn.reshape(m, HGRN_WIDTH),
                   mg, prm['w_branch_bf16'][l], prm['w_out_bf16'][l], min(tm, 512))
    kv_state = kv_new.reshape(b, t, 4, NSA_KV, HEAD_DIM)
    return x_new.reshape(b, t, D_MODEL), (kv_state, win_state, s_r, shift_state, s_h)


def kernel(x_prompt, x_sample, cache_kv, cache_win, state_rwkv, state_shift, state_hgrn, page_table,
           norm_w, w_in, nsa_cmp_pe, nsa_cmp_w, rwkv_mu, rwkv_w0, rwkv_w_up, rwkv_a0, rwkv_a_up,
           rwkv_k_k, rwkv_k_a, rwkv_r_k, rwkv_ln_w, hgrn_lb_logits, hgrn_ln_w, w_branch, w_out, norm_f):
    prm = {'norm_w': norm_w, 'nsa_cmp_pe': nsa_cmp_pe, 'nsa_cmp_w': nsa_cmp_w,
           'rwkv_mu': rwkv_mu, 'rwkv_w0': rwkv_w0, 'rwkv_w_up': rwkv_w_up, 'rwkv_a0': rwkv_a0,
           'rwkv_a_up': rwkv_a_up, 'rwkv_k_k': rwkv_k_k, 'rwkv_k_a': rwkv_k_a, 'rwkv_r_k': rwkv_r_k,
           'rwkv_ln_w': rwkv_ln_w, 'hgrn_ln_w': hgrn_ln_w,
           'w_split': [_split_w_in(w_in[l]) for l in range(DEPTH)],
           'w_branch_bf16': w_branch.astype(BF16), 'w_out_bf16': w_out.astype(BF16)}
    cs = jnp.cumsum(jax.nn.softmax(hgrn_lb_logits, axis=0), axis=0)
    lbs = cs - cs[0:1]
    n_dec, dec_t = x_sample.shape[:2]
    bp, tp = x_prompt.shape[:2]
    tabs_p = _rope_tables(jnp.arange(tp))
    tabs_s = _rope_tables(PAST_LEN + jnp.arange(n_dec * dec_t) % dec_t)
    tm_p = min(1024, bp * tp)
    tm_s = n_dec * dec_t
    xp, xs = x_prompt, x_sample
    outs = [[] for _ in range(10)]
    for l in range(DEPTH):
        xp, st_p = _layer(xp, l, None, prm, lbs[l], tabs_p, tm_p)
        past = (cache_kv, page_table, cache_win, state_rwkv[l], state_shift[l], state_hgrn[l])
        xs, st_s = _layer(xs, l, past, prm, lbs[l], tabs_s, tm_s)
        for n in range(5):
            outs[2 * n].append(st_p[n])
            outs[2 * n + 1].append(st_s[n])
    y_prompt = _final_norm(xp.reshape(bp * tp, D_MODEL), norm_f, tm_p).reshape(xp.shape)
    y_sample = _final_norm(xs.reshape(n_dec * dec_t, D_MODEL), norm_f, tm_s).reshape(xs.shape)
    return (y_prompt, y_sample) + tuple(jnp.stack(o) for o in outs)
```

```python
import functools

import jax
import jax.numpy as jnp
import numpy as np
from jax import lax
from jax.experimental import pallas as pl
from jax.experimental.pallas import tpu as pltpu

F32 = jnp.float32
BF16 = jnp.bfloat16
HI = lax.Precision.HIGHEST

D_MODEL = 1024
DEPTH = 4
PAST_LEN = 2048
PAGE_SIZE = 128
NSA_HEADS = 8
NSA_KV = 2
HEAD_DIM = 64
NSA_HPG = NSA_HEADS // NSA_KV
NSA_WIDTH = NSA_HEADS * HEAD_DIM
ROT_DIM = HEAD_DIM // 4
ROPE_THETA = 500000.0
CMP_BLOCK = 32
CMP_STRIDE = 16
SLC_BLOCK = 64
N_SELECT = 16
WINDOW = 512
Q_BLOCK = 128
RWKV_HEADS = 8
RWKV_HEAD = 64
RWKV_WIDTH = RWKV_HEADS * RWKV_HEAD
RWKV_LORA_W = 64
RWKV_LORA_A = 64
SHIFT_W = 3 * RWKV_WIDTH + RWKV_LORA_W + RWKV_LORA_A
HGRN_HEADS = 4
HGRN_HEAD = 128
HGRN_WIDTH = HGRN_HEADS * HGRN_HEAD
HGRN_CHUNK = 64
N_BRANCH = 3
BRANCH_WIDTH = 512
KV_COLS = 6 * NSA_KV * HEAD_DIM
IN_SIZES = (NSA_WIDTH, KV_COLS, 3 * NSA_HEADS, NSA_WIDTH, SHIFT_W, RWKV_WIDTH,
            HGRN_WIDTH, HGRN_WIDTH, HGRN_WIDTH, HGRN_WIDTH, N_BRANCH * D_MODEL)
RMS_EPS = 1e-6
GN_EPS = 64e-5

LANE = 128
VMEM_LIMIT = 56 * 1024 * 1024
NEG = -1e30
KEY_TILE = 512
LOG2E = 1.4426950408889634
RWKV_CHUNK = 64

A_KV = 0
A_GATE = 768
A_Q = 1024
A_NZ = 1536
A_RZ = 2048
A_HQ = 2560
A_HF = 3072
A_HI = 3584
A_HZ = 4096
A_WIDTH = 4608


def _cparams(sem):
    return pltpu.CompilerParams(dimension_semantics=sem, vmem_limit_bytes=VMEM_LIMIT)


def _inproj_body(x_ref, nw_ref, w_ref, o_ref, h_ref):
    @pl.when(pl.program_id(1) == 0)
    def _():
        x = x_ref[...]
        ms = jnp.mean(x * x, axis=-1, keepdims=True)
        h_ref[...] = (x * lax.rsqrt(ms + RMS_EPS) * nw_ref[...]).astype(BF16)

    o_ref[...] = jnp.dot(h_ref[...], w_ref[...], preferred_element_type=F32)


def _inproj(x2d, nw, w, tm, tn):
    m, d = x2d.shape
    n = w.shape[1]
    return pl.pallas_call(
        _inproj_body,
        grid=(m // tm, n // tn),
        in_specs=[pl.BlockSpec((tm, d), lambda i, j: (i, 0)),
                  pl.BlockSpec((1, d), lambda i, j: (0, 0)),
                  pl.BlockSpec((d, tn), lambda i, j: (0, j))],
        out_specs=pl.BlockSpec((tm, tn), lambda i, j: (i, j)),
        out_shape=jax.ShapeDtypeStruct((m, n), F32),
        scratch_shapes=[pltpu.VMEM((tm, d), BF16)],
        compiler_params=_cparams(("arbitrary", "arbitrary")),
        name="inproj",
    )(x2d, nw, w)


def _rmsnorm_body(x_ref, w_ref, o_ref):
    x = x_ref[...]
    ms = jnp.mean(x * x, axis=-1, keepdims=True)
    o_ref[...] = x * lax.rsqrt(ms + RMS_EPS) * w_ref[...]


def _final_norm(x2d, w, tm):
    m, d = x2d.shape
    return pl.pallas_call(
        _rmsnorm_body,
        grid=(m // tm,),
        in_specs=[pl.BlockSpec((tm, d), lambda i: (i, 0)), pl.BlockSpec((1, d), lambda i: (0, 0))],
        out_specs=pl.BlockSpec((tm, d), lambda i: (i, 0)),
        out_shape=jax.ShapeDtypeStruct((m, d), F32),
        compiler_params=_cparams(("arbitrary",)),
        name="final_norm",
    )(x2d, w.reshape(1, d))


def _rope_tables(pos):
    half = ROT_DIM // 2
    inv = ROPE_THETA ** (-jnp.arange(0, ROT_DIM, 2, dtype=F32) / ROT_DIM)
    ang = pos.astype(F32)[:, None] * inv[None, :]
    cos, sin = jnp.cos(ang), jnp.sin(ang)
    n = pos.shape[0]
    ones = jnp.ones((n, HEAD_DIM - ROT_DIM), F32)
    zeros8 = jnp.zeros((n, half), F32)
    zeros = jnp.zeros((n, HEAD_DIM - ROT_DIM), F32)
    c = jnp.concatenate([cos, cos, ones], axis=1)
    s_up = jnp.concatenate([zeros8, sin, zeros], axis=1)
    s_dn = jnp.concatenate([-sin, zeros8, zeros], axis=1)
    tile = lambda a: jnp.concatenate([a] * NSA_KV, axis=1)
    return tile(c), tile(s_up), tile(s_dn), cos.T, sin.T


def _kvpost_body(with_attn, p_ref, c_ref, su_ref, sd_ref, kv_ref, win_ref, *extra):
    x = p_ref[...]
    c, su, sd = c_ref[...], su_ref[...], sd_ref[...]

    def rope(v):
        return v * c + pltpu.roll(v, ROT_DIM // 2, 1) * su + pltpu.roll(v, LANE - ROT_DIM // 2, 1) * sd

    k_slc = rope(x[:, 256:384])
    k_win = rope(x[:, 512:640])
    kv_ref[:, 0:256] = x[:, 0:256]
    kv_ref[:, 256:384] = k_slc
    kv_ref[:, 384:512] = x[:, 384:512]
    win_ref[:, 0:128] = k_win
    win_ref[:, 128:256] = x[:, 640:768]
    if with_attn:
        kc_ref, ks_ref, vst_ref, kw_ref, vwt_ref = extra
        kc_ref[...] = x[:, 0:256].astype(BF16)
        ks_ref[...] = k_slc.astype(BF16)
        vst_ref[0] = x[:, 384:512].T.astype(BF16)
        kw_ref[...] = k_win.astype(BF16)
        vwt_ref[0] = x[:, 640:768].T.astype(BF16)


def _kvpost(proj_a, tabs, b, t, tm, with_attn):
    m = b * t
    nt = tabs[0].shape[0] // tm
    tab_spec = pl.BlockSpec((tm, LANE), lambda i: (i % nt, 0))
    out_shape = [jax.ShapeDtypeStruct((m, 512), F32), jax.ShapeDtypeStruct((m, 256), F32)]
    out_specs = [pl.BlockSpec((tm, 512), lambda i: (i, 0)), pl.BlockSpec((tm, 256), lambda i: (i, 0))]
    if with_attn:
        out_shape += [jax.ShapeDtypeStruct((m, 256), BF16), jax.ShapeDtypeStruct((m, LANE), BF16),
                      jax.ShapeDtypeStruct((b, LANE, t), BF16), jax.ShapeDtypeStruct((m, LANE), BF16),
                      jax.ShapeDtypeStruct((b, LANE, t), BF16)]
        tspec = pl.BlockSpec((1, LANE, tm), lambda i: (i // nt, 0, i % nt))
        out_specs += [pl.BlockSpec((tm, 256), lambda i: (i, 0)), pl.BlockSpec((tm, LANE), lambda i: (i, 0)),
                      tspec, pl.BlockSpec((tm, LANE), lambda i: (i, 0)), tspec]
    return pl.pallas_call(
        functools.partial(_kvpost_body, with_attn),
        grid=(m // tm,),
        in_specs=[pl.BlockSpec((tm, KV_COLS), lambda i: (i, 0)), tab_spec, tab_spec, tab_spec],
        out_specs=out_specs,
        out_shape=out_shape,
        compiler_params=_cparams(("arbitrary",)),
        name="kvpost",
    )(proj_a, *tabs)


def _cmp_weights(pe, cw):
    eye = jnp.eye(NSA_KV, dtype=F32)
    cols = []
    for s in range(2):
        for part in range(2):
            w = cw[s, part * CMP_STRIDE:(part + 1) * CMP_STRIDE]
            blk = jnp.einsum('jde,gh->jgdhe', w, eye).reshape(CMP_STRIDE, LANE, LANE)
            full = jnp.zeros((CMP_STRIDE, 2, LANE, LANE), F32).at[:, s].set(blk)
            cols.append(full.reshape(CMP_STRIDE * 2 * LANE, LANE))
    w_all = jnp.concatenate(cols, axis=1).astype(BF16)
    bias = jnp.einsum('sjd,sjde->se', pe, cw)
    bias = jnp.concatenate([bias[0], bias[0], bias[1], bias[1]]).reshape(1, 2 * LANE)
    return w_all, bias


def _cmp_body(sub_ref, w_ref, b_ref, kc_ref, vct_ref):
    r = jnp.dot(sub_ref[0], w_ref[...], preferred_element_type=F32)
    n = r.shape[0]
    b = b_ref[...]
    kc = r[:, 0:128] + pltpu.roll(r[:, 128:256], n - 1, 0) + b[:, 0:128]
    vc = r[:, 256:384] + pltpu.roll(r[:, 384:512], n - 1, 0) + b[:, 128:256]
    kc_ref[0] = kc.astype(BF16)
    vct_ref[0] = vc.T.astype(BF16)


def _compress(kcsrc, w_all, bias, b, t):
    n_sub = t // CMP_STRIDE
    sub = kcsrc.reshape(b, n_sub, CMP_STRIDE * 256)
    return pl.pallas_call(
        _cmp_body,
        grid=(b,),
        in_specs=[pl.BlockSpec((1, n_sub, CMP_STRIDE * 256), lambda i: (i, 0, 0)),
                  pl.BlockSpec((CMP_STRIDE * 256, 512), lambda i: (0, 0)),
                  pl.BlockSpec((1, 256), lambda i: (0, 0))],
        out_specs=[pl.BlockSpec((1, n_sub, LANE), lambda i: (i, 0, 0)),
                   pl.BlockSpec((1, LANE, n_sub), lambda i: (i, 0, 0))],
        out_shape=[jax.ShapeDtypeStruct((b, n_sub, LANE), BF16), jax.ShapeDtypeStruct((b, LANE, n_sub), BF16)],
        compiler_params=_cparams(("arbitrary",)),
        name="compress",
    )(sub, w_all, bias)


def _slc_cmp_matrix(n_slc, n_cmp, n_cmp_pad):
    ratio = SLC_BLOCK // CMP_STRIDE
    span = CMP_BLOCK // CMP_STRIDE
    m = np.zeros((n_slc, n_cmp_pad), np.float32)
    for j in range(n_slc):
        for k in range(ratio + span - 1):
            n = ratio * j - (span - 1) + k
            if 0 <= n < n_cmp:
                m[j, n] = 1.0
    return m


def _nsa_body(n_cmp_pad, n_slc, q_ref, gt_ref, cos_ref, sin_ref, kc_ref, vct_ref, ks_ref, vst_ref,
              kw_ref, vwt_ref, mt_ref, o_ref, sc_ref, sel_ref, ml_ref, acc_ref, s_ref, p_ref):
    i = pl.program_id(1)
    q0 = i * Q_BLOCK
    hq = NSA_HPG * Q_BLOCK
    q = q_ref[0]
    cos4 = jnp.concatenate([cos_ref[...]] * NSA_HPG, axis=1)
    sin4 = jnp.concatenate([sin_ref[...]] * NSA_HPG, axis=1)
    gs = jax.nn.sigmoid(gt_ref[0])
    lane = lax.broadcasted_iota(jnp.int32, (Q_BLOCK, Q_BLOCK), 1)
    sub = lax.broadcasted_iota(jnp.int32, (Q_BLOCK, Q_BLOCK), 0)
    sub_k = lax.broadcasted_iota(jnp.int32, (KEY_TILE, Q_BLOCK), 0)
    qp_k = q0 + lax.broadcasted_iota(jnp.int32, (KEY_TILE, Q_BLOCK), 1)
    bpt = KEY_TILE // SLC_BLOCK
    zeros_g = jnp.zeros((HEAD_DIM, hq), F32)
    half = ROT_DIM // 2

    for g in range(NSA_KV):
        xa = q[:, g * 256:g * 256 + 128].T
        xb = q[:, g * 256 + 128:g * 256 + 256].T
        qn = jnp.concatenate([xa[0:64], xa[64:128], xb[0:64], xb[64:128]], axis=1) * (HEAD_DIM ** -0.5 * LOG2E)
        x1, x2 = qn[0:half], qn[half:ROT_DIM]
        qr = jnp.concatenate([x1 * cos4 - x2 * sin4, x2 * cos4 + x1 * sin4, qn[ROT_DIM:]], axis=0)

        def pad(a):
            parts = [zeros_g] * NSA_KV
            parts[g] = a
            return jnp.concatenate(parts, axis=0).astype(BF16)

        qn_p, qr_p = pad(qn), pad(qr)

        def compressed(n_rows):
            s1 = jnp.dot(kc_ref[0, 0:n_rows, :], qn_p, preferred_element_type=F32)
            n_idx = lax.broadcasted_iota(jnp.int32, (n_rows, Q_BLOCK), 0)
            lane_c = lax.broadcasted_iota(jnp.int32, (n_rows, Q_BLOCK), 1)
            cmp_ok = (CMP_STRIDE * n_idx + CMP_BLOCK - 1) <= (q0 + lane_c)
            ps = []
            for h in range(NSA_HPG):
                s = jnp.where(cmp_ok, s1[:, h * Q_BLOCK:(h + 1) * Q_BLOCK], NEG)
                m = jnp.max(s, axis=0, keepdims=True)
                e = jnp.where(cmp_ok, jnp.exp2(s - m), 0.0)
                d = jnp.maximum(jnp.sum(e, axis=0, keepdims=True), 1e-30)
                ps.append(e / d)
            imp = ps[0] + ps[1] + ps[2] + ps[3]
            p1 = jnp.concatenate(ps, axis=1).astype(BF16)
            o_c = jnp.dot(vct_ref[0, g * HEAD_DIM:(g + 1) * HEAD_DIM, 0:n_rows], p1, preferred_element_type=F32)
            return o_c, _mm_ones_rhs(mt_ref[:, 0:n_rows], imp)

        prefixes = sorted({n_cmp_pad * k // 4 for k in (1, 2, 3, 4) if (n_cmp_pad * k // 4) % LANE == 0})

        def pick(cands):
            if len(cands) == 1:
                return compressed(cands[0])
            return lax.cond(CMP_STRIDE * cands[0] + CMP_BLOCK - 1 > q0 + Q_BLOCK - 1,
                            lambda: compressed(cands[0]), lambda: pick(cands[1:]))

        o_cmp, imps = pick(prefixes)

        blk = lax.broadcasted_iota(jnp.int32, (n_slc, Q_BLOCK), 0)
        qp_s = q0 + lax.broadcasted_iota(jnp.int32, (n_slc, Q_BLOCK), 1)
        causal = blk * SLC_BLOCK <= qp_s
        qblk = qp_s // SLC_BLOCK
        forced = (blk == 0) | (blk == qblk) | (blk == qblk - 1)
        score = jnp.where(causal, jnp.where(forced, jnp.inf, imps), -jnp.inf)
        sc_ref[...] = score

        def rank_step(jh, cnt):
            for j in (2 * jh, 2 * jh + 1):
                row = sc_ref[pl.ds(j, 1), :]
                ge = jnp.where(row >= score, 1.0, 0.0)
                gt = jnp.where(row > score, 1.0, 0.0)
                cnt = cnt + jnp.where(blk > j, ge, gt)
            return cnt

        cnt = lax.fori_loop(0, jnp.minimum(i + 1, n_slc // 2), rank_step, jnp.zeros((n_slc, Q_BLOCK), F32))
        sel_ref[...] = jnp.where(cnt < N_SELECT, jnp.where(score > -jnp.inf, 0.0, NEG), NEG)

        init = (jnp.full((1, hq), NEG, F32), jnp.zeros((1, hq), F32), jnp.zeros((HEAD_DIM, hq), F32))

        def scores(k_ref, row0, n_rows):
            return jnp.dot(k_ref[0, pl.ds(pl.multiple_of(row0, KEY_TILE), n_rows), :], qr_p,
                           preferred_element_type=F32)

        def softmax_tile(s, bias, m, l):
            ps, alphas, ms, ls = [], [], [], []
            for h in range(NSA_HPG):
                hs = slice(h * Q_BLOCK, (h + 1) * Q_BLOCK)
                sh = s[:, hs] + bias
                m_new = jnp.maximum(m[:, hs], jnp.max(sh, axis=0, keepdims=True))
                alpha = jnp.exp2(m[:, hs] - m_new)
                p = jnp.exp2(sh - m_new)
                ms.append(m_new)
                ls.append(alpha * l[:, hs] + jnp.sum(p, axis=0, keepdims=True))
                ps.append(p.astype(BF16))
                alphas.append(alpha)
            cat = lambda parts: jnp.concatenate(parts, axis=1)
            return cat(ms), cat(ls), cat(alphas), cat(ps)

        def values_t(vt_ref, t):
            return vt_ref[0, g * HEAD_DIM:(g + 1) * HEAD_DIM, pl.ds(pl.multiple_of(t * KEY_TILE, KEY_TILE), KEY_TILE)]

        n_pairs = (q0 + Q_BLOCK + 2 * KEY_TILE - 1) // (2 * KEY_TILE)
        last = 2 * n_pairs - 1

        def slc_tile(u, slot):
            s_next = scores(ks_ref, jnp.minimum(u + 1, last) * KEY_TILE, KEY_TILE)
            pv = jnp.dot(values_t(vst_ref, jnp.maximum(u - 1, 0)), p_ref[1 - slot], preferred_element_type=F32)
            acc_ref[...] = acc_ref[...] * ml_ref[2:3, :] + jnp.where(u > 0, pv, 0.0)
            rows = [jnp.broadcast_to(sel_ref[pl.ds(bpt * u + jj, 1), :], (SLC_BLOCK, Q_BLOCK)) for jj in range(bpt)]
            bias = jnp.where(u * KEY_TILE + sub_k <= qp_k, jnp.concatenate(rows, axis=0), NEG)
            m, l, alpha, p = softmax_tile(s_ref[slot], bias, ml_ref[0:1, :], ml_ref[1:2, :])
            ml_ref[0:1, :], ml_ref[1:2, :], ml_ref[2:3, :] = m, l, alpha
            p_ref[slot] = p
            s_ref[1 - slot] = s_next

        def slc_step(k, carry):
            slc_tile(2 * k, 0)
            slc_tile(2 * k + 1, 1)
            return carry

        ml_ref[0:1, :], ml_ref[1:2, :], acc_ref[...] = init
        ml_ref[2:3, :] = jnp.ones((1, hq), F32)
        s_ref[0] = scores(ks_ref, 0, KEY_TILE)
        lax.fori_loop(0, n_pairs, slc_step, 0)
        pv = jnp.dot(values_t(vst_ref, last), p_ref[1], preferred_element_type=F32)
        o_slc = (acc_ref[...] * ml_ref[2:3, :] + pv) / ml_ref[1:2, :]

        w_rows = WINDOW + Q_BLOCK
        w0 = pl.multiple_of(jnp.maximum(q0 - WINDOW, 0), Q_BLOCK)
        kp = w0 + lax.broadcasted_iota(jnp.int32, (w_rows, Q_BLOCK), 0)
        qp_w = q0 + lax.broadcasted_iota(jnp.int32, (w_rows, Q_BLOCK), 1)
        bias_w = jnp.where(jnp.where(kp <= qp_w, qp_w - kp, WINDOW) < WINDOW, 0.0, NEG)
        s_w = jnp.dot(kw_ref[0, pl.ds(w0, w_rows), :], qr_p, preferred_element_type=F32)
        _, l_w, _, p_w = softmax_tile(s_w, bias_w, init[0], init[1])
        o_win = jnp.dot(vwt_ref[0, g * HEAD_DIM:(g + 1) * HEAD_DIM, pl.ds(w0, w_rows)], p_w,
                        preferred_element_type=F32) / l_w

        def gate(jj):
            return jnp.concatenate([gs[(g * NSA_HPG + h) * 3 + jj:(g * NSA_HPG + h) * 3 + jj + 1, :]
                                    for h in range(NSA_HPG)], axis=1)

        o_t = gate(0) * o_cmp + gate(1) * o_slc + gate(2) * o_win
        ya = jnp.concatenate([o_t[:, 0:128], o_t[:, 128:256]], axis=0).T
        yb = jnp.concatenate([o_t[:, 256:384], o_t[:, 384:512]], axis=0).T
        o_ref[0, :, g * 256:g * 256 + 128] = ya
        o_ref[0, :, g * 256 + 128:g * 256 + 256] = yb


def _nsa_prompt(proj_a, gate_t, cos_t, sin_t, kc, vct, ksb, vst, kwb, vwt, b, t):
    nb = t // Q_BLOCK
    n_sub = t // CMP_STRIDE
    n_slc = t // SLC_BLOCK
    mt = jnp.asarray(_slc_cmp_matrix(n_slc, n_sub - 1, n_sub)).astype(BF16)
    seq = lambda w: pl.BlockSpec((1, t, w), lambda bi, i: (bi, 0, 0))
    seq_t = pl.BlockSpec((1, LANE, t), lambda bi, i: (bi, 0, 0))
    return pl.pallas_call(
        functools.partial(_nsa_body, n_sub, n_slc),
        grid=(b, nb),
        in_specs=[pl.BlockSpec((1, Q_BLOCK, NSA_WIDTH), lambda bi, i: (bi, i, A_Q // NSA_WIDTH)),
                  pl.BlockSpec((1, 3 * NSA_HEADS, Q_BLOCK), lambda bi, i: (bi, 0, i)),
                  pl.BlockSpec((ROT_DIM // 2, Q_BLOCK), lambda bi, i: (0, i)),
                  pl.BlockSpec((ROT_DIM // 2, Q_BLOCK), lambda bi, i: (0, i)),
                  pl.BlockSpec((1, n_sub, LANE), lambda bi, i: (bi, 0, 0)),
                  pl.BlockSpec((1, LANE, n_sub), lambda bi, i: (bi, 0, 0)),
                  seq(LANE), seq_t, seq(LANE), seq_t,
                  pl.BlockSpec((n_slc, n_sub), lambda bi, i: (0, 0))],
        out_specs=pl.BlockSpec((1, Q_BLOCK, NSA_WIDTH), lambda bi, i: (bi, i, 0)),
        out_shape=jax.ShapeDtypeStruct((b, t, NSA_WIDTH), F32),
        scratch_shapes=[pltpu.VMEM((n_slc, Q_BLOCK), F32), pltpu.VMEM((n_slc, Q_BLOCK), F32),
                        pltpu.VMEM((8, NSA_HPG * Q_BLOCK), F32), pltpu.VMEM((HEAD_DIM, NSA_HPG * Q_BLOCK), F32),
                        pltpu.VMEM((2, KEY_TILE, NSA_HPG * Q_BLOCK), F32),
                        pltpu.VMEM((2, KEY_TILE, NSA_HPG * Q_BLOCK), BF16)],
        compiler_params=_cparams(("arbitrary", "arbitrary")),
        name="nsa_prompt",
    )(proj_a.reshape(b, t, A_WIDTH), gate_t, cos_t, sin_t, kc, vct,
      ksb.reshape(b, t, LANE), vst, kwb.reshape(b, t, LANE), vwt, mt)


def _masked_softmax(s, mask):
    s = jnp.where(mask, s, -jnp.inf)
    m = jnp.max(s, axis=-1, keepdims=True)
    m = jnp.where(jnp.isfinite(m), m, 0.0)
    e = jnp.exp(s - m)
    return e / jnp.maximum(jnp.sum(e, axis=-1, keepdims=True), 1e-30)


def _rope_rows(x, pos):
    half = ROT_DIM // 2
    inv = ROPE_THETA ** (-jnp.arange(0, ROT_DIM, 2, dtype=F32) / ROT_DIM)
    ang = pos.astype(F32)[:, None] * inv[None, :]
    cos = jnp.cos(ang)[None, :, None, :]
    sin = jnp.sin(ang)[None, :, None, :]
    x1, x2 = x[..., :half], x[..., half:ROT_DIM]
    return jnp.concatenate([x1 * cos - x2 * sin, x2 * cos + x1 * sin, x[..., ROT_DIM:]], axis=-1)


def _nsa_decode(q, gates, kv_all, win_all, pe, cw):
    b, tq = q.shape[:2]
    tk = kv_all.shape[1]
    scale = HEAD_DIM ** -0.5
    qpos = PAST_LEN + jnp.arange(tq)
    qn = q.reshape(b, tq, NSA_KV, NSA_HPG, HEAD_DIM)
    qr = _rope_rows(q, qpos).reshape(b, tq, NSA_KV, NSA_HPG, HEAD_DIM)
    n_sub = tk // CMP_STRIDE
    n_cmp = n_sub - 1

    def compress(rows, s):
        sub = rows[:, :n_sub * CMP_STRIDE].reshape(b, n_sub, CMP_STRIDE, NSA_KV, HEAD_DIM)
        a = jnp.einsum('bnjgd,jde->bnge', sub, cw[s, :CMP_STRIDE])
        c = jnp.einsum('bnjgd,jde->bnge', sub, cw[s, CMP_STRIDE:])
        bias = jnp.einsum('jd,jde->e', pe[s], cw[s])
        return a[:, :n_cmp] + c[:, 1:] + bias

    kc = compress(kv_all[:, :, 0], 0)
    vc = compress(kv_all[:, :, 1], 1)
    cmp_end = CMP_STRIDE * jnp.arange(n_cmp) + CMP_BLOCK - 1
    s1 = jnp.einsum('bqghd,bngd->bqghn', qn, kc) * scale
    p1 = _masked_softmax(s1, (cmp_end[None, :] <= qpos[:, None])[None, :, None, None, :])
    o_cmp = jnp.einsum('bqghn,bngd->bqghd', p1, vc)
    n_slc = -(-tk // SLC_BLOCK)
    mt = jnp.asarray(_slc_cmp_matrix(n_slc, n_cmp, n_cmp))
    imp = jnp.einsum('bqgn,jn->bqgj', jnp.sum(p1, axis=3), mt, precision=HI)
    blk = jnp.arange(n_slc)
    qblk = qpos // SLC_BLOCK
    causal = (blk[None, :] * SLC_BLOCK <= qpos[:, None])[None, :, None, :]
    forced = ((blk[None, :] == 0) | (blk[None, :] == qblk[:, None]) | (blk[None, :] == qblk[:, None] - 1))[None, :, None, :]
    score = jnp.where(causal, jnp.where(forced, jnp.inf, imp), -jnp.inf)
    ahead = (score[..., None, :] > score[..., :, None]) | (
        (score[..., None, :] == score[..., :, None]) & (blk[None, :] < blk[:, None]))
    rank = jnp.sum(ahead, axis=-1)
    sel = (rank < N_SELECT) & (score > -jnp.inf)
    kpos = jnp.arange(tk)
    key_ok = sel[..., kpos // SLC_BLOCK] & (kpos[None, :] <= qpos[:, None])[None, :, None, :]
    s2 = jnp.einsum('bqghd,bkgd->bqghk', qr, kv_all[:, :, 2]) * scale
    p2 = _masked_softmax(s2, key_ok[:, :, :, None, :])
    o_slc = jnp.einsum('bqghk,bkgd->bqghd', p2, kv_all[:, :, 3])
    wpos = PAST_LEN - WINDOW + jnp.arange(win_all.shape[1])
    m3 = (wpos[None, :] <= qpos[:, None]) & (qpos[:, None] - wpos[None, :] < WINDOW)
    s3 = jnp.einsum('bqghd,bkgd->bqghk', qr, win_all[:, :, 0]) * scale
    p3 = _masked_softmax(s3, m3[None, :, None, None, :])
    o_win = jnp.einsum('bqghk,bkgd->bqghd', p3, win_all[:, :, 1])
    g = gates.reshape(b, tq, NSA_KV, NSA_HPG, 3)
    o = g[..., 0:1] * o_cmp + g[..., 1:2] * o_slc + g[..., 2:3] * o_win
    return o.reshape(b, tq, NSA_WIDTH)


N_PAGES = PAST_LEN // PAGE_SIZE
DEC_ROWS = 8
DEC_QROWS = NSA_HEADS * 4


DEC_SEQS = 2


def _nsa_dec_body(dec_t, pt_ref, *refs):
    del pt_ref
    for sq in range(DEC_SEQS):
        _nsa_dec_one(dec_t, sq, refs[sq * N_PAGES:(sq + 1) * N_PAGES], *refs[DEC_SEQS * N_PAGES:])


def _nsa_dec_one(dec_t, sq, pages, kvn_ref, cw_ref, wn_ref, qn_ref, qr_ref, g_ref, w_ref, b_ref, mt_ref, o_ref,
                 ck_ref, cv_ref):
    qn, qr = qn_ref[sq], qr_ref[sq]
    nq = qn.shape[0]
    lane = lax.broadcasted_iota(jnp.int32, (nq, LANE), 1)
    qi = lax.broadcasted_iota(jnp.int32, (nq, LANE), 0) % dec_t
    qpos = PAST_LEN + qi
    n_sub = PAST_LEN // CMP_STRIDE

    for p in range(N_PAGES):
        ck_ref[sq, p * PAGE_SIZE:(p + 1) * PAGE_SIZE, :] = pages[p][0, 0, :, 0:128]
        cv_ref[sq, p * PAGE_SIZE:(p + 1) * PAGE_SIZE, :] = pages[p][0, 0, :, 128:256]
    parts = []
    for j in range(CMP_STRIDE):
        parts += [ck_ref[sq, pl.ds(j, n_sub, stride=CMP_STRIDE), :], cv_ref[sq, pl.ds(j, n_sub, stride=CMP_STRIDE), :]]
    r = _mm(jnp.concatenate(parts, axis=1), w_ref[...])
    bias = b_ref[...]
    kc = r[:, 0:128] + pltpu.roll(r[:, 128:256], n_sub - 1, 0) + bias[:, 0:128]
    vc = r[:, 256:384] + pltpu.roll(r[:, 384:512], n_sub - 1, 0) + bias[:, 128:256]

    def softmax_rows(scores, oks):
        scores = [jnp.where(ok, s, NEG) for s, ok in zip(scores, oks)]
        m = functools.reduce(jnp.maximum, [jnp.max(s, axis=1, keepdims=True) for s in scores])
        es = [jnp.where(ok, jnp.exp(s - m), 0.0) for s, ok in zip(scores, oks)]
        d = jnp.maximum(functools.reduce(jnp.add, [jnp.sum(e, axis=1, keepdims=True) for e in es]), 1e-30)
        return [e / d for e in es]

    cmp_ok = (CMP_STRIDE * lane + CMP_BLOCK - 1) <= qpos
    (p1,) = softmax_rows([_mm_nt(qn, kc)], [cmp_ok])
    o_cmp = _mm(p1, vc)
    gq = NSA_KV * dec_t
    imp = functools.reduce(jnp.add, [p1[h * gq:(h + 1) * gq] for h in range(NSA_HPG)])
    imps = _mm_hi(imp, mt_ref[...])
    blk = lane[0:gq]
    qp8 = qpos[0:gq]
    qblk = qp8 // SLC_BLOCK
    causal = blk * SLC_BLOCK <= qp8
    forced = (blk == 0) | (blk == qblk) | (blk == qblk - 1)
    score = jnp.where(causal, jnp.where(forced, jnp.inf, imps), -jnp.inf)
    n_slc = -(-(PAST_LEN + dec_t) // SLC_BLOCK)
    cnt = jnp.zeros((gq, LANE), F32)
    for j in range(n_slc):
        col = score[:, j:j + 1]
        cnt = cnt + jnp.where(blk > j, jnp.where(col >= score, 1.0, 0.0), jnp.where(col > score, 1.0, 0.0))
    sel = jnp.where(cnt < N_SELECT, jnp.where(score > -jnp.inf, 1.0, 0.0), 0.0)

    def sel_rows(j0):
        pick = jnp.where(lane[0:gq] < SLC_BLOCK, sel[:, j0:j0 + 1], sel[:, j0 + 1:j0 + 2])
        return jnp.concatenate([pick] * NSA_HPG, axis=0) > 0.5

    zeros_k = jnp.zeros((PAGE_SIZE - DEC_ROWS, LANE), F32)
    new_ok = (lane < dec_t) & (lane <= qi)
    kvn = kvn_ref[sq]
    s_list = [_mm_nt(qr, pages[p][0, 0, :, 256:384]) for p in range(N_PAGES)]
    s_list.append(_mm_nt(qr, jnp.concatenate([kvn[:, 256:384], zeros_k], axis=0)))
    ok_list = [sel_rows(2 * p) for p in range(N_PAGES)]
    own = jnp.concatenate([sel[:, 2 * N_PAGES:2 * N_PAGES + 1]] * NSA_HPG, axis=0) > 0.5
    ok_list.append(new_ok & own)
    p_list = softmax_rows(s_list, ok_list)
    o_slc = functools.reduce(jnp.add, [_mm(p_list[p], pages[p][0, 0, :, 384:512]) for p in range(N_PAGES)])
    o_slc = o_slc + _mm(p_list[N_PAGES], jnp.concatenate([kvn[:, 384:512], zeros_k], axis=0))

    n_wt = WINDOW // PAGE_SIZE
    wn = wn_ref[sq]
    s_list, ok_list = [], []
    for t in range(n_wt):
        s_list.append(_mm_nt(qr, cw_ref[0, sq, t * PAGE_SIZE:(t + 1) * PAGE_SIZE, 0:128]))
        ok_list.append(qpos - (PAST_LEN - WINDOW + t * PAGE_SIZE + lane) < WINDOW)
    s_list.append(_mm_nt(qr, jnp.concatenate([wn[:, 0:128], zeros_k], axis=0)))
    ok_list.append(new_ok)
    p_list = softmax_rows(s_list, ok_list)
    o_win = functools.reduce(jnp.add, [_mm(p_list[t], cw_ref[0, sq, t * PAGE_SIZE:(t + 1) * PAGE_SIZE, 128:256])
                                       for t in range(n_wt)])
    o_win = o_win + _mm(p_list[n_wt], jnp.concatenate([wn[:, 128:256], zeros_k], axis=0))
    g = g_ref[sq]
    o_ref[sq] = g[:, 0:128] * o_cmp + g[:, 128:256] * o_slc + g[:, 256:384] * o_win


def _nsa_decode_pallas(l, q, gates, cache_kv, page_table, kv_new, cache_win, win_new, pe, cw):
    b, tq = q.shape[:2]
    scale = HEAD_DIM ** -0.5
    eye = jnp.eye(NSA_KV, dtype=F32)

    def rows(x):
        x5 = x.reshape(b, tq, NSA_KV, NSA_HPG, HEAD_DIM).transpose(0, 3, 2, 1, 4)
        return jnp.einsum('bhgqd,gk->bhgqkd', x5, eye).reshape(b, DEC_QROWS, LANE).astype(BF16)

    qn = rows(q * scale)
    qr = rows(_rope_rows(q, PAST_LEN + jnp.arange(tq)) * scale)
    g5 = gates.reshape(b, tq, NSA_KV, NSA_HPG, 3).transpose(0, 3, 2, 1, 4)
    lane_g = jnp.repeat(eye, HEAD_DIM, axis=1)
    gate_b = jnp.einsum('bhgqj,gn->bhgqjn', g5, lane_g).reshape(b, DEC_QROWS, 3 * LANE)
    w_all, bias = _cmp_weights(pe, cw)
    n_cmp = PAST_LEN // CMP_STRIDE - 1
    n_slc = -(-(PAST_LEN + tq) // SLC_BLOCK)
    mt = np.zeros((LANE, LANE), np.float32)
    mt[:n_cmp, :n_slc] = _slc_cmp_matrix(n_slc, n_cmp, n_cmp).T
    pad_rows = lambda a: jnp.pad(a, ((0, 0), (0, DEC_ROWS - tq), (0, 0)))
    n_pool = cache_kv.shape[1]
    ckv = cache_kv.reshape(DEPTH, n_pool, PAGE_SIZE, 512)
    page_spec = lambda sq, p: pl.BlockSpec((1, 1, PAGE_SIZE, 512),
                                           lambda bi, pt: (l, pt[bi * DEC_SEQS + sq, p], 0, 0))
    per_seq = lambda r, w: pl.BlockSpec((DEC_SEQS, r, w), lambda bi, pt: (bi, 0, 0))
    const = lambda shape: pl.BlockSpec(shape, lambda bi, pt: (0,) * len(shape))
    grid_spec = pltpu.PrefetchScalarGridSpec(
        num_scalar_prefetch=1,
        grid=(b // DEC_SEQS,),
        in_specs=[page_spec(sq, p) for sq in range(DEC_SEQS) for p in range(N_PAGES)]
                 + [per_seq(DEC_ROWS, 512),
                    pl.BlockSpec((1, DEC_SEQS, WINDOW, 256), lambda bi, pt: (l, bi, 0, 0)),
                    per_seq(DEC_ROWS, 256), per_seq(DEC_QROWS, LANE), per_seq(DEC_QROWS, LANE),
                    per_seq(DEC_QROWS, 3 * LANE), const((CMP_STRIDE * 256, 512)), const((1, 256)), const((LANE, LANE))],
        out_specs=per_seq(DEC_QROWS, LANE),
        scratch_shapes=[pltpu.VMEM((DEC_SEQS, PAST_LEN, LANE), F32), pltpu.VMEM((DEC_SEQS, PAST_LEN, LANE), F32)],
    )
    o = pl.pallas_call(
        functools.partial(_nsa_dec_body, tq),
        grid_spec=grid_spec,
        out_shape=jax.ShapeDtypeStruct((b, DEC_QROWS, LANE), F32),
        compiler_params=_cparams(("arbitrary",)),
        name="nsa_decode",
    )(page_table, *([ckv] * (DEC_SEQS * N_PAGES)), pad_rows(kv_new), cache_win.reshape(DEPTH, b, WINDOW, 256), pad_rows(win_new),
      qn, qr, gate_b, w_all, bias, jnp.asarray(mt))
    o6 = o.reshape(b, NSA_HPG, NSA_KV, tq, NSA_KV, HEAD_DIM)
    return jnp.einsum('bhgqkd,gk->bqghd', o6, eye).reshape(b, tq, NSA_WIDTH)


def _rwkv_mix(cols, z, prev, s0, mu, w0, w_up, a0, a_up, k_k, k_a, r_k, ln_w):
    b, t = cols.shape[:2]
    shifted = jnp.concatenate([prev[:, None], cols[:, :-1]], axis=1)
    xx = cols + (shifted - cols) * mu
    sp = np.cumsum([RWKV_WIDTH, RWKV_WIDTH, RWKV_WIDTH, RWKV_LORA_W])
    r, k, v, wd, ad = jnp.split(xx, sp, axis=-1)
    w = -jax.nn.softplus(-(w0 + jnp.dot(jnp.tanh(wd), w_up, precision=HI))) - 0.5
    logw = -jnp.exp(w)
    a = jax.nn.sigmoid(a0 + jnp.dot(ad, a_up, precision=HI))

    def heads(u):
        return u.reshape(b, t, RWKV_HEADS, RWKV_HEAD)

    kk = heads(k * k_k)
    kk = kk / jnp.maximum(jnp.sqrt(jnp.sum(kk * kk, axis=-1, keepdims=True)), 1e-12)
    k = k * (1.0 + (a - 1.0) * k_a)
    r, k, v, logw, a = heads(r), heads(k), heads(v), heads(logw), heads(a)
    c = min(RWKV_CHUNK, t)
    nc = t // c

    def chunks(u):
        return u.reshape(b, nc, c, RWKV_HEADS, RWKV_HEAD).transpose(1, 0, 3, 2, 4)

    r, k, v, logw, kk, a = (chunks(u) for u in (r, k, v, logw, kk, a))
    cum = jnp.cumsum(logw, axis=3)
    w_in = jnp.exp(cum)
    w_ex = jnp.exp(cum - logw)
    w_end = w_in[..., -1:, :]
    alpha_t = -kk * w_ex
    beta_h = kk * a / w_in
    k_h = k / w_in
    r_t = r * w_in
    tri_s = jnp.tril(jnp.ones((c, c), F32), -1)
    tri_i = jnp.tril(jnp.ones((c, c), F32))
    mm = functools.partial(jnp.einsum, precision=HI)
    a_ab = mm('nbhtk,nbhsk->nbhts', alpha_t, beta_h) * tri_s
    a_ak = mm('nbhtk,nbhsk->nbhts', alpha_t, k_h) * tri_s
    a_rb = mm('nbhtk,nbhsk->nbhts', r_t, beta_h) * tri_i
    a_rk = mm('nbhtk,nbhsk->nbhts', r_t, k_h) * tri_i
    eye = jnp.eye(c, dtype=F32)
    tm = eye + a_ab
    pw = a_ab
    steps = 1
    while steps * 2 < c:
        pw = mm('nbhts,nbhsu->nbhtu', pw, pw)
        tm = mm('nbhts,nbhsu->nbhtu', tm, eye + pw)
        steps *= 2
    p_m = mm('nbhts,nbhsk->nbhtk', tm, alpha_t)
    q_m = mm('nbhts,nbhsv->nbhtv', tm, mm('nbhts,nbhsv->nbhtv', a_ak, v))
    b_t = beta_h * w_end
    k_t = k_h * w_end
    m_m = mm('nbhtk,nbhtj->nbhkj', p_m, b_t)
    n_m = mm('nbhtv,nbhtk->nbhvk', q_m, b_t) + mm('nbhtv,nbhtk->nbhvk', v, k_t)
    r_p = r_t + mm('nbhts,nbhsk->nbhtk', a_rb, p_m)
    o_p = mm('nbhts,nbhsv->nbhtv', a_rb, q_m) + mm('nbhts,nbhsv->nbhtv', a_rk, v)

    def step(s, inp):
        m_c, n_c, we_c, rp_c, op_c = inp
        o = mm('bhtk,bhvk->bhtv', rp_c, s) + op_c
        s = s * we_c + mm('bhvk,bhkj->bhvj', s, m_c) + n_c
        return s, o

    s_t, o = lax.scan(step, s0, (m_m, n_m, w_end, r_p, o_p))
    o = o.transpose(1, 0, 3, 2, 4).reshape(b, t, RWKV_HEADS, RWKV_HEAD)
    r, k, v = (u.transpose(1, 0, 3, 2, 4).reshape(b, t, RWKV_HEADS, RWKV_HEAD) for u in (r, k, v))
    mean = jnp.mean(o, axis=-1, keepdims=True)
    var = jnp.mean(jnp.square(o - mean), axis=-1, keepdims=True)
    o = (o - mean) * lax.rsqrt(var + GN_EPS) * ln_w.reshape(RWKV_HEADS, RWKV_HEAD)
    o = o + jnp.sum(r * k * r_k, axis=-1, keepdims=True) * v
    y = o.reshape(b, t, RWKV_WIDTH) * jax.nn.silu(z)
    return y, cols[:, -1], s_t


def _hgrn_mix(q, fz, i, z, s0, lb, ln_w):
    b, t = q.shape[:2]
    log_f = jnp.logaddexp(jnp.log(lb), jnp.log1p(-lb) + jax.nn.log_sigmoid(fz))
    kf = (1.0 - lb) * jax.nn.sigmoid(-fz)
    chunk = HGRN_CHUNK if t % HGRN_CHUNK == 0 else t
    nc = t // chunk

    def chunks(u):
        return u.reshape(b, nc, chunk, HGRN_HEADS, HGRN_HEAD).transpose(1, 0, 3, 2, 4)

    mask = jnp.tril(jnp.ones((chunk, chunk), dtype=bool))

    def step(s, inp):
        q_c, lf_c, k_c, i_c = inp
        cl = jnp.cumsum(lf_c, axis=2)
        dec = jnp.exp(jnp.where(mask[:, :, None], cl[:, :, :, None, :] - cl[:, :, None, :, :], -jnp.inf))
        att = jnp.einsum('bhtk,bhtsk,bhsk->bhts', q_c, dec, k_c)
        o = jnp.einsum('bhts,bhsv->bhtv', att, i_c) + jnp.einsum('bhtk,bhkv->bhtv', q_c * jnp.exp(cl), s)
        cl_end = cl[:, :, -1:]
        s = jnp.exp(cl_end[:, :, 0])[..., None] * s + jnp.einsum('bhsk,bhsv->bhkv', k_c * jnp.exp(cl_end - cl), i_c)
        return s, o

    s_t, o = lax.scan(step, s0, (chunks(q), chunks(log_f), chunks(kf), chunks(i)))
    o = o.transpose(1, 0, 3, 2, 4).reshape(b, t, HGRN_HEADS, HGRN_HEAD)
    o = o * lax.rsqrt(jnp.mean(o * o, axis=-1, keepdims=True) + RMS_EPS) * ln_w.reshape(HGRN_HEADS, HGRN_HEAD)
    y = o.reshape(b, t, HGRN_WIDTH) * jax.nn.silu(z)
    return y, s_t


def _mm(a, b):
    return jnp.dot(a.astype(BF16), b.astype(BF16), preferred_element_type=F32)


def _mm_nt(a, b):
    return lax.dot_general(a.astype(BF16), b.astype(BF16), (((1,), (1,)), ((), ())), preferred_element_type=F32)


def _mm_tn(a, b):
    return lax.dot_general(a.astype(BF16), b.astype(BF16), (((0,), (0,)), ((), ())), preferred_element_type=F32)


def _mm_hi(a, b):
    return jnp.dot(a, b, preferred_element_type=F32, precision=HI)


def _split3(x):
    hi = x.astype(BF16)
    r1 = x - hi.astype(F32)
    mid = r1.astype(BF16)
    return hi, mid, (r1 - mid.astype(F32)).astype(BF16)


def _cumsum_rows(tri_bf16, x):
    return functools.reduce(jnp.add, [jnp.dot(tri_bf16, piece, preferred_element_type=F32) for piece in _split3(x)])


def _mm_x3(a, b):
    a_hi, b_hi = a.astype(BF16), b.astype(BF16)
    a_lo, b_lo = (a - a_hi.astype(F32)).astype(BF16), (b - b_hi.astype(F32)).astype(BF16)
    dot = functools.partial(jnp.dot, preferred_element_type=F32)
    return dot(a_hi, b_hi) + dot(a_hi, b_lo) + dot(a_lo, b_hi)


def _mm_ones_rhs(ones_bf16, b):
    hi = b.astype(BF16)
    lo = (b - hi.astype(F32)).astype(BF16)
    return (jnp.dot(ones_bf16, hi, preferred_element_type=F32) + jnp.dot(ones_bf16, lo, preferred_element_type=F32))


def _mm_ones(a, ones_bf16):
    hi = a.astype(BF16)
    lo = (a - hi.astype(F32)).astype(BF16)
    return (jnp.dot(hi, ones_bf16, preferred_element_type=F32) + jnp.dot(lo, ones_bf16, preferred_element_type=F32))


def _log_sigmoid(x):
    return jnp.minimum(x, 0.0) - jnp.log1p(jnp.exp(-jnp.abs(x)))


HGRN_SUB = 16
MIX_TILE = 512


def _hgrn_body(n_t, q_ref, f_ref, i_ref, z_ref, lbc_ref, lnw_ref, y_ref, s_ref, st_ref, o_ref):
    ti = pl.program_id(1)
    c, sc = HGRN_CHUNK, HGRN_SUB
    heads = range(HGRN_HEADS)
    hsl = [slice(h * HGRN_HEAD, (h + 1) * HGRN_HEAD) for h in heads]

    @pl.when(ti == 0)
    def _():
        st_ref[...] = jnp.zeros(st_ref.shape, F32)

    log_lb, log_1m_lb, one_m_lb = lbc_ref[0:1, :], lbc_ref[1:2, :], lbc_ref[2:3, :]
    tri = (lax.broadcasted_iota(jnp.int32, (c, c), 0) >= lax.broadcasted_iota(jnp.int32, (c, c), 1)).astype(BF16)
    sub_s = lax.broadcasted_iota(jnp.int32, (sc, HGRN_WIDTH), 0)

    def chunk(ci, carry):
        r0 = pl.multiple_of(ci * c, c)
        q = q_ref[0, pl.ds(r0, c), :]
        fz = f_ref[0, pl.ds(r0, c), :]
        iv = i_ref[0, pl.ds(r0, c), :]
        b_ = log_1m_lb + _log_sigmoid(fz)
        lf = jnp.maximum(log_lb, b_) + jnp.log1p(jnp.exp(-jnp.abs(log_lb - b_)))
        kf = one_m_lb * jax.nn.sigmoid(-fz)
        cl = _cumsum_rows(tri, lf)
        c_end = cl[c - 1:c]
        q_dec = q * jnp.exp(cl)
        k_end = kf * jnp.exp(c_end - cl)
        w_end = jnp.exp(c_end)
        st = [st_ref[h] for h in heads]
        o = [_mm_nt(q_dec[:, hsl[h]], st[h]) for h in heads]
        upd = [_mm_tn(iv[:, hsl[h]], k_end[:, hsl[h]]) for h in heads]
        for h in heads:
            st_ref[h] = st[h] * w_end[:, hsl[h]] + upd[h]
        pieces = []
        for blk in range(c // sc):
            lo = blk * sc
            cl_r, q_r, k_r, i_r = cl[lo:lo + sc], q[lo:lo + sc], kf[lo:lo + sc], iv[lo:lo + sc]
            acc = [o[h][lo:lo + sc] for h in heads]
            if blk > 0:
                bnd = cl[lo - 1:lo]
                q_b = q_r * jnp.exp(cl_r - bnd)
                k_b = kf[:lo] * jnp.exp(bnd - cl[:lo])
                att = [_mm_nt(q_b[:, hsl[h]], k_b[:, hsl[h]]) for h in heads]
                acc = [acc[h] + _mm(att[h], iv[:lo, hsl[h]]) for h in heads]
            rows = []
            for t in range(sc):
                w = jnp.where(sub_s <= t, jnp.exp(cl_r[t:t + 1] - cl_r), 0.0)
                e = q_r[t:t + 1] * w * k_r
                d = jnp.concatenate([jnp.broadcast_to(jnp.sum(e[:, hsl[h]], axis=1, keepdims=True), (sc, HGRN_HEAD))
                                     for h in heads], axis=1)
                rows.append(jnp.sum(d * i_r, axis=0, keepdims=True))
            pieces.append(jnp.concatenate(acc, axis=1) + jnp.concatenate(rows, axis=0))
        o_ref[pl.ds(r0, c), :] = jnp.concatenate(pieces, axis=0)
        return carry

    lax.fori_loop(0, o_ref.shape[0] // c, chunk, 0)
    z = z_ref[0]
    gate = lnw_ref[...] * (z * jax.nn.sigmoid(z))
    for h in heads:
        o = o_ref[:, hsl[h]]
        y_ref[0, :, hsl[h]] = o * lax.rsqrt(jnp.mean(o * o, axis=-1, keepdims=True) + RMS_EPS) * gate[:, hsl[h]]

    @pl.when(ti == n_t - 1)
    def _():
        for h in heads:
            s_ref[0, h] = st_ref[h].T


def _hgrn_prompt(proj_a3, lb, ln_w):
    b, t = proj_a3.shape[:2]
    tt = min(MIX_TILE, t)
    n_t = t // tt
    lbc = jnp.concatenate([jnp.log(lb)[None], jnp.log1p(-lb)[None], (1.0 - lb)[None],
                           jnp.zeros((5, HGRN_WIDTH), F32)], axis=0)
    col = lambda off: pl.BlockSpec((1, tt, HGRN_WIDTH), lambda bi, i: (bi, i, off // HGRN_WIDTH))
    return pl.pallas_call(
        functools.partial(_hgrn_body, n_t),
        grid=(b, n_t),
        in_specs=[col(A_HQ), col(A_HF), col(A_HI), col(A_HZ),
                  pl.BlockSpec((8, HGRN_WIDTH), lambda bi, i: (0, 0)),
                  pl.BlockSpec((1, HGRN_WIDTH), lambda bi, i: (0, 0))],
        out_specs=[pl.BlockSpec((1, tt, HGRN_WIDTH), lambda bi, i: (bi, i, 0)),
                   pl.BlockSpec((1, HGRN_HEADS, HGRN_HEAD, HGRN_HEAD), lambda bi, i: (bi, 0, 0, 0))],
        out_shape=[jax.ShapeDtypeStruct((b, t, HGRN_WIDTH), F32),
                   jax.ShapeDtypeStruct((b, HGRN_HEADS, HGRN_HEAD, HGRN_HEAD), F32)],
        scratch_shapes=[pltpu.VMEM((HGRN_HEADS, HGRN_HEAD, HGRN_HEAD), F32), pltpu.VMEM((tt, HGRN_WIDTH), F32)],
        compiler_params=_cparams(("arbitrary", "arbitrary")),
        name="hgrn_prompt",
    )(proj_a3, proj_a3, proj_a3, proj_a3, lbc, ln_w.reshape(1, HGRN_WIDTH))


def _rwkv_body(n_t, cols_ref, z_ref, mu_ref, w0_ref, wup_ref, a0_ref, aup_ref, kk_ref, ka_ref, rk_ref, lnw_ref, bd_ref,
               y_ref, s_ref, prev_ref, st_ref, r_s, k_s, v_s, lw_s, kk_s, a_s, o_s):
    ti = pl.program_id(1)
    tt = cols_ref.shape[1]
    c = RWKV_CHUNK
    n_pair = RWKV_HEADS // 2

    @pl.when(ti == 0)
    def _():
        prev_ref[...] = jnp.zeros(prev_ref.shape, F32)
        st_ref[...] = jnp.zeros(st_ref.shape, F32)

    x = cols_ref[0]
    first = lax.broadcasted_iota(jnp.int32, (tt, 1), 0) == 0
    x_prev = jnp.where(first, prev_ref[0:1, :], pltpu.roll(x, 1, 0))
    prev_ref[0:1, :] = x[tt - 1:tt, :]
    xx = x + (x_prev - x) * mu_ref[...]
    r, k, v = xx[:, 0:RWKV_WIDTH], xx[:, RWKV_WIDTH:2 * RWKV_WIDTH], xx[:, 2 * RWKV_WIDTH:3 * RWKV_WIDTH]
    lora = xx[:, 3 * RWKV_WIDTH:]
    nx = -(w0_ref[...] + _mm_x3(jnp.tanh(lora), wup_ref[...]))
    w_log = -(jnp.maximum(nx, 0.0) + jnp.log1p(jnp.exp(-jnp.abs(nx)))) - 0.5
    lw_s[...] = -jnp.exp(w_log)
    a = jax.nn.sigmoid(a0_ref[...] + _mm_x3(lora, aup_ref[...]))
    bd = bd_ref[...]
    kk_raw = k * kk_ref[...]
    kk_s[...] = kk_raw / jnp.maximum(jnp.sqrt(_mm_ones(kk_raw * kk_raw, bd)), 1e-12)
    k2 = k * (1.0 + (a - 1.0) * ka_ref[...])
    r_s[...] = r
    k_s[...] = k2
    v_s[...] = v
    a_s[...] = a
    bonus = _mm_ones(r * k2 * rk_ref[...], bd) * v

    rows = lax.broadcasted_iota(jnp.int32, (c, LANE), 0)
    lane_in = lax.broadcasted_iota(jnp.int32, (c, LANE), 1) % RWKV_HEAD
    strict = rows > lane_in
    incl = rows >= lane_in
    eye2 = (rows == lane_in).astype(F32)
    tri = (lax.broadcasted_iota(jnp.int32, (c, c), 0) >= lax.broadcasted_iota(jnp.int32, (c, c), 1)).astype(BF16)
    lane1 = lax.broadcasted_iota(jnp.int32, (1, LANE), 1)
    m0, m1 = lane1 < RWKV_HEAD, lane1 >= RWKV_HEAD
    bdm = (lax.broadcasted_iota(jnp.int32, (LANE, LANE), 0) // RWKV_HEAD
           == lax.broadcasted_iota(jnp.int32, (LANE, LANE), 1) // RWKV_HEAD)

    def blockdiag(zz):
        zz = zz.astype(BF16)
        return jnp.concatenate([jnp.where(m0, zz, 0), jnp.where(m1, zz, 0)], axis=0)

    def chunk(ci, carry):
        r0 = pl.multiple_of(ci * c, c)
        ds = pl.ds(r0, c)
        lw = lw_s[ds, :]
        cum = _cumsum_rows(tri, lw)
        w_in, w_ex, inv_in = jnp.exp(cum), jnp.exp(cum - lw), jnp.exp(-cum)
        w_end = jnp.exp(cum[c - 1:c])
        kk_c, vv = kk_s[ds, :], v_s[ds, :]
        alpha = -kk_c * w_ex
        beta_h = kk_c * a_s[ds, :] * inv_in
        k_h = k_s[ds, :] * inv_in
        r_t = r_s[ds, :] * w_in
        beta_e, k_e = beta_h * w_end, k_h * w_end
        pairs = range(n_pair)
        sls = [slice(p * LANE, (p + 1) * LANE) for p in pairs]
        al = [alpha[:, sl] for sl in sls]
        rt = [r_t[:, sl] for sl in sls]
        vb = [blockdiag(vv[:, sl]) for sl in sls]
        aa = [_mm_nt(jnp.concatenate([al[p], rt[p]], axis=0),
                     jnp.concatenate([blockdiag(beta_h[:, sls[p]]), blockdiag(k_h[:, sls[p]])], axis=0))
              for p in pairs]
        a_ab = [jnp.where(strict, aa[p][0:c, 0:LANE], 0.0) for p in pairs]
        a_ak = [jnp.where(strict, aa[p][0:c, LANE:], 0.0) for p in pairs]
        a_rb = [jnp.where(incl, aa[p][c:, 0:LANE], 0.0) for p in pairs]
        a_rk = [jnp.where(incl, aa[p][c:, LANE:], 0.0) for p in pairs]
        akv = [_mm(a_ak[p], vb[p]) for p in pairs]
        rkv = [_mm(a_rk[p], vb[p]) for p in pairs]
        vk = [_mm_tn(vv[:, sls[p]], k_e[:, sls[p]]) for p in pairs]
        tm = [eye2 + a_ab[p] for p in pairs]
        pw = a_ab
        n = 1
        while 2 * n < c:
            pw = [_mm(pw[p], blockdiag(pw[p])) for p in pairs]
            tm = [tm[p] + _mm(tm[p], blockdiag(pw[p])) for p in pairs]
            n *= 2
        pq = [_mm(tm[p], jnp.concatenate([blockdiag(al[p]), blockdiag(akv[p])], axis=1)) for p in pairs]
        ro = [_mm(a_rb[p], jnp.concatenate([blockdiag(pq[p][:, 0:LANE]), blockdiag(pq[p][:, LANE:])], axis=1))
              for p in pairs]
        mn = [_mm_tn(pq[p], beta_e[:, sls[p]]) for p in pairs]
        outs = []
        for p in pairs:
            m_bd = jnp.where(bdm, mn[p][0:LANE], 0.0)
            n_f = jnp.where(bdm, mn[p][LANE:] + vk[p], 0.0)
            st = st_ref[p]
            outs.append(_mm_nt(rt[p] + ro[p][:, 0:LANE], blockdiag(st)) + ro[p][:, LANE:] + rkv[p])
            st_ref[p] = st * w_end[:, sls[p]] + _mm(st, m_bd) + n_f[0:RWKV_HEAD] + n_f[RWKV_HEAD:]
        o_s[ds, :] = jnp.concatenate(outs, axis=1)
        return carry

    lax.fori_loop(0, tt // c, chunk, 0)
    o = o_s[...]
    inv_n = 1.0 / RWKV_HEAD
    d = o - _mm_ones(o, bd) * inv_n
    var = _mm_ones(d * d, bd) * inv_n
    z = z_ref[0]
    y_ref[0] = (d * lax.rsqrt(var + GN_EPS) * lnw_ref[...] + bonus) * (z * jax.nn.sigmoid(z))

    @pl.when(ti == n_t - 1)
    def _():
        s_ref[0] = st_ref[...]


def _rwkv_prompt(rcols, proj_a3, mu, w0, w_up, a0, a_up, k_k, k_a, r_k, ln_w):
    b, t = rcols.shape[:2]
    tt = min(MIX_TILE, t)
    n_t = t // tt
    n_pair = RWKV_HEADS // 2
    zpad = jnp.zeros((RWKV_LORA_W, RWKV_WIDTH), F32)
    wup = jnp.concatenate([w_up, zpad], axis=0)
    aup = jnp.concatenate([zpad, a_up], axis=0)
    hid = np.arange(RWKV_WIDTH) // RWKV_HEAD
    bd = jnp.asarray((hid[:, None] == hid[None, :]).astype(np.float32)).astype(BF16)
    vec = lambda a: a.reshape(1, -1)
    full = lambda shape: pl.BlockSpec(shape, lambda bi, i: (0,) * len(shape))
    y, s = pl.pallas_call(
        functools.partial(_rwkv_body, n_t),
        grid=(b, n_t),
        in_specs=[pl.BlockSpec((1, tt, SHIFT_W), lambda bi, i: (bi, i, 0)),
                  pl.BlockSpec((1, tt, RWKV_WIDTH), lambda bi, i: (bi, i, A_RZ // RWKV_WIDTH)),
                  full((1, SHIFT_W)), full((1, RWKV_WIDTH)), full((LANE, RWKV_WIDTH)), full((1, RWKV_WIDTH)),
                  full((LANE, RWKV_WIDTH)), full((1, RWKV_WIDTH)), full((1, RWKV_WIDTH)), full((1, RWKV_WIDTH)),
                  full((1, RWKV_WIDTH)), full((RWKV_WIDTH, RWKV_WIDTH))],
        out_specs=[pl.BlockSpec((1, tt, RWKV_WIDTH), lambda bi, i: (bi, i, 0)),
                   pl.BlockSpec((1, n_pair, RWKV_HEAD, LANE), lambda bi, i: (bi, 0, 0, 0))],
        out_shape=[jax.ShapeDtypeStruct((b, t, RWKV_WIDTH), F32),
                   jax.ShapeDtypeStruct((b, n_pair, RWKV_HEAD, LANE), F32)],
        scratch_shapes=[pltpu.VMEM((8, SHIFT_W), F32), pltpu.VMEM((n_pair, RWKV_HEAD, LANE), F32)]
                       + [pltpu.VMEM((tt, RWKV_WIDTH), F32)] * 7,
        compiler_params=_cparams(("arbitrary", "arbitrary")),
        name="rwkv_prompt",
    )(rcols, proj_a3, vec(mu), vec(w0), wup, vec(a0), aup, vec(k_k), vec(k_a), vec(r_k), vec(ln_w), bd)
    s = s.reshape(b, n_pair, RWKV_HEAD, 2, RWKV_HEAD).transpose(0, 1, 3, 2, 4)
    return y, s.reshape(b, RWKV_HEADS, RWKV_HEAD, RWKV_HEAD)


def _merge_body(x_ref, yn_ref, nz_ref, yr_ref, yh_ref, mg_ref, wb_ref, wo_ref, o_ref):
    nz = nz_ref[...]
    branches = (yn_ref[...] * (nz * jax.nn.sigmoid(nz)), yr_ref[...], yh_ref[...])
    acc = jnp.zeros(o_ref.shape, F32)
    for n, y in enumerate(branches):
        t = jnp.dot(y.astype(BF16), wb_ref[n], preferred_element_type=F32)
        acc = acc + jax.nn.sigmoid(mg_ref[:, n * D_MODEL:(n + 1) * D_MODEL]) * t
    o_ref[...] = x_ref[...] + jnp.dot(acc.astype(BF16), wo_ref[...], preferred_element_type=F32)


def _merge(x2d, y_nsa, proj_a, y_rwkv, y_hgrn, mg, wb, wo, tm):
    m = x2d.shape[0]
    row = lambda w: pl.BlockSpec((tm, w), lambda i: (i, 0))
    return pl.pallas_call(
        _merge_body,
        grid=(m // tm,),
        in_specs=[row(D_MODEL), row(BRANCH_WIDTH),
                  pl.BlockSpec((tm, BRANCH_WIDTH), lambda i: (i, A_NZ // BRANCH_WIDTH)),
                  row(BRANCH_WIDTH), row(BRANCH_WIDTH), row(N_BRANCH * D_MODEL),
                  pl.BlockSpec((N_BRANCH, BRANCH_WIDTH, D_MODEL), lambda i: (0, 0, 0)),
                  pl.BlockSpec((D_MODEL, D_MODEL), lambda i: (0, 0))],
        out_specs=row(D_MODEL),
        out_shape=jax.ShapeDtypeStruct((m, D_MODEL), F32),
        compiler_params=_cparams(("arbitrary",)),
        name="merge",
    )(x2d, y_nsa, proj_a, y_rwkv, y_hgrn, mg, wb, wo)


def _split_w_in(w):
    o = np.concatenate([[0], np.cumsum(IN_SIZES)])
    seg = lambda n: w[:, o[n]:o[n + 1]]
    pad = jnp.zeros((w.shape[0], A_Q - A_GATE - IN_SIZES[2]), w.dtype)
    wa = jnp.concatenate([seg(1), seg(2), pad, seg(0), seg(3), seg(5), seg(6), seg(7), seg(8), seg(9)], axis=1)
    return wa.astype(BF16), seg(4).astype(BF16), seg(10).astype(BF16)


def _layer(x, l, past, prm, lb, tabs, tm):
    b, t = x.shape[:2]
    m = b * t
    x2d = x.reshape(m, D_MODEL)
    nw = prm['norm_w'][l].reshape(1, D_MODEL)
    wa, wb_cols, wc = prm['w_split'][l]
    proj_a = _inproj(x2d, nw, wa, tm, 1536)
    rcols = _inproj(x2d, nw, wb_cols, tm, SHIFT_W).reshape(b, t, SHIFT_W)
    mg = _inproj(x2d, nw, wc, tm, 1536)
    seg = lambda off, w: proj_a[:, off:off + w].reshape(b, t, w)
    c_tab, su_tab, sd_tab, cos_t, sin_t = tabs
    if past is None:
        kv_new, win_new, kcsrc, ksb, vst, kwb, vwt = _kvpost(proj_a, (c_tab, su_tab, sd_tab), b, t, min(tm, t), True)
        w_all, bias = _cmp_weights(prm['nsa_cmp_pe'][l], prm['nsa_cmp_w'][l])
        kc, vct = _compress(kcsrc, w_all, bias, b, t)
        gate_t = seg(A_GATE, 3 * NSA_HEADS).transpose(0, 2, 1)
        y_nsa = _nsa_prompt(proj_a, gate_t, cos_t, sin_t, kc, vct, ksb, vst, kwb, vwt, b, t)
        win_state = win_new.reshape(b, t, 2, NSA_KV, HEAD_DIM)[:, -min(WINDOW, t):]
    else:
        cache_kv, page_table, cache_win, s_r, prev, s_h = past
        kv_new, win_new = _kvpost(proj_a, (c_tab, su_tab, sd_tab), b, t, tm, False)
        win5 = win_new.reshape(b, t, 2, NSA_KV, HEAD_DIM)
        win_state = jnp.concatenate([cache_win[l][:, t:], win5], axis=1)
        gates = jax.nn.sigmoid(seg(A_GATE, 3 * NSA_HEADS)).reshape(b, t, NSA_HEADS, 3)
        q = seg(A_Q, NSA_WIDTH).reshape(b, t, NSA_HEADS, HEAD_DIM)
        y_nsa = _nsa_decode_pallas(l, q, gates, cache_kv, page_table, kv_new.reshape(b, t, 512), cache_win,
                                   win_new.reshape(b, t, 256), prm['nsa_cmp_pe'][l], prm['nsa_cmp_w'][l])
    rwkv_prm = (prm['rwkv_mu'][l], prm['rwkv_w0'][l], prm['rwkv_w_up'][l], prm['rwkv_a0'][l], prm['rwkv_a_up'][l],
                prm['rwkv_k_k'][l], prm['rwkv_k_a'][l], prm['rwkv_r_k'][l], prm['rwkv_ln_w'][l])
    if past is None:
        proj_a3 = proj_a.reshape(b, t, A_WIDTH)
        y_rwkv, s_r = _rwkv_prompt(rcols, proj_a3, *rwkv_prm)
        shift_state = rcols[:, -1]
        y_hgrn, s_h = _hgrn_prompt(proj_a3, lb, prm['hgrn_ln_w'][l])
    else:
        y_rwkv, shift_state, s_r = _rwkv_mix(rcols, seg(A_RZ, RWKV_WIDTH), prev, s_r, *rwkv_prm)
        y_hgrn, s_h = _hgrn_mix(seg(A_HQ, HGRN_WIDTH), seg(A_HF, HGRN_WIDTH), seg(A_HI, HGRN_WIDTH),
                                seg(A_HZ, HGRN_WIDTH), s_h, lb, prm['hgrn_ln_w'][l])
    x_new = _merge(x2d, y_nsa.reshape(m, NSA_WIDTH), proj_a, y_rwkv.reshape(m, RWKV_WIDTH), y_hgrn.reshape(m, HGRN_WIDTH),
                   mg, prm['w_branch_bf16'][l], prm['w_out_bf16'][l], min(tm, 512))
    kv_state = kv_new.reshape(b, t, 4, NSA_KV, HEAD_DIM)
    return x_new.reshape(b, t, D_MODEL), (kv_state, win_state, s_r, shift_state, s_h)


def kernel(x_prompt, x_sample, cache_kv, cache_win, state_rwkv, state_shift, state_hgrn, page_table,
           norm_w, w_in, nsa_cmp_pe, nsa_cmp_w, rwkv_mu, rwkv_w0, rwkv_w_up, rwkv_a0, rwkv_a_up,
           rwkv_k_k, rwkv_k_a, rwkv_r_k, rwkv_ln_w, hgrn_lb_logits, hgrn_ln_w, w_branch, w_out, norm_f):
    prm = {'norm_w': norm_w, 'nsa_cmp_pe': nsa_cmp_pe, 'nsa_cmp_w': nsa_cmp_w,
           'rwkv_mu': rwkv_mu, 'rwkv_w0': rwkv_w0, 'rwkv_w_up': rwkv_w_up, 'rwkv_a0': rwkv_a0,
           'rwkv_a_up': rwkv_a_up, 'rwkv_k_k': rwkv_k_k, 'rwkv_k_a': rwkv_k_a, 'rwkv_r_k': rwkv_r_k,
           'rwkv_ln_w': rwkv_ln_w, 'hgrn_ln_w': hgrn_ln_w,
           'w_split': [_split_w_in(w_in[l]) for l in range(DEPTH)],
           'w_branch_bf16': w_branch.astype(BF16), 'w_out_bf16': w_out.astype(BF16)}
    cs = jnp.cumsum(jax.nn.softmax(hgrn_lb_logits, axis=0), axis=0)
    lbs = cs - cs[0:1]
    n_dec, dec_t = x_sample.shape[:2]
    bp, tp = x_prompt.shape[:2]
    tabs_p = _rope_tables(jnp.arange(tp))
    tabs_s = _rope_tables(PAST_LEN + jnp.arange(n_dec * dec_t) % dec_t)
    tm_p = min(1024, bp * tp)
    tm_s = n_dec * dec_t
    xp, xs = x_prompt, x_sample
    outs = [[] for _ in range(10)]
    for l in range(DEPTH):
        xp, st_p = _layer(xp, l, None, prm, lbs[l], tabs_p, tm_p)
        past = (cache_kv, page_table, cache_win, state_rwkv[l], state_shift[l], state_hgrn[l])
        xs, st_s = _layer(xs, l, past, prm, lbs[l], tabs_s, tm_s)
        for n in range(5):
            outs[2 * n].append(st_p[n])
            outs[2 * n + 1].append(st_s[n])
    y_prompt = _final_norm(xp.reshape(bp * tp, D_MODEL), norm_f, tm_p).reshape(xp.shape)
    y_sample = _final_norm(xs.reshape(n_dec * dec_t, D_MODEL), norm_f, tm_s).reshape(xs.shape)
    return (y_prompt, y_sample) + tuple(jnp.stack(o) for o in outs)
```

```python
import functools

import jax
import jax.numpy as jnp
import numpy as np
from jax import lax
from jax.experimental import pallas as pl
from jax.experimental.pallas import tpu as pltpu

F32 = jnp.float32
BF16 = jnp.bfloat16
HI = lax.Precision.HIGHEST

D_MODEL = 1024
DEPTH = 4
PAST_LEN = 2048
PAGE_SIZE = 128
NSA_HEADS = 8
NSA_KV = 2
HEAD_DIM = 64
NSA_HPG = NSA_HEADS // NSA_KV
NSA_WIDTH = NSA_HEADS * HEAD_DIM
ROT_DIM = HEAD_DIM // 4
ROPE_THETA = 500000.0
CMP_BLOCK = 32
CMP_STRIDE = 16
SLC_BLOCK = 64
N_SELECT = 16
WINDOW = 512
Q_BLOCK = 128
RWKV_HEADS = 8
RWKV_HEAD = 64
RWKV_WIDTH = RWKV_HEADS * RWKV_HEAD
RWKV_LORA_W = 64
RWKV_LORA_A = 64
SHIFT_W = 3 * RWKV_WIDTH + RWKV_LORA_W + RWKV_LORA_A
HGRN_HEADS = 4
HGRN_HEAD = 128
HGRN_WIDTH = HGRN_HEADS * HGRN_HEAD
HGRN_CHUNK = 64
N_BRANCH = 3
BRANCH_WIDTH = 512
KV_COLS = 6 * NSA_KV * HEAD_DIM
IN_SIZES = (NSA_WIDTH, KV_COLS, 3 * NSA_HEADS, NSA_WIDTH, SHIFT_W, RWKV_WIDTH,
            HGRN_WIDTH, HGRN_WIDTH, HGRN_WIDTH, HGRN_WIDTH, N_BRANCH * D_MODEL)
RMS_EPS = 1e-6
GN_EPS = 64e-5

LANE = 128
VMEM_LIMIT = 56 * 1024 * 1024
NEG = -1e30
KEY_TILE = 512
LOG2E = 1.4426950408889634
RWKV_CHUNK = 64

A_KV = 0
A_GATE = 768
A_Q = 1024
A_NZ = 1536
A_RZ = 2048
A_HQ = 2560
A_HF = 3072
A_HI = 3584
A_HZ = 4096
A_WIDTH = 4608


def _cparams(sem):
    return pltpu.CompilerParams(dimension_semantics=sem, vmem_limit_bytes=VMEM_LIMIT)


def _inproj_body(x_ref, nw_ref, w_ref, o_ref, h_ref):
    @pl.when(pl.program_id(1) == 0)
    def _():
        x = x_ref[...]
        ms = jnp.mean(x * x, axis=-1, keepdims=True)
        h_ref[...] = (x * lax.rsqrt(ms + RMS_EPS) * nw_ref[...]).astype(BF16)

    o_ref[...] = jnp.dot(h_ref[...], w_ref[...], preferred_element_type=F32)


def _inproj(x2d, nw, w, tm, tn):
    m, d = x2d.shape
    n = w.shape[1]
    return pl.pallas_call(
        _inproj_body,
        grid=(m // tm, n // tn),
        in_specs=[pl.BlockSpec((tm, d), lambda i, j: (i, 0)),
                  pl.BlockSpec((1, d), lambda i, j: (0, 0)),
                  pl.BlockSpec((d, tn), lambda i, j: (0, j))],
        out_specs=pl.BlockSpec((tm, tn), lambda i, j: (i, j)),
        out_shape=jax.ShapeDtypeStruct((m, n), F32),
        scratch_shapes=[pltpu.VMEM((tm, d), BF16)],
        compiler_params=_cparams(("arbitrary", "arbitrary")),
        name="inproj",
    )(x2d, nw, w)


def _rmsnorm_body(x_ref, w_ref, o_ref):
    x = x_ref[...]
    ms = jnp.mean(x * x, axis=-1, keepdims=True)
    o_ref[...] = x * lax.rsqrt(ms + RMS_EPS) * w_ref[...]


def _final_norm(x2d, w, tm):
    m, d = x2d.shape
    return pl.pallas_call(
        _rmsnorm_body,
        grid=(m // tm,),
        in_specs=[pl.BlockSpec((tm, d), lambda i: (i, 0)), pl.BlockSpec((1, d), lambda i: (0, 0))],
        out_specs=pl.BlockSpec((tm, d), lambda i: (i, 0)),
        out_shape=jax.ShapeDtypeStruct((m, d), F32),
        compiler_params=_cparams(("arbitrary",)),
        name="final_norm",
    )(x2d, w.reshape(1, d))


def _rope_tables(pos):
    half = ROT_DIM // 2
    inv = ROPE_THETA ** (-jnp.arange(0, ROT_DIM, 2, dtype=F32) / ROT_DIM)
    ang = pos.astype(F32)[:, None] * inv[None, :]
    cos, sin = jnp.cos(ang), jnp.sin(ang)
    n = pos.shape[0]
    ones = jnp.ones((n, HEAD_DIM - ROT_DIM), F32)
    zeros8 = jnp.zeros((n, half), F32)
    zeros = jnp.zeros((n, HEAD_DIM - ROT_DIM), F32)
    c = jnp.concatenate([cos, cos, ones], axis=1)
    s_up = jnp.concatenate([zeros8, sin, zeros], axis=1)
    s_dn = jnp.concatenate([-sin, zeros8, zeros], axis=1)
    tile = lambda a: jnp.concatenate([a] * NSA_KV, axis=1)
    return tile(c), tile(s_up), tile(s_dn), cos.T, sin.T


def _kvpost_body(with_attn, p_ref, c_ref, su_ref, sd_ref, kv_ref, win_ref, *extra):
    x = p_ref[...]
    c, su, sd = c_ref[...], su_ref[...], sd_ref[...]

    def rope(v):
        return v * c + pltpu.roll(v, ROT_DIM // 2, 1) * su + pltpu.roll(v, LANE - ROT_DIM // 2, 1) * sd

    k_slc = rope(x[:, 256:384])
    k_win = rope(x[:, 512:640])
    kv_ref[:, 0:256] = x[:, 0:256]
    kv_ref[:, 256:384] = k_slc
    kv_ref[:, 384:512] = x[:, 384:512]
    win_ref[:, 0:128] = k_win
    win_ref[:, 128:256] = x[:, 640:768]
    if with_attn:
        kc_ref, ks_ref, vst_ref, kw_ref, vwt_ref = extra
        kc_ref[...] = x[:, 0:256].astype(BF16)
        ks_ref[...] = k_slc.astype(BF16)
        vst_ref[0] = x[:, 384:512].T.astype(BF16)
        kw_ref[...] = k_win.astype(BF16)
        vwt_ref[0] = x[:, 640:768].T.astype(BF16)


def _kvpost(proj_a, tabs, b, t, tm, with_attn):
    m = b * t
    nt = tabs[0].shape[0] // tm
    tab_spec = pl.BlockSpec((tm, LANE), lambda i: (i % nt, 0))
    out_shape = [jax.ShapeDtypeStruct((m, 512), F32), jax.ShapeDtypeStruct((m, 256), F32)]
    out_specs = [pl.BlockSpec((tm, 512), lambda i: (i, 0)), pl.BlockSpec((tm, 256), lambda i: (i, 0))]
    if with_attn:
        out_shape += [jax.ShapeDtypeStruct((m, 256), BF16), jax.ShapeDtypeStruct((m, LANE), BF16),
                      jax.ShapeDtypeStruct((b, LANE, t), BF16), jax.ShapeDtypeStruct((m, LANE), BF16),
                      jax.ShapeDtypeStruct((b, LANE, t), BF16)]
        tspec = pl.BlockSpec((1, LANE, tm), lambda i: (i // nt, 0, i % nt))
        out_specs += [pl.BlockSpec((tm, 256), lambda i: (i, 0)), pl.BlockSpec((tm, LANE), lambda i: (i, 0)),
                      tspec, pl.BlockSpec((tm, LANE), lambda i: (i, 0)), tspec]
    return pl.pallas_call(
        functools.partial(_kvpost_body, with_attn),
        grid=(m // tm,),
        in_specs=[pl.BlockSpec((tm, KV_COLS), lambda i: (i, 0)), tab_spec, tab_spec, tab_spec],
        out_specs=out_specs,
        out_shape=out_shape,
        compiler_params=_cparams(("arbitrary",)),
        name="kvpost",
    )(proj_a, *tabs)


def _cmp_weights(pe, cw):
    eye = jnp.eye(NSA_KV, dtype=F32)
    cols = []
    for s in range(2):
        for part in range(2):
            w = cw[s, part * CMP_STRIDE:(part + 1) * CMP_STRIDE]
            blk = jnp.einsum('jde,gh->jgdhe', w, eye).reshape(CMP_STRIDE, LANE, LANE)
            full = jnp.zeros((CMP_STRIDE, 2, LANE, LANE), F32).at[:, s].set(blk)
            cols.append(full.reshape(CMP_STRIDE * 2 * LANE, LANE))
    w_all = jnp.concatenate(cols, axis=1).astype(BF16)
    bias = jnp.einsum('sjd,sjde->se', pe, cw)
    bias = jnp.concatenate([bias[0], bias[0], bias[1], bias[1]]).reshape(1, 2 * LANE)
    return w_all, bias


def _cmp_body(sub_ref, w_ref, b_ref, kc_ref, vct_ref):
    r = jnp.dot(sub_ref[0], w_ref[...], preferred_element_type=F32)
    n = r.shape[0]
    b = b_ref[...]
    kc = r[:, 0:128] + pltpu.roll(r[:, 128:256], n - 1, 0) + b[:, 0:128]
    vc = r[:, 256:384] + pltpu.roll(r[:, 384:512], n - 1, 0) + b[:, 128:256]
    kc_ref[0] = kc.astype(BF16)
    vct_ref[0] = vc.T.astype(BF16)


def _compress(kcsrc, w_all, bias, b, t):
    n_sub = t // CMP_STRIDE
    sub = kcsrc.reshape(b, n_sub, CMP_STRIDE * 256)
    return pl.pallas_call(
        _cmp_body,
        grid=(b,),
        in_specs=[pl.BlockSpec((1, n_sub, CMP_STRIDE * 256), lambda i: (i, 0, 0)),
                  pl.BlockSpec((CMP_STRIDE * 256, 512), lambda i: (0, 0)),
                  pl.BlockSpec((1, 256), lambda i: (0, 0))],
        out_specs=[pl.BlockSpec((1, n_sub, LANE), lambda i: (i, 0, 0)),
                   pl.BlockSpec((1, LANE, n_sub), lambda i: (i, 0, 0))],
        out_shape=[jax.ShapeDtypeStruct((b, n_sub, LANE), BF16), jax.ShapeDtypeStruct((b, LANE, n_sub), BF16)],
        compiler_params=_cparams(("arbitrary",)),
        name="compress",
    )(sub, w_all, bias)


def _slc_cmp_matrix(n_slc, n_cmp, n_cmp_pad):
    ratio = SLC_BLOCK // CMP_STRIDE
    span = CMP_BLOCK // CMP_STRIDE
    m = np.zeros((n_slc, n_cmp_pad), np.float32)
    for j in range(n_slc):
        for k in range(ratio + span - 1):
            n = ratio * j - (span - 1) + k
            if 0 <= n < n_cmp:
                m[j, n] = 1.0
    return m


def _nsa_body(n_cmp_pad, n_slc, q_ref, gt_ref, cos_ref, sin_ref, kc_ref, vct_ref, ks_ref, vst_ref,
              kw_ref, vwt_ref, mt_ref, o_ref, sc_ref, sel_ref, ml_ref, acc_ref, s_ref, p_ref):
    i = pl.program_id(1)
    q0 = i * Q_BLOCK
    hq = NSA_HPG * Q_BLOCK
    q = q_ref[0]
    cos4 = jnp.concatenate([cos_ref[...]] * NSA_HPG, axis=1)
    sin4 = jnp.concatenate([sin_ref[...]] * NSA_HPG, axis=1)
    gs = jax.nn.sigmoid(gt_ref[0])
    lane = lax.broadcasted_iota(jnp.int32, (Q_BLOCK, Q_BLOCK), 1)
    sub = lax.broadcasted_iota(jnp.int32, (Q_BLOCK, Q_BLOCK), 0)
    sub_k = lax.broadcasted_iota(jnp.int32, (KEY_TILE, Q_BLOCK), 0)
    qp_k = q0 + lax.broadcasted_iota(jnp.int32, (KEY_TILE, Q_BLOCK), 1)
    bpt = KEY_TILE // SLC_BLOCK
    zeros_g = jnp.zeros((HEAD_DIM, hq), F32)
    half = ROT_DIM // 2

    for g in range(NSA_KV):
        xa = q[:, g * 256:g * 256 + 128].T
        xb = q[:, g * 256 + 128:g * 256 + 256].T
        qn = jnp.concatenate([xa[0:64], xa[64:128], xb[0:64], xb[64:128]], axis=1) * (HEAD_DIM ** -0.5 * LOG2E)
        x1, x2 = qn[0:half], qn[half:ROT_DIM]
        qr = jnp.concatenate([x1 * cos4 - x2 * sin4, x2 * cos4 + x1 * sin4, qn[ROT_DIM:]], axis=0)

        def pad(a):
            parts = [zeros_g] * NSA_KV
            parts[g] = a
            return jnp.concatenate(parts, axis=0).astype(BF16)

        qn_p, qr_p = pad(qn), pad(qr)

        def compressed(n_rows):
            s1 = jnp.dot(kc_ref[0, 0:n_rows, :], qn_p, preferred_element_type=F32)
            n_idx = lax.broadcasted_iota(jnp.int32, (n_rows, Q_BLOCK), 0)
            lane_c = lax.broadcasted_iota(jnp.int32, (n_rows, Q_BLOCK), 1)
            cmp_ok = (CMP_STRIDE * n_idx + CMP_BLOCK - 1) <= (q0 + lane_c)
            ps = []
            for h in range(NSA_HPG):
                s = jnp.where(cmp_ok, s1[:, h * Q_BLOCK:(h + 1) * Q_BLOCK], NEG)
                m = jnp.max(s, axis=0, keepdims=True)
                e = jnp.where(cmp_ok, jnp.exp2(s - m), 0.0)
                d = jnp.maximum(jnp.sum(e, axis=0, keepdims=True), 1e-30)
                ps.append(e / d)
            imp = ps[0] + ps[1] + ps[2] + ps[3]
            p1 = jnp.concatenate(ps, axis=1).astype(BF16)
            o_c = jnp.dot(vct_ref[0, g * HEAD_DIM:(g + 1) * HEAD_DIM, 0:n_rows], p1, preferred_element_type=F32)
            return o_c, _mm_ones_rhs(mt_ref[:, 0:n_rows], imp)

        prefixes = sorted({n_cmp_pad * k // 4 for k in (1, 2, 3, 4) if (n_cmp_pad * k // 4) % LANE == 0})

        def pick(cands):
            if len(cands) == 1:
                return compressed(cands[0])
            return lax.cond(CMP_STRIDE * cands[0] + CMP_BLOCK - 1 > q0 + Q_BLOCK - 1,
                            lambda: compressed(cands[0]), lambda: pick(cands[1:]))

        o_cmp, imps = pick(prefixes)

        blk = lax.broadcasted_iota(jnp.int32, (n_slc, Q_BLOCK), 0)
        qp_s = q0 + lax.broadcasted_iota(jnp.int32, (n_slc, Q_BLOCK), 1)
        causal = blk * SLC_BLOCK <= qp_s
        qblk = qp_s // SLC_BLOCK
        forced = (blk == 0) | (blk == qblk) | (blk == qblk - 1)
        score = jnp.where(causal, jnp.where(forced, jnp.inf, imps), -jnp.inf)
        sc_ref[...] = score

        def rank_step(jh, cnt):
            for j in (2 * jh, 2 * jh + 1):
                row = sc_ref[pl.ds(j, 1), :]
                ge = jnp.where(row >= score, 1.0, 0.0)
                gt = jnp.where(row > score, 1.0, 0.0)
                cnt = cnt + jnp.where(blk > j, ge, gt)
            return cnt

        cnt = lax.fori_loop(0, jnp.minimum(i + 1, n_slc // 2), rank_step, jnp.zeros((n_slc, Q_BLOCK), F32))
        sel_ref[...] = jnp.where(cnt < N_SELECT, jnp.where(score > -jnp.inf, 0.0, NEG), NEG)

        init = (jnp.full((1, hq), NEG, F32), jnp.zeros((1, hq), F32), jnp.zeros((HEAD_DIM, hq), F32))

        def scores(k_ref, row0, n_rows):
            return jnp.dot(k_ref[0, pl.ds(pl.multiple_of(row0, KEY_TILE), n_rows), :], qr_p,
                           preferred_element_type=F32)

        def softmax_tile(s, bias, m, l):
            ps, alphas, ms, ls = [], [], [], []
            for h in range(NSA_HPG):
                hs = slice(h * Q_BLOCK, (h + 1) * Q_BLOCK)
                sh = s[:, hs] + bias
                m_new = jnp.maximum(m[:, hs], jnp.max(sh, axis=0, keepdims=True))
                alpha = jnp.exp2(m[:, hs] - m_new)
                p = jnp.exp2(sh - m_new)
                ms.append(m_new)
                ls.append(alpha * l[:, hs] + jnp.sum(p, axis=0, keepdims=True))
                ps.append(p.astype(BF16))
                alphas.append(alpha)
            cat = lambda parts: jnp.concatenate(parts, axis=1)
            return cat(ms), cat(ls), cat(alphas), cat(ps)

        def values_t(vt_ref, t):
            return vt_ref[0, g * HEAD_DIM:(g + 1) * HEAD_DIM, pl.ds(pl.multiple_of(t * KEY_TILE, KEY_TILE), KEY_TILE)]

        n_pairs = (q0 + Q_BLOCK + 2 * KEY_TILE - 1) // (2 * KEY_TILE)
        last = 2 * n_pairs - 1

        def slc_tile(u, slot):
            s_next = scores(ks_ref, jnp.minimum(u + 1, last) * KEY_TILE, KEY_TILE)
            pv = jnp.dot(values_t(vst_ref, jnp.maximum(u - 1, 0)), p_ref[1 - slot], preferred_element_type=F32)
            acc_ref[...] = acc_ref[...] * ml_ref[2:3, :] + jnp.where(u > 0, pv, 0.0)
            rows = [jnp.broadcast_to(sel_ref[pl.ds(bpt * u + jj, 1), :], (SLC_BLOCK, Q_BLOCK)) for jj in range(bpt)]
            bias = jnp.where(u * KEY_TILE + sub_k <= qp_k, jnp.concatenate(rows, axis=0), NEG)
            m, l, alpha, p = softmax_tile(s_ref[slot], bias, ml_ref[0:1, :], ml_ref[1:2, :])
            ml_ref[0:1, :], ml_ref[1:2, :], ml_ref[2:3, :] = m, l, alpha
            p_ref[slot] = p
            s_ref[1 - slot] = s_next

        def slc_step(k, carry):
            slc_tile(2 * k, 0)
            slc_tile(2 * k + 1, 1)
            return carry

        ml_ref[0:1, :], ml_ref[1:2, :], acc_ref[...] = init
        ml_ref[2:3, :] = jnp.ones((1, hq), F32)
        s_ref[0] = scores(ks_ref, 0, KEY_TILE)
        lax.fori_loop(0, n_pairs, slc_step, 0)
        pv = jnp.dot(values_t(vst_ref, last), p_ref[1], preferred_element_type=F32)
        o_slc = (acc_ref[...] * ml_ref[2:3, :] + pv) / ml_ref[1:2, :]

        w_rows = WINDOW + Q_BLOCK
        w0 = pl.multiple_of(jnp.maximum(q0 - WINDOW, 0), Q_BLOCK)
        kp = w0 + lax.broadcasted_iota(jnp.int32, (w_rows, Q_BLOCK), 0)
        qp_w = q0 + lax.broadcasted_iota(jnp.int32, (w_rows, Q_BLOCK), 1)
        bias_w = jnp.where(jnp.where(kp <= qp_w, qp_w - kp, WINDOW) < WINDOW, 0.0, NEG)
        s_w = jnp.dot(kw_ref[0, pl.ds(w0, w_rows), :], qr_p, preferred_element_type=F32)
        _, l_w, _, p_w = softmax_tile(s_w, bias_w, init[0], init[1])
        o_win = jnp.dot(vwt_ref[0, g * HEAD_DIM:(g + 1) * HEAD_DIM, pl.ds(w0, w_rows)], p_w,
                        preferred_element_type=F32) / l_w

        def gate(jj):
            return jnp.concatenate([gs[(g * NSA_HPG + h) * 3 + jj:(g * NSA_HPG + h) * 3 + jj + 1, :]
                                    for h in range(NSA_HPG)], axis=1)

        o_t = gate(0) * o_cmp + gate(1) * o_slc + gate(2) * o_win
        ya = jnp.concatenate([o_t[:, 0:128], o_t[:, 128:256]], axis=0).T
        yb = jnp.concatenate([o_t[:, 256:384], o_t[:, 384:512]], axis=0).T
        o_ref[0, :, g * 256:g * 256 + 128] = ya
        o_ref[0, :, g * 256 + 128:g * 256 + 256] = yb


def _nsa_prompt(proj_a, gate_t, cos_t, sin_t, kc, vct, ksb, vst, kwb, vwt, b, t):
    nb = t // Q_BLOCK
    n_sub = t // CMP_STRIDE
    n_slc = t // SLC_BLOCK
    mt = jnp.asarray(_slc_cmp_matrix(n_slc, n_sub - 1, n_sub)).astype(BF16)
    seq = lambda w: pl.BlockSpec((1, t, w), lambda bi, i: (bi, 0, 0))
    seq_t = pl.BlockSpec((1, LANE, t), lambda bi, i: (bi, 0, 0))
    return pl.pallas_call(
        functools.partial(_nsa_body, n_sub, n_slc),
        grid=(b, nb),
        in_specs=[pl.BlockSpec((1, Q_BLOCK, NSA_WIDTH), lambda bi, i: (bi, i, A_Q // NSA_WIDTH)),
                  pl.BlockSpec((1, 3 * NSA_HEADS, Q_BLOCK), lambda bi, i: (bi, 0, i)),
                  pl.BlockSpec((ROT_DIM // 2, Q_BLOCK), lambda bi, i: (0, i)),
                  pl.BlockSpec((ROT_DIM // 2, Q_BLOCK), lambda bi, i: (0, i)),
                  pl.BlockSpec((1, n_sub, LANE), lambda bi, i: (bi, 0, 0)),
                  pl.BlockSpec((1, LANE, n_sub), lambda bi, i: (bi, 0, 0)),
                  seq(LANE), seq_t, seq(LANE), seq_t,
                  pl.BlockSpec((n_slc, n_sub), lambda bi, i: (0, 0))],
        out_specs=pl.BlockSpec((1, Q_BLOCK, NSA_WIDTH), lambda bi, i: (bi, i, 0)),
        out_shape=jax.ShapeDtypeStruct((b, t, NSA_WIDTH), F32),
        scratch_shapes=[pltpu.VMEM((n_slc, Q_BLOCK), F32), pltpu.VMEM((n_slc, Q_BLOCK), F32),
                        pltpu.VMEM((8, NSA_HPG * Q_BLOCK), F32), pltpu.VMEM((HEAD_DIM, NSA_HPG * Q_BLOCK), F32),
                        pltpu.VMEM((2, KEY_TILE, NSA_HPG * Q_BLOCK), F32),
                        pltpu.VMEM((2, KEY_TILE, NSA_HPG * Q_BLOCK), BF16)],
        compiler_params=_cparams(("arbitrary", "arbitrary")),
        name="nsa_prompt",
    )(proj_a.reshape(b, t, A_WIDTH), gate_t, cos_t, sin_t, kc, vct,
      ksb.reshape(b, t, LANE), vst, kwb.reshape(b, t, LANE), vwt, mt)


def _rope_rows(x, pos):
    half = ROT_DIM // 2
    inv = ROPE_THETA ** (-jnp.arange(0, ROT_DIM, 2, dtype=F32) / ROT_DIM)
    ang = pos.astype(F32)[:, None] * inv[None, :]
    cos = jnp.cos(ang)[None, :, None, :]
    sin = jnp.sin(ang)[None, :, None, :]
    x1, x2 = x[..., :half], x[..., half:ROT_DIM]
    return jnp.concatenate([x1 * cos - x2 * sin, x2 * cos + x1 * sin, x[..., ROT_DIM:]], axis=-1)


N_PAGES = PAST_LEN // PAGE_SIZE
DEC_ROWS = 8
DEC_QROWS = NSA_HEADS * 4


DEC_SEQS = 2


def _nsa_dec_body(dec_t, pt_ref, *refs):
    del pt_ref
    for sq in range(DEC_SEQS):
        _nsa_dec_one(dec_t, sq, refs[sq * N_PAGES:(sq + 1) * N_PAGES], *refs[DEC_SEQS * N_PAGES:])


def _nsa_dec_one(dec_t, sq, pages, kvn_ref, cw_ref, wn_ref, qn_ref, qr_ref, g_ref, w_ref, b_ref, mt_ref, o_ref,
                 ck_ref, cv_ref):
    qn, qr = qn_ref[sq], qr_ref[sq]
    nq = qn.shape[0]
    lane = lax.broadcasted_iota(jnp.int32, (nq, LANE), 1)
    qi = lax.broadcasted_iota(jnp.int32, (nq, LANE), 0) % dec_t
    qpos = PAST_LEN + qi
    n_sub = PAST_LEN // CMP_STRIDE

    for p in range(N_PAGES):
        ck_ref[sq, p * PAGE_SIZE:(p + 1) * PAGE_SIZE, :] = pages[p][0, 0, :, 0:128]
        cv_ref[sq, p * PAGE_SIZE:(p + 1) * PAGE_SIZE, :] = pages[p][0, 0, :, 128:256]
    parts = []
    for j in range(CMP_STRIDE):
        parts += [ck_ref[sq, pl.ds(j, n_sub, stride=CMP_STRIDE), :], cv_ref[sq, pl.ds(j, n_sub, stride=CMP_STRIDE), :]]
    r = _mm(jnp.concatenate(parts, axis=1), w_ref[...])
    bias = b_ref[...]
    kc = r[:, 0:128] + pltpu.roll(r[:, 128:256], n_sub - 1, 0) + bias[:, 0:128]
    vc = r[:, 256:384] + pltpu.roll(r[:, 384:512], n_sub - 1, 0) + bias[:, 128:256]

    def softmax_rows(scores, oks):
        scores = [jnp.where(ok, s, NEG) for s, ok in zip(scores, oks)]
        m = functools.reduce(jnp.maximum, [jnp.max(s, axis=1, keepdims=True) for s in scores])
        es = [jnp.where(ok, jnp.exp(s - m), 0.0) for s, ok in zip(scores, oks)]
        d = jnp.maximum(functools.reduce(jnp.add, [jnp.sum(e, axis=1, keepdims=True) for e in es]), 1e-30)
        return [e / d for e in es]

    cmp_ok = (CMP_STRIDE * lane + CMP_BLOCK - 1) <= qpos
    (p1,) = softmax_rows([_mm_nt(qn, kc)], [cmp_ok])
    o_cmp = _mm(p1, vc)
    gq = NSA_KV * dec_t
    imp = functools.reduce(jnp.add, [p1[h * gq:(h + 1) * gq] for h in range(NSA_HPG)])
    imps = _mm_hi(imp, mt_ref[...])
    blk = lane[0:gq]
    qp8 = qpos[0:gq]
    qblk = qp8 // SLC_BLOCK
    causal = blk * SLC_BLOCK <= qp8
    forced = (blk == 0) | (blk == qblk) | (blk == qblk - 1)
    score = jnp.where(causal, jnp.where(forced, jnp.inf, imps), -jnp.inf)
    n_slc = -(-(PAST_LEN + dec_t) // SLC_BLOCK)
    cnt = jnp.zeros((gq, LANE), F32)
    for j in range(n_slc):
        col = score[:, j:j + 1]
        cnt = cnt + jnp.where(blk > j, jnp.where(col >= score, 1.0, 0.0), jnp.where(col > score, 1.0, 0.0))
    sel = jnp.where(cnt < N_SELECT, jnp.where(score > -jnp.inf, 1.0, 0.0), 0.0)

    def sel_rows(j0):
        pick = jnp.where(lane[0:gq] < SLC_BLOCK, sel[:, j0:j0 + 1], sel[:, j0 + 1:j0 + 2])
        return jnp.concatenate([pick] * NSA_HPG, axis=0) > 0.5

    zeros_k = jnp.zeros((PAGE_SIZE - DEC_ROWS, LANE), F32)
    new_ok = (lane < dec_t) & (lane <= qi)
    kvn = kvn_ref[sq]
    s_list = [_mm_nt(qr, pages[p][0, 0, :, 256:384]) for p in range(N_PAGES)]
    s_list.append(_mm_nt(qr, jnp.concatenate([kvn[:, 256:384], zeros_k], axis=0)))
    ok_list = [sel_rows(2 * p) for p in range(N_PAGES)]
    own = jnp.concatenate([sel[:, 2 * N_PAGES:2 * N_PAGES + 1]] * NSA_HPG, axis=0) > 0.5
    ok_list.append(new_ok & own)
    p_list = softmax_rows(s_list, ok_list)
    o_slc = functools.reduce(jnp.add, [_mm(p_list[p], pages[p][0, 0, :, 384:512]) for p in range(N_PAGES)])
    o_slc = o_slc + _mm(p_list[N_PAGES], jnp.concatenate([kvn[:, 384:512], zeros_k], axis=0))

    n_wt = WINDOW // PAGE_SIZE
    wn = wn_ref[sq]
    s_list, ok_list = [], []
    for t in range(n_wt):
        s_list.append(_mm_nt(qr, cw_ref[0, sq, t * PAGE_SIZE:(t + 1) * PAGE_SIZE, 0:128]))
        ok_list.append(qpos - (PAST_LEN - WINDOW + t * PAGE_SIZE + lane) < WINDOW)
    s_list.append(_mm_nt(qr, jnp.concatenate([wn[:, 0:128], zeros_k], axis=0)))
    ok_list.append(new_ok)
    p_list = softmax_rows(s_list, ok_list)
    o_win = functools.reduce(jnp.add, [_mm(p_list[t], cw_ref[0, sq, t * PAGE_SIZE:(t + 1) * PAGE_SIZE, 128:256])
                                       for t in range(n_wt)])
    o_win = o_win + _mm(p_list[n_wt], jnp.concatenate([wn[:, 128:256], zeros_k], axis=0))
    g = g_ref[sq]
    o_ref[sq] = g[:, 0:128] * o_cmp + g[:, 128:256] * o_slc + g[:, 256:384] * o_win


def _nsa_decode_pallas(l, q, gates, cache_kv, page_table, kv_new, cache_win, win_new, pe, cw):
    b, tq = q.shape[:2]
    scale = HEAD_DIM ** -0.5
    eye = jnp.eye(NSA_KV, dtype=F32)

    def rows(x):
        x5 = x.reshape(b, tq, NSA_KV, NSA_HPG, HEAD_DIM).transpose(0, 3, 2, 1, 4)
        return jnp.einsum('bhgqd,gk->bhgqkd', x5, eye).reshape(b, DEC_QROWS, LANE).astype(BF16)

    qn = rows(q * scale)
    qr = rows(_rope_rows(q, PAST_LEN + jnp.arange(tq)) * scale)
    g5 = gates.reshape(b, tq, NSA_KV, NSA_HPG, 3).transpose(0, 3, 2, 1, 4)
    lane_g = jnp.repeat(eye, HEAD_DIM, axis=1)
    gate_b = jnp.einsum('bhgqj,gn->bhgqjn', g5, lane_g).reshape(b, DEC_QROWS, 3 * LANE)
    w_all, bias = _cmp_weights(pe, cw)
    n_cmp = PAST_LEN // CMP_STRIDE - 1
    n_slc = -(-(PAST_LEN + tq) // SLC_BLOCK)
    mt = np.zeros((LANE, LANE), np.float32)
    mt[:n_cmp, :n_slc] = _slc_cmp_matrix(n_slc, n_cmp, n_cmp).T
    pad_rows = lambda a: jnp.pad(a, ((0, 0), (0, DEC_ROWS - tq), (0, 0)))
    n_pool = cache_kv.shape[1]
    ckv = cache_kv.reshape(DEPTH, n_pool, PAGE_SIZE, 512)
    page_spec = lambda sq, p: pl.BlockSpec((1, 1, PAGE_SIZE, 512),
                                           lambda bi, pt: (l, pt[bi * DEC_SEQS + sq, p], 0, 0))
    per_seq = lambda r, w: pl.BlockSpec((DEC_SEQS, r, w), lambda bi, pt: (bi, 0, 0))
    const = lambda shape: pl.BlockSpec(shape, lambda bi, pt: (0,) * len(shape))
    grid_spec = pltpu.PrefetchScalarGridSpec(
        num_scalar_prefetch=1,
        grid=(b // DEC_SEQS,),
        in_specs=[page_spec(sq, p) for sq in range(DEC_SEQS) for p in range(N_PAGES)]
                 + [per_seq(DEC_ROWS, 512),
                    pl.BlockSpec((1, DEC_SEQS, WINDOW, 256), lambda bi, pt: (l, bi, 0, 0)),
                    per_seq(DEC_ROWS, 256), per_seq(DEC_QROWS, LANE), per_seq(DEC_QROWS, LANE),
                    per_seq(DEC_QROWS, 3 * LANE), const((CMP_STRIDE * 256, 512)), const((1, 256)), const((LANE, LANE))],
        out_specs=per_seq(DEC_QROWS, LANE),
        scratch_shapes=[pltpu.VMEM((DEC_SEQS, PAST_LEN, LANE), F32), pltpu.VMEM((DEC_SEQS, PAST_LEN, LANE), F32)],
    )
    o = pl.pallas_call(
        functools.partial(_nsa_dec_body, tq),
        grid_spec=grid_spec,
        out_shape=jax.ShapeDtypeStruct((b, DEC_QROWS, LANE), F32),
        compiler_params=_cparams(("arbitrary",)),
        name="nsa_decode",
    )(page_table, *([ckv] * (DEC_SEQS * N_PAGES)), pad_rows(kv_new), cache_win.reshape(DEPTH, b, WINDOW, 256), pad_rows(win_new),
      qn, qr, gate_b, w_all, bias, jnp.asarray(mt))
    o6 = o.reshape(b, NSA_HPG, NSA_KV, tq, NSA_KV, HEAD_DIM)
    return jnp.einsum('bhgqkd,gk->bqghd', o6, eye).reshape(b, tq, NSA_WIDTH)


def _mm(a, b):
    return jnp.dot(a.astype(BF16), b.astype(BF16), preferred_element_type=F32)


def _mm_nt(a, b):
    return lax.dot_general(a.astype(BF16), b.astype(BF16), (((1,), (1,)), ((), ())), preferred_element_type=F32)


def _mm_tn(a, b):
    return lax.dot_general(a.astype(BF16), b.astype(BF16), (((0,), (0,)), ((), ())), preferred_element_type=F32)


def _mm_hi(a, b):
    return jnp.dot(a, b, preferred_element_type=F32, precision=HI)


def _split3(x):
    hi = x.astype(BF16)
    r1 = x - hi.astype(F32)
    mid = r1.astype(BF16)
    return hi, mid, (r1 - mid.astype(F32)).astype(BF16)


def _cumsum_rows(tri_bf16, x):
    return functools.reduce(jnp.add, [jnp.dot(tri_bf16, piece, preferred_element_type=F32) for piece in _split3(x)])


def _mm_x3(a, b):
    a_hi, b_hi = a.astype(BF16), b.astype(BF16)
    a_lo, b_lo = (a - a_hi.astype(F32)).astype(BF16), (b - b_hi.astype(F32)).astype(BF16)
    dot = functools.partial(jnp.dot, preferred_element_type=F32)
    return dot(a_hi, b_hi) + dot(a_hi, b_lo) + dot(a_lo, b_hi)


def _mm_ones_rhs(ones_bf16, b):
    hi = b.astype(BF16)
    lo = (b - hi.astype(F32)).astype(BF16)
    return (jnp.dot(ones_bf16, hi, preferred_element_type=F32) + jnp.dot(ones_bf16, lo, preferred_element_type=F32))


def _mm_ones(a, ones_bf16):
    hi = a.astype(BF16)
    lo = (a - hi.astype(F32)).astype(BF16)
    return (jnp.dot(hi, ones_bf16, preferred_element_type=F32) + jnp.dot(lo, ones_bf16, preferred_element_type=F32))


def _log_sigmoid(x):
    return jnp.minimum(x, 0.0) - jnp.log1p(jnp.exp(-jnp.abs(x)))


DEC_PAD = 16
HGRN_SUB = 16
MIX_TILE = 512


def _hgrn_body(n_t, c, sc, n_valid, q_ref, f_ref, i_ref, z_ref, s0_ref, lbc_ref, lnw_ref, y_ref, s_ref, st_ref, o_ref):
    ti = pl.program_id(1)
    tt = o_ref.shape[0]
    heads = range(HGRN_HEADS)
    hsl = [slice(h * HGRN_HEAD, (h + 1) * HGRN_HEAD) for h in heads]

    @pl.when(ti == 0)
    def _():
        for h in heads:
            st_ref[h] = s0_ref[0, h].T

    log_lb, log_1m_lb, one_m_lb = lbc_ref[0:1, :], lbc_ref[1:2, :], lbc_ref[2:3, :]
    tri = (lax.broadcasted_iota(jnp.int32, (c, c), 0) >= lax.broadcasted_iota(jnp.int32, (c, c), 1)).astype(BF16)
    sub_s = lax.broadcasted_iota(jnp.int32, (sc, HGRN_WIDTH), 0)

    def chunk(ci, carry):
        r0 = pl.multiple_of(ci * c, c)
        q = q_ref[0, pl.ds(r0, c), :]
        fz = f_ref[0, pl.ds(r0, c), :]
        iv = i_ref[0, pl.ds(r0, c), :]
        b_ = log_1m_lb + _log_sigmoid(fz)
        lf = jnp.maximum(log_lb, b_) + jnp.log1p(jnp.exp(-jnp.abs(log_lb - b_)))
        kf = one_m_lb * jax.nn.sigmoid(-fz)
        if n_valid < tt:
            pad_row = lax.broadcasted_iota(jnp.int32, (c, 1), 0) + r0 >= n_valid
            lf, kf = jnp.where(pad_row, 0.0, lf), jnp.where(pad_row, 0.0, kf)
        cl = _cumsum_rows(tri, lf)
        c_end = cl[c - 1:c]
        q_dec = q * jnp.exp(cl)
        k_end = kf * jnp.exp(c_end - cl)
        w_end = jnp.exp(c_end)
        st = [st_ref[h] for h in heads]
        o = [_mm_nt(q_dec[:, hsl[h]], st[h]) for h in heads]
        upd = [_mm_tn(iv[:, hsl[h]], k_end[:, hsl[h]]) for h in heads]
        for h in heads:
            st_ref[h] = st[h] * w_end[:, hsl[h]] + upd[h]
        pieces = []
        for blk in range(c // sc):
            lo = blk * sc
            cl_r, q_r, k_r, i_r = cl[lo:lo + sc], q[lo:lo + sc], kf[lo:lo + sc], iv[lo:lo + sc]
            acc = [o[h][lo:lo + sc] for h in heads]
            if blk > 0:
                bnd = cl[lo - 1:lo]
                q_b = q_r * jnp.exp(cl_r - bnd)
                k_b = kf[:lo] * jnp.exp(bnd - cl[:lo])
                att = [_mm_nt(q_b[:, hsl[h]], k_b[:, hsl[h]]) for h in heads]
                acc = [acc[h] + _mm(att[h], iv[:lo, hsl[h]]) for h in heads]
            rows = []
            for t in range(sc):
                w = jnp.where(sub_s <= t, jnp.exp(cl_r[t:t + 1] - cl_r), 0.0)
                e = q_r[t:t + 1] * w * k_r
                d = jnp.concatenate([jnp.broadcast_to(jnp.sum(e[:, hsl[h]], axis=1, keepdims=True), (sc, HGRN_HEAD))
                                     for h in heads], axis=1)
                rows.append(jnp.sum(d * i_r, axis=0, keepdims=True))
            pieces.append(jnp.concatenate(acc, axis=1) + jnp.concatenate(rows, axis=0))
        o_ref[pl.ds(r0, c), :] = jnp.concatenate(pieces, axis=0)
        return carry

    lax.fori_loop(0, o_ref.shape[0] // c, chunk, 0)
    z = z_ref[0]
    gate = lnw_ref[...] * (z * jax.nn.sigmoid(z))
    for h in heads:
        o = o_ref[:, hsl[h]]
        y_ref[0, :, hsl[h]] = o * lax.rsqrt(jnp.mean(o * o, axis=-1, keepdims=True) + RMS_EPS) * gate[:, hsl[h]]

    @pl.when(ti == n_t - 1)
    def _():
        for h in heads:
            s_ref[0, h] = st_ref[h].T


def _hgrn_call(arrs, blks, s0, n_valid, c, sc, lb, ln_w):
    b, t = arrs[0].shape[:2]
    tt = min(MIX_TILE, t)
    n_t = t // tt
    lbc = jnp.concatenate([jnp.log(lb)[None], jnp.log1p(-lb)[None], (1.0 - lb)[None],
                           jnp.zeros((5, HGRN_WIDTH), F32)], axis=0)
    col = lambda blk: pl.BlockSpec((1, tt, HGRN_WIDTH), lambda bi, i: (bi, i, blk))
    return pl.pallas_call(
        functools.partial(_hgrn_body, n_t, c, sc, n_valid),
        grid=(b, n_t),
        in_specs=[col(blks[0]), col(blks[1]), col(blks[2]), col(blks[3]),
                  pl.BlockSpec((1, HGRN_HEADS, HGRN_HEAD, HGRN_HEAD), lambda bi, i: (bi, 0, 0, 0)),
                  pl.BlockSpec((8, HGRN_WIDTH), lambda bi, i: (0, 0)),
                  pl.BlockSpec((1, HGRN_WIDTH), lambda bi, i: (0, 0))],
        out_specs=[pl.BlockSpec((1, tt, HGRN_WIDTH), lambda bi, i: (bi, i, 0)),
                   pl.BlockSpec((1, HGRN_HEADS, HGRN_HEAD, HGRN_HEAD), lambda bi, i: (bi, 0, 0, 0))],
        out_shape=[jax.ShapeDtypeStruct((b, t, HGRN_WIDTH), F32),
                   jax.ShapeDtypeStruct((b, HGRN_HEADS, HGRN_HEAD, HGRN_HEAD), F32)],
        scratch_shapes=[pltpu.VMEM((HGRN_HEADS, HGRN_HEAD, HGRN_HEAD), F32), pltpu.VMEM((tt, HGRN_WIDTH), F32)],
        compiler_params=_cparams(("arbitrary", "arbitrary")),
        name="hgrn_prompt",
    )(*arrs, s0, lbc, ln_w.reshape(1, HGRN_WIDTH))


def _rwkv_body(n_t, c, n_valid, cols_ref, z_ref, prev0_ref, s0_ref, mu_ref, w0_ref, wup_ref, a0_ref, aup_ref, kk_ref,
               ka_ref, rk_ref, lnw_ref, bd_ref, y_ref, s_ref, prev_ref, st_ref, r_s, k_s, v_s, lw_s, kk_s, a_s, o_s):
    ti = pl.program_id(1)
    tt = cols_ref.shape[1]
    n_pair = RWKV_HEADS // 2

    @pl.when(ti == 0)
    def _():
        prev_ref[0:1, :] = prev0_ref[0]
        st_ref[...] = s0_ref[0]

    x = cols_ref[0]
    first = lax.broadcasted_iota(jnp.int32, (tt, 1), 0) == 0
    x_prev = jnp.where(first, prev_ref[0:1, :], pltpu.roll(x, 1, 0))
    prev_ref[0:1, :] = x[tt - 1:tt, :]
    xx = x + (x_prev - x) * mu_ref[...]
    r, k, v = xx[:, 0:RWKV_WIDTH], xx[:, RWKV_WIDTH:2 * RWKV_WIDTH], xx[:, 2 * RWKV_WIDTH:3 * RWKV_WIDTH]
    lora = xx[:, 3 * RWKV_WIDTH:]
    nx = -(w0_ref[...] + _mm_x3(jnp.tanh(lora), wup_ref[...]))
    w_log = -(jnp.maximum(nx, 0.0) + jnp.log1p(jnp.exp(-jnp.abs(nx)))) - 0.5
    lw_s[...] = -jnp.exp(w_log)
    a = jax.nn.sigmoid(a0_ref[...] + _mm_x3(lora, aup_ref[...]))
    bd = bd_ref[...]
    kk_raw = k * kk_ref[...]
    kk_s[...] = kk_raw / jnp.maximum(jnp.sqrt(_mm_ones(kk_raw * kk_raw, bd)), 1e-12)
    k2 = k * (1.0 + (a - 1.0) * ka_ref[...])
    bonus = _mm_ones(r * k2 * rk_ref[...], bd) * v
    if n_valid < tt:
        pad_row = lax.broadcasted_iota(jnp.int32, (tt, 1), 0) >= n_valid
        lw_s[...] = jnp.where(pad_row, 0.0, lw_s[...])
        kk_s[...] = jnp.where(pad_row, 0.0, kk_s[...])
        r, k2, v = (jnp.where(pad_row, 0.0, u) for u in (r, k2, v))
    r_s[...] = r
    k_s[...] = k2
    v_s[...] = v
    a_s[...] = a

    w2 = 2 * c
    rows = lax.broadcasted_iota(jnp.int32, (c, w2), 0)
    lane_in = lax.broadcasted_iota(jnp.int32, (c, w2), 1) % c
    strict = rows > lane_in
    incl = rows >= lane_in
    eye2 = (rows == lane_in).astype(F32)
    lane_s = lax.broadcasted_iota(jnp.int32, (1, w2), 1)

    def blockdiag_s(zz):
        zz = zz.astype(BF16)
        return jnp.concatenate([jnp.where(lane_s < c, zz, 0), jnp.where(lane_s >= c, zz, 0)], axis=0)

    tri = (lax.broadcasted_iota(jnp.int32, (c, c), 0) >= lax.broadcasted_iota(jnp.int32, (c, c), 1)).astype(BF16)
    lane1 = lax.broadcasted_iota(jnp.int32, (1, LANE), 1)
    m0, m1 = lane1 < RWKV_HEAD, lane1 >= RWKV_HEAD
    bdm = (lax.broadcasted_iota(jnp.int32, (LANE, LANE), 0) // RWKV_HEAD
           == lax.broadcasted_iota(jnp.int32, (LANE, LANE), 1) // RWKV_HEAD)

    def blockdiag(zz):
        zz = zz.astype(BF16)
        return jnp.concatenate([jnp.where(m0, zz, 0), jnp.where(m1, zz, 0)], axis=0)

    def chunk(ci, carry):
        r0 = pl.multiple_of(ci * c, c)
        ds = pl.ds(r0, c)
        lw = lw_s[ds, :]
        cum = _cumsum_rows(tri, lw)
        w_in, w_ex, inv_in = jnp.exp(cum), jnp.exp(cum - lw), jnp.exp(-cum)
        w_end = jnp.exp(cum[c - 1:c])
        kk_c, vv = kk_s[ds, :], v_s[ds, :]
        alpha = -kk_c * w_ex
        beta_h = kk_c * a_s[ds, :] * inv_in
        k_h = k_s[ds, :] * inv_in
        r_t = r_s[ds, :] * w_in
        beta_e, k_e = beta_h * w_end, k_h * w_end
        pairs = range(n_pair)
        sls = [slice(p * LANE, (p + 1) * LANE) for p in pairs]
        al = [alpha[:, sl] for sl in sls]
        rt = [r_t[:, sl] for sl in sls]
        vb = [blockdiag(vv[:, sl]) for sl in sls]
        aa = [_mm_nt(jnp.concatenate([al[p], rt[p]], axis=0),
                     jnp.concatenate([blockdiag(beta_h[:, sls[p]]), blockdiag(k_h[:, sls[p]])], axis=0))
              for p in pairs]
        a_ab = [jnp.where(strict, aa[p][0:c, 0:w2], 0.0) for p in pairs]
        a_ak = [jnp.where(strict, aa[p][0:c, w2:], 0.0) for p in pairs]
        a_rb = [jnp.where(incl, aa[p][c:, 0:w2], 0.0) for p in pairs]
        a_rk = [jnp.where(incl, aa[p][c:, w2:], 0.0) for p in pairs]
        akv = [_mm(a_ak[p], vb[p]) for p in pairs]
        rkv = [_mm(a_rk[p], vb[p]) for p in pairs]
        vk = [_mm_tn(vv[:, sls[p]], k_e[:, sls[p]]) for p in pairs]
        tm = [eye2 + a_ab[p] for p in pairs]
        pw = a_ab
        n = 1
        while 2 * n < c:
            pw = [_mm(pw[p], blockdiag_s(pw[p])) for p in pairs]
            tm = [tm[p] + _mm(tm[p], blockdiag_s(pw[p])) for p in pairs]
            n *= 2
        pq = [_mm(tm[p], jnp.concatenate([blockdiag(al[p]), blockdiag(akv[p])], axis=1)) for p in pairs]
        ro = [_mm(a_rb[p], jnp.concatenate([blockdiag(pq[p][:, 0:LANE]), blockdiag(pq[p][:, LANE:])], axis=1))
              for p in pairs]
        mn = [_mm_tn(pq[p], beta_e[:, sls[p]]) for p in pairs]
        outs = []
        for p in pairs:
            m_bd = jnp.where(bdm, mn[p][0:LANE], 0.0)
            n_f = jnp.where(bdm, mn[p][LANE:] + vk[p], 0.0)
            st = st_ref[p]
            outs.append(_mm_nt(rt[p] + ro[p][:, 0:LANE], blockdiag(st)) + ro[p][:, LANE:] + rkv[p])
            st_ref[p] = st * w_end[:, sls[p]] + _mm(st, m_bd) + n_f[0:RWKV_HEAD] + n_f[RWKV_HEAD:]
        o_s[ds, :] = jnp.concatenate(outs, axis=1)
        return carry

    lax.fori_loop(0, tt // c, chunk, 0)
    o = o_s[...]
    inv_n = 1.0 / RWKV_HEAD
    d = o - _mm_ones(o, bd) * inv_n
    var = _mm_ones(d * d, bd) * inv_n
    z = z_ref[0]
    y_ref[0] = (d * lax.rsqrt(var + GN_EPS) * lnw_ref[...] + bonus) * (z * jax.nn.sigmoid(z))

    @pl.when(ti == n_t - 1)
    def _():
        s_ref[0] = st_ref[...]


def _rwkv_call(rcols, z_arr, z_blk, prev0, s0, n_valid, c, mu, w0, w_up, a0, a_up, k_k, k_a, r_k, ln_w):
    b, t = rcols.shape[:2]
    tt = min(MIX_TILE, t)
    n_t = t // tt
    n_pair = RWKV_HEADS // 2
    s0p = s0.reshape(b, n_pair, 2, RWKV_HEAD, RWKV_HEAD).transpose(0, 1, 3, 2, 4).reshape(b, n_pair, RWKV_HEAD, LANE)
    zpad = jnp.zeros((RWKV_LORA_W, RWKV_WIDTH), F32)
    wup = jnp.concatenate([w_up, zpad], axis=0)
    aup = jnp.concatenate([zpad, a_up], axis=0)
    hid = np.arange(RWKV_WIDTH) // RWKV_HEAD
    bd = jnp.asarray((hid[:, None] == hid[None, :]).astype(np.float32)).astype(BF16)
    vec = lambda a: a.reshape(1, -1)
    full = lambda shape: pl.BlockSpec(shape, lambda bi, i: (0,) * len(shape))
    y, s = pl.pallas_call(
        functools.partial(_rwkv_body, n_t, c, n_valid),
        grid=(b, n_t),
        in_specs=[pl.BlockSpec((1, tt, SHIFT_W), lambda bi, i: (bi, i, 0)),
                  pl.BlockSpec((1, tt, RWKV_WIDTH), lambda bi, i: (bi, i, z_blk)),
                  pl.BlockSpec((1, 1, SHIFT_W), lambda bi, i: (bi, 0, 0)),
                  pl.BlockSpec((1, n_pair, RWKV_HEAD, LANE), lambda bi, i: (bi, 0, 0, 0)),
                  full((1, SHIFT_W)), full((1, RWKV_WIDTH)), full((LANE, RWKV_WIDTH)), full((1, RWKV_WIDTH)),
                  full((LANE, RWKV_WIDTH)), full((1, RWKV_WIDTH)), full((1, RWKV_WIDTH)), full((1, RWKV_WIDTH)),
                  full((1, RWKV_WIDTH)), full((RWKV_WIDTH, RWKV_WIDTH))],
        out_specs=[pl.BlockSpec((1, tt, RWKV_WIDTH), lambda bi, i: (bi, i, 0)),
                   pl.BlockSpec((1, n_pair, RWKV_HEAD, LANE), lambda bi, i: (bi, 0, 0, 0))],
        out_shape=[jax.ShapeDtypeStruct((b, t, RWKV_WIDTH), F32),
                   jax.ShapeDtypeStruct((b, n_pair, RWKV_HEAD, LANE), F32)],
        scratch_shapes=[pltpu.VMEM((8, SHIFT_W), F32), pltpu.VMEM((n_pair, RWKV_HEAD, LANE), F32)]
                       + [pltpu.VMEM((tt, RWKV_WIDTH), F32)] * 7,
        compiler_params=_cparams(("arbitrary", "arbitrary")),
        name="rwkv_prompt",
    )(rcols, z_arr, prev0.reshape(b, 1, SHIFT_W), s0p, vec(mu), vec(w0), wup, vec(a0), aup, vec(k_k), vec(k_a),
      vec(r_k), vec(ln_w), bd)
    s = s.reshape(b, n_pair, RWKV_HEAD, 2, RWKV_HEAD).transpose(0, 1, 3, 2, 4)
    return y, s.reshape(b, RWKV_HEADS, RWKV_HEAD, RWKV_HEAD)


def _merge_body(x_ref, yn_ref, nz_ref, yr_ref, yh_ref, mg_ref, wb_ref, wo_ref, o_ref):
    nz = nz_ref[...]
    branches = (yn_ref[...] * (nz * jax.nn.sigmoid(nz)), yr_ref[...], yh_ref[...])
    acc = jnp.zeros(o_ref.shape, F32)
    for n, y in enumerate(branches):
        t = jnp.dot(y.astype(BF16), wb_ref[n], preferred_element_type=F32)
        acc = acc + jax.nn.sigmoid(mg_ref[:, n * D_MODEL:(n + 1) * D_MODEL]) * t
    o_ref[...] = x_ref[...] + jnp.dot(acc.astype(BF16), wo_ref[...], preferred_element_type=F32)


def _merge(x2d, y_nsa, proj_a, y_rwkv, y_hgrn, mg, wb, wo, tm):
    m = x2d.shape[0]
    row = lambda w: pl.BlockSpec((tm, w), lambda i: (i, 0))
    return pl.pallas_call(
        _merge_body,
        grid=(m // tm,),
        in_specs=[row(D_MODEL), row(BRANCH_WIDTH),
                  pl.BlockSpec((tm, BRANCH_WIDTH), lambda i: (i, A_NZ // BRANCH_WIDTH)),
                  row(BRANCH_WIDTH), row(BRANCH_WIDTH), row(N_BRANCH * D_MODEL),
                  pl.BlockSpec((N_BRANCH, BRANCH_WIDTH, D_MODEL), lambda i: (0, 0, 0)),
                  pl.BlockSpec((D_MODEL, D_MODEL), lambda i: (0, 0))],
        out_specs=row(D_MODEL),
        out_shape=jax.ShapeDtypeStruct((m, D_MODEL), F32),
        compiler_params=_cparams(("arbitrary",)),
        name="merge",
    )(x2d, y_nsa, proj_a, y_rwkv, y_hgrn, mg, wb, wo)


def _split_w_in(w):
    o = np.concatenate([[0], np.cumsum(IN_SIZES)])
    seg = lambda n: w[:, o[n]:o[n + 1]]
    pad = jnp.zeros((w.shape[0], A_Q - A_GATE - IN_SIZES[2]), w.dtype)
    wa = jnp.concatenate([seg(1), seg(2), pad, seg(0), seg(3), seg(5), seg(6), seg(7), seg(8), seg(9)], axis=1)
    return wa.astype(BF16), seg(4).astype(BF16), seg(10).astype(BF16)


def _layer(x, l, past, prm, lb, tabs, tm):
    b, t = x.shape[:2]
    m = b * t
    x2d = x.reshape(m, D_MODEL)
    nw = prm['norm_w'][l].reshape(1, D_MODEL)
    wa, wb_cols, wc = prm['w_split'][l]
    proj_a = _inproj(x2d, nw, wa, tm, 1536)
    rcols = _inproj(x2d, nw, wb_cols, tm, SHIFT_W).reshape(b, t, SHIFT_W)
    mg = _inproj(x2d, nw, wc, tm, 1536)
    seg = lambda off, w: proj_a[:, off:off + w].reshape(b, t, w)
    c_tab, su_tab, sd_tab, cos_t, sin_t = tabs
    if past is None:
        kv_new, win_new, kcsrc, ksb, vst, kwb, vwt = _kvpost(proj_a, (c_tab, su_tab, sd_tab), b, t, min(tm, t), True)
        w_all, bias = _cmp_weights(prm['nsa_cmp_pe'][l], prm['nsa_cmp_w'][l])
        kc, vct = _compress(kcsrc, w_all, bias, b, t)
        gate_t = seg(A_GATE, 3 * NSA_HEADS).transpose(0, 2, 1)
        y_nsa = _nsa_prompt(proj_a, gate_t, cos_t, sin_t, kc, vct, ksb, vst, kwb, vwt, b, t)
        win_state = win_new.reshape(b, t, 2, NSA_KV, HEAD_DIM)[:, -min(WINDOW, t):]
    else:
        cache_kv, page_table, cache_win, s_r, prev, s_h = past
        kv_new, win_new = _kvpost(proj_a, (c_tab, su_tab, sd_tab), b, t, tm, False)
        win5 = win_new.reshape(b, t, 2, NSA_KV, HEAD_DIM)
        win_state = jnp.concatenate([cache_win[l][:, t:], win5], axis=1)
        gates = jax.nn.sigmoid(seg(A_GATE, 3 * NSA_HEADS)).reshape(b, t, NSA_HEADS, 3)
        q = seg(A_Q, NSA_WIDTH).reshape(b, t, NSA_HEADS, HEAD_DIM)
        y_nsa = _nsa_decode_pallas(l, q, gates, cache_kv, page_table, kv_new.reshape(b, t, 512), cache_win,
                                   win_new.reshape(b, t, 256), prm['nsa_cmp_pe'][l], prm['nsa_cmp_w'][l])
    rwkv_prm = (prm['rwkv_mu'][l], prm['rwkv_w0'][l], prm['rwkv_w_up'][l], prm['rwkv_a0'][l], prm['rwkv_a_up'][l],
                prm['rwkv_k_k'][l], prm['rwkv_k_a'][l], prm['rwkv_r_k'][l], prm['rwkv_ln_w'][l])
    shift_state = rcols[:, -1]
    hgrn_cols = (A_HQ, A_HF, A_HI, A_HZ)
    if past is None:
        proj_a3 = proj_a.reshape(b, t, A_WIDTH)
        y_rwkv, s_r = _rwkv_call(rcols, proj_a3, A_RZ // RWKV_WIDTH, jnp.zeros((b, SHIFT_W), F32),
                                 jnp.zeros((b, RWKV_HEADS, RWKV_HEAD, RWKV_HEAD), F32), t, RWKV_CHUNK, *rwkv_prm)
        y_hgrn, s_h = _hgrn_call([proj_a3] * 4, [off // HGRN_WIDTH for off in hgrn_cols],
                                 jnp.zeros((b, HGRN_HEADS, HGRN_HEAD, HGRN_HEAD), F32), t, HGRN_CHUNK, HGRN_SUB,
                                 lb, prm['hgrn_ln_w'][l])
    else:
        pad = lambda a: jnp.pad(a, ((0, 0), (0, DEC_PAD - t), (0, 0)))
        y_rwkv, s_r = _rwkv_call(pad(rcols), pad(seg(A_RZ, RWKV_WIDTH)), 0, prev, s_r, t, DEC_PAD, *rwkv_prm)
        y_hgrn, s_h = _hgrn_call([pad(seg(off, HGRN_WIDTH)) for off in hgrn_cols], [0] * 4, s_h, t, DEC_PAD, DEC_PAD,
                                 lb, prm['hgrn_ln_w'][l])
        y_rwkv, y_hgrn = y_rwkv[:, :t], y_hgrn[:, :t]
    x_new = _merge(x2d, y_nsa.reshape(m, NSA_WIDTH), proj_a, y_rwkv.reshape(m, RWKV_WIDTH), y_hgrn.reshape(m, HGRN_WIDTH),
                   mg, prm['w_branch_bf16'][l], prm['w_out_bf16'][l], min(tm, 512))
    kv_state = kv_new.reshape(b, t, 4, NSA_KV, HEAD_DIM)
    return x_new.reshape(b, t, D_MODEL), (kv_state, win_state, s_r, shift_state, s_h)


def kernel(x_prompt, x_sample, cache_kv, cache_win, state_rwkv, state_shift, state_hgrn, page_table,
           norm_w, w_in, nsa_cmp_pe, nsa_cmp_w, rwkv_mu, rwkv_w0, rwkv_w_up, rwkv_a0, rwkv_a_up,
           rwkv_k_k, rwkv_k_a, rwkv_r_k, rwkv_ln_w, hgrn_lb_logits, hgrn_ln_w, w_branch, w_out, norm_f):
    prm = {'norm_w': norm_w, 'nsa_cmp_pe': nsa_cmp_pe, 'nsa_cmp_w': nsa_cmp_w,
           'rwkv_mu': rwkv_mu, 'rwkv_w0': rwkv_w0, 'rwkv_w_up': rwkv_w_up, 'rwkv_a0': rwkv_a0,
           'rwkv_a_up': rwkv_a_up, 'rwkv_k_k': rwkv_k_k, 'rwkv_k_a': rwkv_k_a, 'rwkv_r_k': rwkv_r_k,
           'rwkv_ln_w': rwkv_ln_w, 'hgrn_ln_w': hgrn_ln_w,
           'w_split': [_split_w_in(w_in[l]) for l in range(DEPTH)],
           'w_branch_bf16': w_branch.astype(BF16), 'w_out_bf16': w_out.astype(BF16)}
    cs = jnp.cumsum(jax.nn.softmax(hgrn_lb_logits, axis=0), axis=0)
    lbs = cs - cs[0:1]
    n_dec, dec_t = x_sample.shape[:2]
    bp, tp = x_prompt.shape[:2]
    tabs_p = _rope_tables(jnp.arange(tp))
    tabs_s = _rope_tables(PAST_LEN + jnp.arange(n_dec * dec_t) % dec_t)
    tm_p = min(1024, bp * tp)
    tm_s = n_dec * dec_t
    xp, xs = x_prompt, x_sample
    outs = [[] for _ in range(10)]
    for l in range(DEPTH):
        xp, st_p = _layer(xp, l, None, prm, lbs[l], tabs_p, tm_p)
        past = (cache_kv, page_table, cache_win, state_rwkv[l], state_shift[l], state_hgrn[l])
        xs, st_s = _layer(xs, l, past, prm, lbs[l], tabs_s, tm_s)
        for n in range(5):
            outs[2 * n].append(st_p[n])
            outs[2 * n + 1].append(st_s[n])
    y_prompt = _final_norm(xp.reshape(bp * tp, D_MODEL), norm_f, tm_p).reshape(xp.shape)
    y_sample = _final_norm(xs.reshape(n_dec * dec_t, D_MODEL), norm_f, tm_s).reshape(xs.shape)
    return (y_prompt, y_sample) + tuple(jnp.stack(o) for o in outs)
```

```python
import functools

import jax
import jax.numpy as jnp
import numpy as np
from jax import lax
from jax.experimental import pallas as pl
from jax.experimental.pallas import tpu as pltpu

F32 = jnp.float32
BF16 = jnp.bfloat16
HI = lax.Precision.HIGHEST

D_MODEL = 1024
DEPTH = 4
PAST_LEN = 2048
PAGE_SIZE = 128
NSA_HEADS = 8
NSA_KV = 2
HEAD_DIM = 64
NSA_HPG = NSA_HEADS // NSA_KV
NSA_WIDTH = NSA_HEADS * HEAD_DIM
ROT_DIM = HEAD_DIM // 4
ROPE_THETA = 500000.0
CMP_BLOCK = 32
CMP_STRIDE = 16
SLC_BLOCK = 64
N_SELECT = 16
WINDOW = 512
Q_BLOCK = 128
RWKV_HEADS = 8
RWKV_HEAD = 64
RWKV_WIDTH = RWKV_HEADS * RWKV_HEAD
RWKV_LORA_W = 64
RWKV_LORA_A = 64
SHIFT_W = 3 * RWKV_WIDTH + RWKV_LORA_W + RWKV_LORA_A
HGRN_HEADS = 4
HGRN_HEAD = 128
HGRN_WIDTH = HGRN_HEADS * HGRN_HEAD
HGRN_CHUNK = 64
N_BRANCH = 3
BRANCH_WIDTH = 512
KV_COLS = 6 * NSA_KV * HEAD_DIM
IN_SIZES = (NSA_WIDTH, KV_COLS, 3 * NSA_HEADS, NSA_WIDTH, SHIFT_W, RWKV_WIDTH,
            HGRN_WIDTH, HGRN_WIDTH, HGRN_WIDTH, HGRN_WIDTH, N_BRANCH * D_MODEL)
RMS_EPS = 1e-6
GN_EPS = 64e-5

LANE = 128
VMEM_LIMIT = 56 * 1024 * 1024
NEG = -1e30
KEY_TILE = 512
LOG2E = 1.4426950408889634
RWKV_CHUNK = 64

A_KV = 0
A_GATE = 768
A_Q = 1024
A_NZ = 1536
A_RZ = 2048
A_HQ = 2560
A_HF = 3072
A_HI = 3584
A_HZ = 4096
A_WIDTH = 4608


def _cparams(sem):
    return pltpu.CompilerParams(dimension_semantics=sem, vmem_limit_bytes=VMEM_LIMIT)


def _inproj_body(x_ref, nw_ref, w_ref, o_ref, h_ref):
    @pl.when(pl.program_id(1) == 0)
    def _():
        x = x_ref[...]
        ms = jnp.mean(x * x, axis=-1, keepdims=True)
        h_ref[...] = (x * lax.rsqrt(ms + RMS_EPS) * nw_ref[...]).astype(BF16)

    o_ref[...] = jnp.dot(h_ref[...], w_ref[...], preferred_element_type=F32)


def _inproj(x2d, nw, w, tm, tn):
    m, d = x2d.shape
    n = w.shape[1]
    return pl.pallas_call(
        _inproj_body,
        grid=(m // tm, n // tn),
        in_specs=[pl.BlockSpec((tm, d), lambda i, j: (i, 0)),
                  pl.BlockSpec((1, d), lambda i, j: (0, 0)),
                  pl.BlockSpec((d, tn), lambda i, j: (0, j))],
        out_specs=pl.BlockSpec((tm, tn), lambda i, j: (i, j)),
        out_shape=jax.ShapeDtypeStruct((m, n), F32),
        scratch_shapes=[pltpu.VMEM((tm, d), BF16)],
        compiler_params=_cparams(("arbitrary", "arbitrary")),
        name="inproj",
    )(x2d, nw, w)


def _rmsnorm_body(x_ref, w_ref, o_ref):
    x = x_ref[...]
    ms = jnp.mean(x * x, axis=-1, keepdims=True)
    o_ref[...] = x * lax.rsqrt(ms + RMS_EPS) * w_ref[...]


def _final_norm(x2d, w, tm):
    m, d = x2d.shape
    return pl.pallas_call(
        _rmsnorm_body,
        grid=(m // tm,),
        in_specs=[pl.BlockSpec((tm, d), lambda i: (i, 0)), pl.BlockSpec((1, d), lambda i: (0, 0))],
        out_specs=pl.BlockSpec((tm, d), lambda i: (i, 0)),
        out_shape=jax.ShapeDtypeStruct((m, d), F32),
        compiler_params=_cparams(("arbitrary",)),
        name="final_norm",
    )(x2d, w.reshape(1, d))


def _rope_tables(pos):
    half = ROT_DIM // 2
    inv = ROPE_THETA ** (-jnp.arange(0, ROT_DIM, 2, dtype=F32) / ROT_DIM)
    ang = pos.astype(F32)[:, None] * inv[None, :]
    cos, sin = jnp.cos(ang), jnp.sin(ang)
    n = pos.shape[0]
    ones = jnp.ones((n, HEAD_DIM - ROT_DIM), F32)
    zeros8 = jnp.zeros((n, half), F32)
    zeros = jnp.zeros((n, HEAD_DIM - ROT_DIM), F32)
    c = jnp.concatenate([cos, cos, ones], axis=1)
    s_up = jnp.concatenate([zeros8, sin, zeros], axis=1)
    s_dn = jnp.concatenate([-sin, zeros8, zeros], axis=1)
    tile = lambda a: jnp.concatenate([a] * NSA_KV, axis=1)
    return tile(c), tile(s_up), tile(s_dn), cos.T, sin.T


def _kvpost_body(with_attn, p_ref, c_ref, su_ref, sd_ref, kv_ref, win_ref, *extra):
    x = p_ref[...]
    c, su, sd = c_ref[...], su_ref[...], sd_ref[...]

    def rope(v):
        return v * c + pltpu.roll(v, ROT_DIM // 2, 1) * su + pltpu.roll(v, LANE - ROT_DIM // 2, 1) * sd

    k_slc = rope(x[:, 256:384])
    k_win = rope(x[:, 512:640])
    kv_ref[:, 0:256] = x[:, 0:256]
    kv_ref[:, 256:384] = k_slc
    kv_ref[:, 384:512] = x[:, 384:512]
    win_ref[:, 0:128] = k_win
    win_ref[:, 128:256] = x[:, 640:768]
    if with_attn:
        kc_ref, ks_ref, vst_ref, kw_ref, vwt_ref = extra
        kc_ref[...] = x[:, 0:256].astype(BF16)
        ks_ref[...] = k_slc.astype(BF16)
        vst_ref[0] = x[:, 384:512].T.astype(BF16)
        kw_ref[...] = k_win.astype(BF16)
        vwt_ref[0] = x[:, 640:768].T.astype(BF16)


def _kvpost(proj_a, tabs, b, t, tm, with_attn):
    m = b * t
    nt = tabs[0].shape[0] // tm
    tab_spec = pl.BlockSpec((tm, LANE), lambda i: (i % nt, 0))
    out_shape = [jax.ShapeDtypeStruct((m, 512), F32), jax.ShapeDtypeStruct((m, 256), F32)]
    out_specs = [pl.BlockSpec((tm, 512), lambda i: (i, 0)), pl.BlockSpec((tm, 256), lambda i: (i, 0))]
    if with_attn:
        out_shape += [jax.ShapeDtypeStruct((m, 256), BF16), jax.ShapeDtypeStruct((m, LANE), BF16),
                      jax.ShapeDtypeStruct((b, LANE, t), BF16), jax.ShapeDtypeStruct((m, LANE), BF16),
                      jax.ShapeDtypeStruct((b, LANE, t), BF16)]
        tspec = pl.BlockSpec((1, LANE, tm), lambda i: (i // nt, 0, i % nt))
        out_specs += [pl.BlockSpec((tm, 256), lambda i: (i, 0)), pl.BlockSpec((tm, LANE), lambda i: (i, 0)),
                      tspec, pl.BlockSpec((tm, LANE), lambda i: (i, 0)), tspec]
    return pl.pallas_call(
        functools.partial(_kvpost_body, with_attn),
        grid=(m // tm,),
        in_specs=[pl.BlockSpec((tm, KV_COLS), lambda i: (i, 0)), tab_spec, tab_spec, tab_spec],
        out_specs=out_specs,
        out_shape=out_shape,
        compiler_params=_cparams(("arbitrary",)),
        name="kvpost",
    )(proj_a, *tabs)


def _cmp_weights(pe, cw):
    eye = jnp.eye(NSA_KV, dtype=F32)
    cols = []
    for s in range(2):
        for part in range(2):
            w = cw[s, part * CMP_STRIDE:(part + 1) * CMP_STRIDE]
            blk = jnp.einsum('jde,gh->jgdhe', w, eye).reshape(CMP_STRIDE, LANE, LANE)
            full = jnp.zeros((CMP_STRIDE, 2, LANE, LANE), F32).at[:, s].set(blk)
            cols.append(full.reshape(CMP_STRIDE * 2 * LANE, LANE))
    w_all = jnp.concatenate(cols, axis=1).astype(BF16)
    bias = jnp.einsum('sjd,sjde->se', pe, cw)
    bias = jnp.concatenate([bias[0], bias[0], bias[1], bias[1]]).reshape(1, 2 * LANE)
    return w_all, bias


def _cmp_body(sub_ref, w_ref, b_ref, kc_ref, vct_ref):
    r = jnp.dot(sub_ref[0], w_ref[...], preferred_element_type=F32)
    n = r.shape[0]
    b = b_ref[...]
    kc = r[:, 0:128] + pltpu.roll(r[:, 128:256], n - 1, 0) + b[:, 0:128]
    vc = r[:, 256:384] + pltpu.roll(r[:, 384:512], n - 1, 0) + b[:, 128:256]
    kc_ref[0] = kc.astype(BF16)
    vct_ref[0] = vc.T.astype(BF16)


def _compress(kcsrc, w_all, bias, b, t):
    n_sub = t // CMP_STRIDE
    sub = kcsrc.reshape(b, n_sub, CMP_STRIDE * 256)
    return pl.pallas_call(
        _cmp_body,
        grid=(b,),
        in_specs=[pl.BlockSpec((1, n_sub, CMP_STRIDE * 256), lambda i: (i, 0, 0)),
                  pl.BlockSpec((CMP_STRIDE * 256, 512), lambda i: (0, 0)),
                  pl.BlockSpec((1, 256), lambda i: (0, 0))],
        out_specs=[pl.BlockSpec((1, n_sub, LANE), lambda i: (i, 0, 0)),
                   pl.BlockSpec((1, LANE, n_sub), lambda i: (i, 0, 0))],
        out_shape=[jax.ShapeDtypeStruct((b, n_sub, LANE), BF16), jax.ShapeDtypeStruct((b, LANE, n_sub), BF16)],
        compiler_params=_cparams(("arbitrary",)),
        name="compress",
    )(sub, w_all, bias)


def _slc_cmp_matrix(n_slc, n_cmp, n_cmp_pad):
    ratio = SLC_BLOCK // CMP_STRIDE
    span = CMP_BLOCK // CMP_STRIDE
    m = np.zeros((n_slc, n_cmp_pad), np.float32)
    for j in range(n_slc):
        for k in range(ratio + span - 1):
            n = ratio * j - (span - 1) + k
            if 0 <= n < n_cmp:
                m[j, n] = 1.0
    return m


def _nsa_body(n_cmp_pad, n_slc, q_ref, gt_ref, cos_ref, sin_ref, kc_ref, vct_ref, ks_ref, vst_ref,
              kw_ref, vwt_ref, mt_ref, o_ref, sc_ref, sel_ref, ml_ref, acc_ref, s_ref, p_ref):
    i = pl.program_id(1)
    q0 = i * Q_BLOCK
    hq = NSA_HPG * Q_BLOCK
    q = q_ref[0]
    cos4 = jnp.concatenate([cos_ref[...]] * NSA_HPG, axis=1)
    sin4 = jnp.concatenate([sin_ref[...]] * NSA_HPG, axis=1)
    gs = jax.nn.sigmoid(gt_ref[0])
    lane = lax.broadcasted_iota(jnp.int32, (Q_BLOCK, Q_BLOCK), 1)
    sub = lax.broadcasted_iota(jnp.int32, (Q_BLOCK, Q_BLOCK), 0)
    sub_k = lax.broadcasted_iota(jnp.int32, (KEY_TILE, Q_BLOCK), 0)
    qp_k = q0 + lax.broadcasted_iota(jnp.int32, (KEY_TILE, Q_BLOCK), 1)
    bpt = KEY_TILE // SLC_BLOCK
    zeros_g = jnp.zeros((HEAD_DIM, hq), F32)
    half = ROT_DIM // 2

    for g in range(NSA_KV):
        xa = q[:, g * 256:g * 256 + 128].T
        xb = q[:, g * 256 + 128:g * 256 + 256].T
        qn = jnp.concatenate([xa[0:64], xa[64:128], xb[0:64], xb[64:128]], axis=1) * (HEAD_DIM ** -0.5 * LOG2E)
        x1, x2 = qn[0:half], qn[half:ROT_DIM]
        qr = jnp.concatenate([x1 * cos4 - x2 * sin4, x2 * cos4 + x1 * sin4, qn[ROT_DIM:]], axis=0)

        def pad(a):
            parts = [zeros_g] * NSA_KV
            parts[g] = a
            return jnp.concatenate(parts, axis=0).astype(BF16)

        qn_p, qr_p = pad(qn), pad(qr)

        def compressed(n_rows):
            s1 = jnp.dot(kc_ref[0, 0:n_rows, :], qn_p, preferred_element_type=F32)
            n_idx = lax.broadcasted_iota(jnp.int32, (n_rows, Q_BLOCK), 0)
            lane_c = lax.broadcasted_iota(jnp.int32, (n_rows, Q_BLOCK), 1)
            cmp_ok = (CMP_STRIDE * n_idx + CMP_BLOCK - 1) <= (q0 + lane_c)
            ps = []
            for h in range(NSA_HPG):
                s = jnp.where(cmp_ok, s1[:, h * Q_BLOCK:(h + 1) * Q_BLOCK], NEG)
                m = jnp.max(s, axis=0, keepdims=True)
                e = jnp.where(cmp_ok, jnp.exp2(s - m), 0.0)
                d = jnp.maximum(jnp.sum(e, axis=0, keepdims=True), 1e-30)
                ps.append(e / d)
            imp = ps[0] + ps[1] + ps[2] + ps[3]
            p1 = jnp.concatenate(ps, axis=1).astype(BF16)
            o_c = jnp.dot(vct_ref[0, g * HEAD_DIM:(g + 1) * HEAD_DIM, 0:n_rows], p1, preferred_element_type=F32)
            return o_c, _mm_ones_rhs(mt_ref[:, 0:n_rows], imp)

        prefixes = sorted({n_cmp_pad * k // 4 for k in (1, 2, 3, 4) if (n_cmp_pad * k // 4) % LANE == 0})

        def pick(cands):
            if len(cands) == 1:
                return compressed(cands[0])
            return lax.cond(CMP_STRIDE * cands[0] + CMP_BLOCK - 1 > q0 + Q_BLOCK - 1,
                            lambda: compressed(cands[0]), lambda: pick(cands[1:]))

        o_cmp, imps = pick(prefixes)

        blk = lax.broadcasted_iota(jnp.int32, (n_slc, Q_BLOCK), 0)
        qp_s = q0 + lax.broadcasted_iota(jnp.int32, (n_slc, Q_BLOCK), 1)
        causal = blk * SLC_BLOCK <= qp_s
        qblk = qp_s // SLC_BLOCK
        forced = (blk == 0) | (blk == qblk) | (blk == qblk - 1)
        score = jnp.where(causal, jnp.where(forced, jnp.inf, imps), -jnp.inf)
        sc_ref[...] = score

        def rank_prefix(n_rows):
            sc, bl = score[0:n_rows], blk[0:n_rows]

            def rank_step(jh, cnt):
                for j in (2 * jh, 2 * jh + 1):
                    row = sc_ref[pl.ds(j, 1), :]
                    ge = jnp.where(row >= sc, 1.0, 0.0)
                    gt = jnp.where(row > sc, 1.0, 0.0)
                    cnt = cnt + jnp.where(bl > j, ge, gt)
                return cnt

            cnt = lax.fori_loop(0, jnp.minimum(i + 1, n_slc // 2), rank_step, jnp.zeros((n_rows, Q_BLOCK), F32))
            sel_ref[0:n_rows, :] = jnp.where(cnt < N_SELECT, jnp.where(sc > -jnp.inf, 0.0, NEG), NEG)
            if n_rows < n_slc:
                sel_ref[n_rows:, :] = jnp.full((n_slc - n_rows, Q_BLOCK), NEG, F32)

        row_prefixes = sorted({n_slc * k // 4 for k in (1, 2, 3, 4) if (n_slc * k // 4) % 8 == 0})

        def rank_pick(cands):
            if len(cands) == 1:
                rank_prefix(cands[0])
            else:
                lax.cond(2 * i + 2 <= cands[0], lambda: rank_prefix(cands[0]), lambda: rank_pick(cands[1:]))

        rank_pick(row_prefixes)

        init = (jnp.full((1, hq), NEG, F32), jnp.zeros((1, hq), F32), jnp.zeros((HEAD_DIM, hq), F32))

        def scores(k_ref, row0, n_rows):
            return jnp.dot(k_ref[0, pl.ds(pl.multiple_of(row0, KEY_TILE), n_rows), :], qr_p,
                           preferred_element_type=F32)

        def softmax_tile(s, bias, m, l):
            ps, alphas, ms, ls = [], [], [], []
            for h in range(NSA_HPG):
                hs = slice(h * Q_BLOCK, (h + 1) * Q_BLOCK)
                sh = s[:, hs] + bias
                m_new = jnp.maximum(m[:, hs], jnp.max(sh, axis=0, keepdims=True))
                alpha = jnp.exp2(m[:, hs] - m_new)
                p = jnp.exp2(sh - m_new)
                ms.append(m_new)
                ls.append(alpha * l[:, hs] + jnp.sum(p, axis=0, keepdims=True))
                ps.append(p.astype(BF16))
                alphas.append(alpha)
            cat = lambda parts: jnp.concatenate(parts, axis=1)
            return cat(ms), cat(ls), cat(alphas), cat(ps)

        def values_t(vt_ref, t):
            return vt_ref[0, g * HEAD_DIM:(g + 1) * HEAD_DIM, pl.ds(pl.multiple_of(t * KEY_TILE, KEY_TILE), KEY_TILE)]

        n_pairs = (q0 + Q_BLOCK + 2 * KEY_TILE - 1) // (2 * KEY_TILE)
        last = 2 * n_pairs - 1

        def slc_tile(u, slot):
            s_next = scores(ks_ref, jnp.minimum(u + 1, last) * KEY_TILE, KEY_TILE)
            pv = jnp.dot(values_t(vst_ref, jnp.maximum(u - 1, 0)), p_ref[1 - slot], preferred_element_type=F32)
            acc_ref[...] = acc_ref[...] * ml_ref[2:3, :] + jnp.where(u > 0, pv, 0.0)
            rows = [jnp.broadcast_to(sel_ref[pl.ds(bpt * u + jj, 1), :], (SLC_BLOCK, Q_BLOCK)) for jj in range(bpt)]
            bias = jnp.where(u * KEY_TILE + sub_k <= qp_k, jnp.concatenate(rows, axis=0), NEG)
            m, l, alpha, p = softmax_tile(s_ref[slot], bias, ml_ref[0:1, :], ml_ref[1:2, :])
            ml_ref[0:1, :], ml_ref[1:2, :], ml_ref[2:3, :] = m, l, alpha
            p_ref[slot] = p
            s_ref[1 - slot] = s_next

        def slc_step(k, carry):
            slc_tile(2 * k, 0)
            slc_tile(2 * k + 1, 1)
            return carry

        ml_ref[0:1, :], ml_ref[1:2, :], acc_ref[...] = init
        ml_ref[2:3, :] = jnp.ones((1, hq), F32)
        s_ref[0] = scores(ks_ref, 0, KEY_TILE)
        lax.fori_loop(0, n_pairs, slc_step, 0)
        pv = jnp.dot(values_t(vst_ref, last), p_ref[1], preferred_element_type=F32)
        o_slc = (acc_ref[...] * ml_ref[2:3, :] + pv) / ml_ref[1:2, :]

        w_rows = WINDOW + Q_BLOCK
        w0 = pl.multiple_of(jnp.maximum(q0 - WINDOW, 0), Q_BLOCK)
        kp = w0 + lax.broadcasted_iota(jnp.int32, (w_rows, Q_BLOCK), 0)
        qp_w = q0 + lax.broadcasted_iota(jnp.int32, (w_rows, Q_BLOCK), 1)
        bias_w = jnp.where(jnp.where(kp <= qp_w, qp_w - kp, WINDOW) < WINDOW, 0.0, NEG)
        s_w = jnp.dot(kw_ref[0, pl.ds(w0, w_rows), :], qr_p, preferred_element_type=F32)
        _, l_w, _, p_w = softmax_tile(s_w, bias_w, init[0], init[1])
        o_win = jnp.dot(vwt_ref[0, g * HEAD_DIM:(g + 1) * HEAD_DIM, pl.ds(w0, w_rows)], p_w,
                        preferred_element_type=F32) / l_w

        def gate(jj):
            return jnp.concatenate([gs[(g * NSA_HPG + h) * 3 + jj:(g * NSA_HPG + h) * 3 + jj + 1, :]
                                    for h in range(NSA_HPG)], axis=1)

        o_t = gate(0) * o_cmp + gate(1) * o_slc + gate(2) * o_win
        ya = jnp.concatenate([o_t[:, 0:128], o_t[:, 128:256]], axis=0).T
        yb = jnp.concatenate([o_t[:, 256:384], o_t[:, 384:512]], axis=0).T
        o_ref[0, :, g * 256:g * 256 + 128] = ya
        o_ref[0, :, g * 256 + 128:g * 256 + 256] = yb


def _nsa_prompt(proj_a, gate_t, cos_t, sin_t, kc, vct, ksb, vst, kwb, vwt, b, t):
    nb = t // Q_BLOCK
    n_sub = t // CMP_STRIDE
    n_slc = t // SLC_BLOCK
    mt = jnp.asarray(_slc_cmp_matrix(n_slc, n_sub - 1, n_sub)).astype(BF16)
    seq = lambda w: pl.BlockSpec((1, t, w), lambda bi, i: (bi, 0, 0))
    seq_t = pl.BlockSpec((1, LANE, t), lambda bi, i: (bi, 0, 0))
    return pl.pallas_call(
        functools.partial(_nsa_body, n_sub, n_slc),
        grid=(b, nb),
        in_specs=[pl.BlockSpec((1, Q_BLOCK, NSA_WIDTH), lambda bi, i: (bi, i, A_Q // NSA_WIDTH)),
                  pl.BlockSpec((1, 3 * NSA_HEADS, Q_BLOCK), lambda bi, i: (bi, 0, i)),
                  pl.BlockSpec((ROT_DIM // 2, Q_BLOCK), lambda bi, i: (0, i)),
                  pl.BlockSpec((ROT_DIM // 2, Q_BLOCK), lambda bi, i: (0, i)),
                  pl.BlockSpec((1, n_sub, LANE), lambda bi, i: (bi, 0, 0)),
                  pl.BlockSpec((1, LANE, n_sub), lambda bi, i: (bi, 0, 0)),
                  seq(LANE), seq_t, seq(LANE), seq_t,
                  pl.BlockSpec((n_slc, n_sub), lambda bi, i: (0, 0))],
        out_specs=pl.BlockSpec((1, Q_BLOCK, NSA_WIDTH), lambda bi, i: (bi, i, 0)),
        out_shape=jax.ShapeDtypeStruct((b, t, NSA_WIDTH), F32),
        scratch_shapes=[pltpu.VMEM((n_slc, Q_BLOCK), F32), pltpu.VMEM((n_slc, Q_BLOCK), F32),
                        pltpu.VMEM((8, NSA_HPG * Q_BLOCK), F32), pltpu.VMEM((HEAD_DIM, NSA_HPG * Q_BLOCK), F32),
                        pltpu.VMEM((2, KEY_TILE, NSA_HPG * Q_BLOCK), F32),
                        pltpu.VMEM((2, KEY_TILE, NSA_HPG * Q_BLOCK), BF16)],
        compiler_params=_cparams(("arbitrary", "arbitrary")),
        name="nsa_prompt",
    )(proj_a.reshape(b, t, A_WIDTH), gate_t, cos_t, sin_t, kc, vct,
      ksb.reshape(b, t, LANE), vst, kwb.reshape(b, t, LANE), vwt, mt)


def _rope_rows(x, pos):
    half = ROT_DIM // 2
    inv = ROPE_THETA ** (-jnp.arange(0, ROT_DIM, 2, dtype=F32) / ROT_DIM)
    ang = pos.astype(F32)[:, None] * inv[None, :]
    cos = jnp.cos(ang)[None, :, None, :]
    sin = jnp.sin(ang)[None, :, None, :]
    x1, x2 = x[..., :half], x[..., half:ROT_DIM]
    return jnp.concatenate([x1 * cos - x2 * sin, x2 * cos + x1 * sin, x[..., ROT_DIM:]], axis=-1)


N_PAGES = PAST_LEN // PAGE_SIZE
DEC_ROWS = 8
DEC_QROWS = NSA_HEADS * 4


DEC_SEQS = 2


def _nsa_dec_body(dec_t, pt_ref, *refs):
    del pt_ref
    for sq in range(DEC_SEQS):
        _nsa_dec_one(dec_t, sq, refs[sq * N_PAGES:(sq + 1) * N_PAGES], *refs[DEC_SEQS * N_PAGES:])


def _nsa_dec_one(dec_t, sq, pages, kvn_ref, cw_ref, wn_ref, qn_ref, qr_ref, g_ref, w_ref, b_ref, mt_ref, o_ref,
                 ck_ref, cv_ref):
    qn, qr = qn_ref[sq], qr_ref[sq]
    nq = qn.shape[0]
    lane = lax.broadcasted_iota(jnp.int32, (nq, LANE), 1)
    qi = lax.broadcasted_iota(jnp.int32, (nq, LANE), 0) % dec_t
    qpos = PAST_LEN + qi
    n_sub = PAST_LEN // CMP_STRIDE

    for p in range(N_PAGES):
        ck_ref[sq, p * PAGE_SIZE:(p + 1) * PAGE_SIZE, :] = pages[p][0, 0, :, 0:128]
        cv_ref[sq, p * PAGE_SIZE:(p + 1) * PAGE_SIZE, :] = pages[p][0, 0, :, 128:256]
    parts = []
    for j in range(CMP_STRIDE):
        parts += [ck_ref[sq, pl.ds(j, n_sub, stride=CMP_STRIDE), :], cv_ref[sq, pl.ds(j, n_sub, stride=CMP_STRIDE), :]]
    r = _mm(jnp.concatenate(parts, axis=1), w_ref[...])
    bias = b_ref[...]
    kc = r[:, 0:128] + pltpu.roll(r[:, 128:256], n_sub - 1, 0) + bias[:, 0:128]
    vc = r[:, 256:384] + pltpu.roll(r[:, 384:512], n_sub - 1, 0) + bias[:, 128:256]

    def softmax_rows(scores, oks):
        scores = [jnp.where(ok, s, NEG) for s, ok in zip(scores, oks)]
        m = functools.reduce(jnp.maximum, [jnp.max(s, axis=1, keepdims=True) for s in scores])
        es = [jnp.where(ok, jnp.exp(s - m), 0.0) for s, ok in zip(scores, oks)]
        d = jnp.maximum(functools.reduce(jnp.add, [jnp.sum(e, axis=1, keepdims=True) for e in es]), 1e-30)
        return [e / d for e in es]

    cmp_ok = (CMP_STRIDE * lane + CMP_BLOCK - 1) <= qpos
    (p1,) = softmax_rows([_mm_nt(qn, kc)], [cmp_ok])
    o_cmp = _mm(p1, vc)
    gq = NSA_KV * dec_t
    imp = functools.reduce(jnp.add, [p1[h * gq:(h + 1) * gq] for h in range(NSA_HPG)])
    imps = _mm_hi(imp, mt_ref[...])
    blk = lane[0:gq]
    qp8 = qpos[0:gq]
    qblk = qp8 // SLC_BLOCK
    causal = blk * SLC_BLOCK <= qp8
    forced = (blk == 0) | (blk == qblk) | (blk == qblk - 1)
    score = jnp.where(causal, jnp.where(forced, jnp.inf, imps), -jnp.inf)
    n_slc = -(-(PAST_LEN + dec_t) // SLC_BLOCK)
    cnt = jnp.zeros((gq, LANE), F32)
    for j in range(n_slc):
        col = score[:, j:j + 1]
        cnt = cnt + jnp.where(blk > j, jnp.where(col >= score, 1.0, 0.0), jnp.where(col > score, 1.0, 0.0))
    sel = jnp.where(cnt < N_SELECT, jnp.where(score > -jnp.inf, 1.0, 0.0), 0.0)

    def sel_rows(j0):
        pick = jnp.where(lane[0:gq] < SLC_BLOCK, sel[:, j0:j0 + 1], sel[:, j0 + 1:j0 + 2])
        return jnp.concatenate([pick] * NSA_HPG, axis=0) > 0.5

    zeros_k = jnp.zeros((PAGE_SIZE - DEC_ROWS, LANE), F32)
    new_ok = (lane < dec_t) & (lane <= qi)
    kvn = kvn_ref[sq]
    s_list = [_mm_nt(qr, pages[p][0, 0, :, 256:384]) for p in range(N_PAGES)]
    s_list.append(_mm_nt(qr, jnp.concatenate([kvn[:, 256:384], zeros_k], axis=0)))
    ok_list = [sel_rows(2 * p) for p in range(N_PAGES)]
    own = jnp.concatenate([sel[:, 2 * N_PAGES:2 * N_PAGES + 1]] * NSA_HPG, axis=0) > 0.5
    ok_list.append(new_ok & own)
    p_list = softmax_rows(s_list, ok_list)
    o_slc = functools.reduce(jnp.add, [_mm(p_list[p], pages[p][0, 0, :, 384:512]) for p in range(N_PAGES)])
    o_slc = o_slc + _mm(p_list[N_PAGES], jnp.concatenate([kvn[:, 384:512], zeros_k], axis=0))

    n_wt = WINDOW // PAGE_SIZE
    wn = wn_ref[sq]
    s_list, ok_list = [], []
    for t in range(n_wt):
        s_list.append(_mm_nt(qr, cw_ref[0, sq, t * PAGE_SIZE:(t + 1) * PAGE_SIZE, 0:128]))
        ok_list.append(qpos - (PAST_LEN - WINDOW + t * PAGE_SIZE + lane) < WINDOW)
    s_list.append(_mm_nt(qr, jnp.concatenate([wn[:, 0:128], zeros_k], axis=0)))
    ok_list.append(new_ok)
    p_list = softmax_rows(s_list, ok_list)
    o_win = functools.reduce(jnp.add, [_mm(p_list[t], cw_ref[0, sq, t * PAGE_SIZE:(t + 1) * PAGE_SIZE, 128:256])
                                       for t in range(n_wt)])
    o_win = o_win + _mm(p_list[n_wt], jnp.concatenate([wn[:, 128:256], zeros_k], axis=0))
    g = g_ref[sq]
    o_ref[sq] = g[:, 0:128] * o_cmp + g[:, 128:256] * o_slc + g[:, 256:384] * o_win


def _nsa_decode_pallas(l, q, gates, cache_kv, page_table, kv_new, cache_win, win_new, pe, cw):
    b, tq = q.shape[:2]
    scale = HEAD_DIM ** -0.5
    eye = jnp.eye(NSA_KV, dtype=F32)

    def rows(x):
        x5 = x.reshape(b, tq, NSA_KV, NSA_HPG, HEAD_DIM).transpose(0, 3, 2, 1, 4)
        return jnp.einsum('bhgqd,gk->bhgqkd', x5, eye).reshape(b, DEC_QROWS, LANE).astype(BF16)

    qn = rows(q * scale)
    qr = rows(_rope_rows(q, PAST_LEN + jnp.arange(tq)) * scale)
    g5 = gates.reshape(b, tq, NSA_KV, NSA_HPG, 3).transpose(0, 3, 2, 1, 4)
    lane_g = jnp.repeat(eye, HEAD_DIM, axis=1)
    gate_b = jnp.einsum('bhgqj,gn->bhgqjn', g5, lane_g).reshape(b, DEC_QROWS, 3 * LANE)
    w_all, bias = _cmp_weights(pe, cw)
    n_cmp = PAST_LEN // CMP_STRIDE - 1
    n_slc = -(-(PAST_LEN + tq) // SLC_BLOCK)
    mt = np.zeros((LANE, LANE), np.float32)
    mt[:n_cmp, :n_slc] = _slc_cmp_matrix(n_slc, n_cmp, n_cmp).T
    pad_rows = lambda a: jnp.pad(a, ((0, 0), (0, DEC_ROWS - tq), (0, 0)))
    n_pool = cache_kv.shape[1]
    ckv = cache_kv.reshape(DEPTH, n_pool, PAGE_SIZE, 512)
    page_spec = lambda sq, p: pl.BlockSpec((1, 1, PAGE_SIZE, 512),
                                           lambda bi, pt: (l, pt[bi * DEC_SEQS + sq, p], 0, 0))
    per_seq = lambda r, w: pl.BlockSpec((DEC_SEQS, r, w), lambda bi, pt: (bi, 0, 0))
    const = lambda shape: pl.BlockSpec(shape, lambda bi, pt: (0,) * len(shape))
    grid_spec = pltpu.PrefetchScalarGridSpec(
        num_scalar_prefetch=1,
        grid=(b // DEC_SEQS,),
        in_specs=[page_spec(sq, p) for sq in range(DEC_SEQS) for p in range(N_PAGES)]
                 + [per_seq(DEC_ROWS, 512),
                    pl.BlockSpec((1, DEC_SEQS, WINDOW, 256), lambda bi, pt: (l, bi, 0, 0)),
                    per_seq(DEC_ROWS, 256), per_seq(DEC_QROWS, LANE), per_seq(DEC_QROWS, LANE),
                    per_seq(DEC_QROWS, 3 * LANE), const((CMP_STRIDE * 256, 512)), const((1, 256)), const((LANE, LANE))],
        out_specs=per_seq(DEC_QROWS, LANE),
        scratch_shapes=[pltpu.VMEM((DEC_SEQS, PAST_LEN, LANE), F32), pltpu.VMEM((DEC_SEQS, PAST_LEN, LANE), F32)],
    )
    o = pl.pallas_call(
        functools.partial(_nsa_dec_body, tq),
        grid_spec=grid_spec,
        out_shape=jax.ShapeDtypeStruct((b, DEC_QROWS, LANE), F32),
        compiler_params=_cparams(("arbitrary",)),
        name="nsa_decode",
    )(page_table, *([ckv] * (DEC_SEQS * N_PAGES)), pad_rows(kv_new), cache_win.reshape(DEPTH, b, WINDOW, 256), pad_rows(win_new),
      qn, qr, gate_b, w_all, bias, jnp.asarray(mt))
    o6 = o.reshape(b, NSA_HPG, NSA_KV, tq, NSA_KV, HEAD_DIM)
    return jnp.einsum('bhgqkd,gk->bqghd', o6, eye).reshape(b, tq, NSA_WIDTH)


def _mm(a, b):
    return jnp.dot(a.astype(BF16), b.astype(BF16), preferred_element_type=F32)


def _mm_nt(a, b):
    return lax.dot_general(a.astype(BF16), b.astype(BF16), (((1,), (1,)), ((), ())), preferred_element_type=F32)


def _mm_tn(a, b):
    return lax.dot_general(a.astype(BF16), b.astype(BF16), (((0,), (0,)), ((), ())), preferred_element_type=F32)


def _mm_hi(a, b):
    return jnp.dot(a, b, preferred_element_type=F32, precision=HI)


def _split3(x):
    hi = x.astype(BF16)
    r1 = x - hi.astype(F32)
    mid = r1.astype(BF16)
    return hi, mid, (r1 - mid.astype(F32)).astype(BF16)


def _cumsum_rows(tri_bf16, x):
    return functools.reduce(jnp.add, [jnp.dot(tri_bf16, piece, preferred_element_type=F32) for piece in _split3(x)])


def _mm_x3(a, b):
    a_hi, b_hi = a.astype(BF16), b.astype(BF16)
    a_lo, b_lo = (a - a_hi.astype(F32)).astype(BF16), (b - b_hi.astype(F32)).astype(BF16)
    dot = functools.partial(jnp.dot, preferred_element_type=F32)
    return dot(a_hi, b_hi) + dot(a_hi, b_lo) + dot(a_lo, b_hi)


def _mm_ones_rhs(ones_bf16, b):
    hi = b.astype(BF16)
    lo = (b - hi.astype(F32)).astype(BF16)
    return (jnp.dot(ones_bf16, hi, preferred_element_type=F32) + jnp.dot(ones_bf16, lo, preferred_element_type=F32))


def _mm_ones(a, ones_bf16):
    hi = a.astype(BF16)
    lo = (a - hi.astype(F32)).astype(BF16)
    return (jnp.dot(hi, ones_bf16, preferred_element_type=F32) + jnp.dot(lo, ones_bf16, preferred_element_type=F32))


def _log_sigmoid(x):
    return jnp.minimum(x, 0.0) - jnp.log1p(jnp.exp(-jnp.abs(x)))


DEC_PAD = 16
HGRN_SUB = 16
MIX_TILE = 512


def _hgrn_body(n_t, c, sc, n_valid, q_ref, f_ref, i_ref, z_ref, s0_ref, lbc_ref, lnw_ref, y_ref, s_ref, st_ref, o_ref):
    ti = pl.program_id(1)
    tt = o_ref.shape[0]
    heads = range(HGRN_HEADS)
    hsl = [slice(h * HGRN_HEAD, (h + 1) * HGRN_HEAD) for h in heads]

    @pl.when(ti == 0)
    def _():
        for h in heads:
            st_ref[h] = s0_ref[0, h].T

    log_lb, log_1m_lb, one_m_lb = lbc_ref[0:1, :], lbc_ref[1:2, :], lbc_ref[2:3, :]
    tri = (lax.broadcasted_iota(jnp.int32, (c, c), 0) >= lax.broadcasted_iota(jnp.int32, (c, c), 1)).astype(BF16)
    sub_s = lax.broadcasted_iota(jnp.int32, (sc, HGRN_WIDTH), 0)

    def chunk(ci, carry):
        r0 = pl.multiple_of(ci * c, c)
        q = q_ref[0, pl.ds(r0, c), :]
        fz = f_ref[0, pl.ds(r0, c), :]
        iv = i_ref[0, pl.ds(r0, c), :]
        b_ = log_1m_lb + _log_sigmoid(fz)
        lf = jnp.maximum(log_lb, b_) + jnp.log1p(jnp.exp(-jnp.abs(log_lb - b_)))
        kf = one_m_lb * jax.nn.sigmoid(-fz)
        if n_valid < tt:
            pad_row = lax.broadcasted_iota(jnp.int32, (c, 1), 0) + r0 >= n_valid
            lf, kf = jnp.where(pad_row, 0.0, lf), jnp.where(pad_row, 0.0, kf)
        cl = _cumsum_rows(tri, lf)
        c_end = cl[c - 1:c]
        q_dec = q * jnp.exp(cl)
        k_end = kf * jnp.exp(c_end - cl)
        w_end = jnp.exp(c_end)
        st = [st_ref[h] for h in heads]
        o = [_mm_nt(q_dec[:, hsl[h]], st[h]) for h in heads]
        upd = [_mm_tn(iv[:, hsl[h]], k_end[:, hsl[h]]) for h in heads]
        for h in heads:
            st_ref[h] = st[h] * w_end[:, hsl[h]] + upd[h]
        pieces = []
        for blk in range(c // sc):
            lo = blk * sc
            cl_r, q_r, k_r, i_r = cl[lo:lo + sc], q[lo:lo + sc], kf[lo:lo + sc], iv[lo:lo + sc]
            acc = [o[h][lo:lo + sc] for h in heads]
            if blk > 0:
                bnd = cl[lo - 1:lo]
                q_b = q_r * jnp.exp(cl_r - bnd)
                k_b = kf[:lo] * jnp.exp(bnd - cl[:lo])
                att = [_mm_nt(q_b[:, hsl[h]], k_b[:, hsl[h]]) for h in heads]
                acc = [acc[h] + _mm(att[h], iv[:lo, hsl[h]]) for h in heads]
            rows = []
            for t in range(sc):
                w = jnp.where(sub_s <= t, jnp.exp(cl_r[t:t + 1] - cl_r), 0.0)
                e = q_r[t:t + 1] * w * k_r
                d = jnp.concatenate([jnp.broadcast_to(jnp.sum(e[:, hsl[h]], axis=1, keepdims=True), (sc, HGRN_HEAD))
                                     for h in heads], axis=1)
                rows.append(jnp.sum(d * i_r, axis=0, keepdims=True))
            pieces.append(jnp.concatenate(acc, axis=1) + jnp.concatenate(rows, axis=0))
        o_ref[pl.ds(r0, c), :] = jnp.concatenate(pieces, axis=0)
        return carry

    lax.fori_loop(0, o_ref.shape[0] // c, chunk, 0)
    z = z_ref[0]
    gate = lnw_ref[...] * (z * jax.nn.sigmoid(z))
    for h in heads:
        o = o_ref[:, hsl[h]]
        y_ref[0, :, hsl[h]] = o * lax.rsqrt(jnp.mean(o * o, axis=-1, keepdims=True) + RMS_EPS) * gate[:, hsl[h]]

    @pl.when(ti == n_t - 1)
    def _():
        for h in heads:
            s_ref[0, h] = st_ref[h].T


def _hgrn_call(arrs, blks, s0, n_valid, c, sc, lb, ln_w):
    b, t = arrs[0].shape[:2]
    tt = min(MIX_TILE, t)
    n_t = t // tt
    lbc = jnp.concatenate([jnp.log(lb)[None], jnp.log1p(-lb)[None], (1.0 - lb)[None],
                           jnp.zeros((5, HGRN_WIDTH), F32)], axis=0)
    col = lambda blk: pl.BlockSpec((1, tt, HGRN_WIDTH), lambda bi, i: (bi, i, blk))
    return pl.pallas_call(
        functools.partial(_hgrn_body, n_t, c, sc, n_valid),
        grid=(b, n_t),
        in_specs=[col(blks[0]), col(blks[1]), col(blks[2]), col(blks[3]),
                  pl.BlockSpec((1, HGRN_HEADS, HGRN_HEAD, HGRN_HEAD), lambda bi, i: (bi, 0, 0, 0)),
                  pl.BlockSpec((8, HGRN_WIDTH), lambda bi, i: (0, 0)),
                  pl.BlockSpec((1, HGRN_WIDTH), lambda bi, i: (0, 0))],
        out_specs=[pl.BlockSpec((1, tt, HGRN_WIDTH), lambda bi, i: (bi, i, 0)),
                   pl.BlockSpec((1, HGRN_HEADS, HGRN_HEAD, HGRN_HEAD), lambda bi, i: (bi, 0, 0, 0))],
        out_shape=[jax.ShapeDtypeStruct((b, t, HGRN_WIDTH), F32),
                   jax.ShapeDtypeStruct((b, HGRN_HEADS, HGRN_HEAD, HGRN_HEAD), F32)],
        scratch_shapes=[pltpu.VMEM((HGRN_HEADS, HGRN_HEAD, HGRN_HEAD), F32), pltpu.VMEM((tt, HGRN_WIDTH), F32)],
        compiler_params=_cparams(("arbitrary", "arbitrary")),
        name="hgrn_prompt",
    )(*arrs, s0, lbc, ln_w.reshape(1, HGRN_WIDTH))


def _rwkv_body(n_t, c, n_valid, cols_ref, z_ref, prev0_ref, s0_ref, mu_ref, w0_ref, wup_ref, a0_ref, aup_ref, kk_ref,
               ka_ref, rk_ref, lnw_ref, bd_ref, y_ref, s_ref, prev_ref, st_ref, r_s, k_s, v_s, lw_s, kk_s, a_s, o_s):
    ti = pl.program_id(1)
    tt = cols_ref.shape[1]
    n_pair = RWKV_HEADS // 2

    @pl.when(ti == 0)
    def _():
        prev_ref[0:1, :] = prev0_ref[0]
        st_ref[...] = s0_ref[0]

    x = cols_ref[0]
    first = lax.broadcasted_iota(jnp.int32, (tt, 1), 0) == 0
    x_prev = jnp.where(first, prev_ref[0:1, :], pltpu.roll(x, 1, 0))
    prev_ref[0:1, :] = x[tt - 1:tt, :]
    xx = x + (x_prev - x) * mu_ref[...]
    r, k, v = xx[:, 0:RWKV_WIDTH], xx[:, RWKV_WIDTH:2 * RWKV_WIDTH], xx[:, 2 * RWKV_WIDTH:3 * RWKV_WIDTH]
    lora = xx[:, 3 * RWKV_WIDTH:]
    nx = -(w0_ref[...] + _mm_x3(jnp.tanh(lora), wup_ref[...]))
    w_log = -(jnp.maximum(nx, 0.0) + jnp.log1p(jnp.exp(-jnp.abs(nx)))) - 0.5
    lw_s[...] = -jnp.exp(w_log)
    a = jax.nn.sigmoid(a0_ref[...] + _mm_x3(lora, aup_ref[...]))
    bd = bd_ref[...]
    kk_raw = k * kk_ref[...]
    kk_s[...] = kk_raw / jnp.maximum(jnp.sqrt(_mm_ones(kk_raw * kk_raw, bd)), 1e-12)
    k2 = k * (1.0 + (a - 1.0) * ka_ref[...])
    bonus = _mm_ones(r * k2 * rk_ref[...], bd) * v
    if n_valid < tt:
        pad_row = lax.broadcasted_iota(jnp.int32, (tt, 1), 0) >= n_valid
        lw_s[...] = jnp.where(pad_row, 0.0, lw_s[...])
        kk_s[...] = jnp.where(pad_row, 0.0, kk_s[...])
        r, k2, v = (jnp.where(pad_row, 0.0, u) for u in (r, k2, v))
    r_s[...] = r
    k_s[...] = k2
    v_s[...] = v
    a_s[...] = a

    w2 = 2 * c
    rows = lax.broadcasted_iota(jnp.int32, (c, w2), 0)
    lane_in = lax.broadcasted_iota(jnp.int32, (c, w2), 1) % c
    strict = rows > lane_in
    incl = rows >= lane_in
    eye2 = (rows == lane_in).astype(F32)
    lane_s = lax.broadcasted_iota(jnp.int32, (1, w2), 1)

    def blockdiag_s(zz):
        zz = zz.astype(BF16)
        return jnp.concatenate([jnp.where(lane_s < c, zz, 0), jnp.where(lane_s >= c, zz, 0)], axis=0)

    tri = (lax.broadcasted_iota(jnp.int32, (c, c), 0) >= lax.broadcasted_iota(jnp.int32, (c, c), 1)).astype(BF16)
    lane1 = lax.broadcasted_iota(jnp.int32, (1, LANE), 1)
    m0, m1 = lane1 < RWKV_HEAD, lane1 >= RWKV_HEAD
    bdm = (lax.broadcasted_iota(jnp.int32, (LANE, LANE), 0) // RWKV_HEAD
           == lax.broadcasted_iota(jnp.int32, (LANE, LANE), 1) // RWKV_HEAD)

    def blockdiag(zz):
        zz = zz.astype(BF16)
        return jnp.concatenate([jnp.where(m0, zz, 0), jnp.where(m1, zz, 0)], axis=0)

    def chunk(ci, carry):
        r0 = pl.multiple_of(ci * c, c)
        ds = pl.ds(r0, c)
        lw = lw_s[ds, :]
        cum = _cumsum_rows(tri, lw)
        w_in, w_ex, inv_in = jnp.exp(cum), jnp.exp(cum - lw), jnp.exp(-cum)
        w_end = jnp.exp(cum[c - 1:c])
        kk_c, vv = kk_s[ds, :], v_s[ds, :]
        alpha = -kk_c * w_ex
        beta_h = kk_c * a_s[ds, :] * inv_in
        k_h = k_s[ds, :] * inv_in
        r_t = r_s[ds, :] * w_in
        beta_e, k_e = beta_h * w_end, k_h * w_end
        pairs = range(n_pair)
        sls = [slice(p * LANE, (p + 1) * LANE) for p in pairs]
        al = [alpha[:, sl] for sl in sls]
        rt = [r_t[:, sl] for sl in sls]
        vb = [blockdiag(vv[:, sl]) for sl in sls]
        aa = [_mm_nt(jnp.concatenate([al[p], rt[p]], axis=0),
                     jnp.concatenate([blockdiag(beta_h[:, sls[p]]), blockdiag(k_h[:, sls[p]])], axis=0))
              for p in pairs]
        a_ab = [jnp.where(strict, aa[p][0:c, 0:w2], 0.0) for p in pairs]
        a_ak = [jnp.where(strict, aa[p][0:c, w2:], 0.0) for p in pairs]
        a_rb = [jnp.where(incl, aa[p][c:, 0:w2], 0.0) for p in pairs]
        a_rk = [jnp.where(incl, aa[p][c:, w2:], 0.0) for p in pairs]
        akv = [_mm(a_ak[p], vb[p]) for p in pairs]
        rkv = [_mm(a_rk[p], vb[p]) for p in pairs]
        vk = [_mm_tn(vv[:, sls[p]], k_e[:, sls[p]]) for p in pairs]
        tm = [eye2 + a_ab[p] for p in pairs]
        pw = a_ab
        n = 1
        while 2 * n < c:
            pw = [_mm(pw[p], blockdiag_s(pw[p])) for p in pairs]
            tm = [tm[p] + _mm(tm[p], blockdiag_s(pw[p])) for p in pairs]
            n *= 2
        pq = [_mm(tm[p], jnp.concatenate([blockdiag(al[p]), blockdiag(akv[p])], axis=1)) for p in pairs]
        ro = [_mm(a_rb[p], jnp.concatenate([blockdiag(pq[p][:, 0:LANE]), blockdiag(pq[p][:, LANE:])], axis=1))
              for p in pairs]
        mn = [_mm_tn(pq[p], beta_e[:, sls[p]]) for p in pairs]
        outs = []
        for p in pairs:
            m_bd = jnp.where(bdm, mn[p][0:LANE], 0.0)
            n_f = jnp.where(bdm, mn[p][LANE:] + vk[p], 0.0)
            st = st_ref[p]
            outs.append(_mm_nt(rt[p] + ro[p][:, 0:LANE], blockdiag(st)) + ro[p][:, LANE:] + rkv[p])
            st_ref[p] = st * w_end[:, sls[p]] + _mm(st, m_bd) + n_f[0:RWKV_HEAD] + n_f[RWKV_HEAD:]
        o_s[ds, :] = jnp.concatenate(outs, axis=1)
        return carry

    lax.fori_loop(0, tt // c, chunk, 0)
    o = o_s[...]
    inv_n = 1.0 / RWKV_HEAD
    d = o - _mm_ones(o, bd) * inv_n
    var = _mm_ones(d * d, bd) * inv_n
    z = z_ref[0]
    y_ref[0] = (d * lax.rsqrt(var + GN_EPS) * lnw_ref[...] + bonus) * (z * jax.nn.sigmoid(z))

    @pl.when(ti == n_t - 1)
    def _():
        s_ref[0] = st_ref[...]


def _rwkv_call(rcols, z_arr, z_blk, prev0, s0, n_valid, c, mu, w0, w_up, a0, a_up, k_k, k_a, r_k, ln_w):
    b, t = rcols.shape[:2]
    tt = min(MIX_TILE, t)
    n_t = t // tt
    n_pair = RWKV_HEADS // 2
    s0p = s0.reshape(b, n_pair, 2, RWKV_HEAD, RWKV_HEAD).transpose(0, 1, 3, 2, 4).reshape(b, n_pair, RWKV_HEAD, LANE)
    zpad = jnp.zeros((RWKV_LORA_W, RWKV_WIDTH), F32)
    wup = jnp.concatenate([w_up, zpad], axis=0)
    aup = jnp.concatenate([zpad, a_up], axis=0)
    hid = np.arange(RWKV_WIDTH) // RWKV_HEAD
    bd = jnp.asarray((hid[:, None] == hid[None, :]).astype(np.float32)).astype(BF16)
    vec = lambda a: a.reshape(1, -1)
    full = lambda shape: pl.BlockSpec(shape, lambda bi, i: (0,) * len(shape))
    y, s = pl.pallas_call(
        functools.partial(_rwkv_body, n_t, c, n_valid),
        grid=(b, n_t),
        in_specs=[pl.BlockSpec((1, tt, SHIFT_W), lambda bi, i: (bi, i, 0)),
                  pl.BlockSpec((1, tt, RWKV_WIDTH), lambda bi, i: (bi, i, z_blk)),
                  pl.BlockSpec((1, 1, SHIFT_W), lambda bi, i: (bi, 0, 0)),
                  pl.BlockSpec((1, n_pair, RWKV_HEAD, LANE), lambda bi, i: (bi, 0, 0, 0)),
                  full((1, SHIFT_W)), full((1, RWKV_WIDTH)), full((LANE, RWKV_WIDTH)), full((1, RWKV_WIDTH)),
                  full((LANE, RWKV_WIDTH)), full((1, RWKV_WIDTH)), full((1, RWKV_WIDTH)), full((1, RWKV_WIDTH)),
                  full((1, RWKV_WIDTH)), full((RWKV_WIDTH, RWKV_WIDTH))],
        out_specs=[pl.BlockSpec((1, tt, RWKV_WIDTH), lambda bi, i: (bi, i, 0)),
                   pl.BlockSpec((1, n_pair, RWKV_HEAD, LANE), lambda bi, i: (bi, 0, 0, 0))],
        out_shape=[jax.ShapeDtypeStruct((b, t, RWKV_WIDTH), F32),
                   jax.ShapeDtypeStruct((b, n_pair, RWKV_HEAD, LANE), F32)],
        scratch_shapes=[pltpu.VMEM((8, SHIFT_W), F32), pltpu.VMEM((n_pair, RWKV_HEAD, LANE), F32)]
                       + [pltpu.VMEM((tt, RWKV_WIDTH), F32)] * 7,
        compiler_params=_cparams(("arbitrary", "arbitrary")),
        name="rwkv_prompt",
    )(rcols, z_arr, prev0.reshape(b, 1, SHIFT_W), s0p, vec(mu), vec(w0), wup, vec(a0), aup, vec(k_k), vec(k_a),
      vec(r_k), vec(ln_w), bd)
    s = s.reshape(b, n_pair, RWKV_HEAD, 2, RWKV_HEAD).transpose(0, 1, 3, 2, 4)
    return y, s.reshape(b, RWKV_HEADS, RWKV_HEAD, RWKV_HEAD)


def _merge_body(x_ref, yn_ref, nz_ref, yr_ref, yh_ref, mg_ref, wb_ref, wo_ref, o_ref):
    nz = nz_ref[...]
    branches = (yn_ref[...] * (nz * jax.nn.sigmoid(nz)), yr_ref[...], yh_ref[...])
    acc = jnp.zeros(o_ref.shape, F32)
    for n, y in enumerate(branches):
        t = jnp.dot(y.astype(BF16), wb_ref[n], preferred_element_type=F32)
        acc = acc + jax.nn.sigmoid(mg_ref[:, n * D_MODEL:(n + 1) * D_MODEL]) * t
    o_ref[...] = x_ref[...] + jnp.dot(acc.astype(BF16), wo_ref[...], preferred_element_type=F32)


def _merge(x2d, y_nsa, proj_a, y_rwkv, y_hgrn, mg, wb, wo, tm):
    m = x2d.shape[0]
    row = lambda w: pl.BlockSpec((tm, w), lambda i: (i, 0))
    return pl.pallas_call(
        _merge_body,
        grid=(m // tm,),
        in_specs=[row(D_MODEL), row(BRANCH_WIDTH),
                  pl.BlockSpec((tm, BRANCH_WIDTH), lambda i: (i, A_NZ // BRANCH_WIDTH)),
                  row(BRANCH_WIDTH), row(BRANCH_WIDTH), row(N_BRANCH * D_MODEL),
                  pl.BlockSpec((N_BRANCH, BRANCH_WIDTH, D_MODEL), lambda i: (0, 0, 0)),
                  pl.BlockSpec((D_MODEL, D_MODEL), lambda i: (0, 0))],
        out_specs=row(D_MODEL),
        out_shape=jax.ShapeDtypeStruct((m, D_MODEL), F32),
        compiler_params=_cparams(("arbitrary",)),
        name="merge",
    )(x2d, y_nsa, proj_a, y_rwkv, y_hgrn, mg, wb, wo)


def _split_w_in(w):
    o = np.concatenate([[0], np.cumsum(IN_SIZES)])
    seg = lambda n: w[:, o[n]:o[n + 1]]
    pad = jnp.zeros((w.shape[0], A_Q - A_GATE - IN_SIZES[2]), w.dtype)
    wa = jnp.concatenate([seg(1), seg(2), pad, seg(0), seg(3), seg(5), seg(6), seg(7), seg(8), seg(9)], axis=1)
    return wa.astype(BF16), seg(4).astype(BF16), seg(10).astype(BF16)


def _layer(x, l, past, prm, lb, tabs, tm):
    b, t = x.shape[:2]
    m = b * t
    x2d = x.reshape(m, D_MODEL)
    nw = prm['norm_w'][l].reshape(1, D_MODEL)
    wa, wb_cols, wc = prm['w_split'][l]
    proj_a = _inproj(x2d, nw, wa, tm, 1536)
    rcols = _inproj(x2d, nw, wb_cols, tm, SHIFT_W).reshape(b, t, SHIFT_W)
    mg = _inproj(x2d, nw, wc, tm, 1536)
    seg = lambda off, w: proj_a[:, off:off + w].reshape(b, t, w)
    c_tab, su_tab, sd_tab, cos_t, sin_t = tabs
    if past is None:
        kv_new, win_new, kcsrc, ksb, vst, kwb, vwt = _kvpost(proj_a, (c_tab, su_tab, sd_tab), b, t, min(tm, t), True)
        w_all, bias = _cmp_weights(prm['nsa_cmp_pe'][l], prm['nsa_cmp_w'][l])
        kc, vct = _compress(kcsrc, w_all, bias, b, t)
        gate_t = seg(A_GATE, 3 * NSA_HEADS).transpose(0, 2, 1)
        y_nsa = _nsa_prompt(proj_a, gate_t, cos_t, sin_t, kc, vct, ksb, vst, kwb, vwt, b, t)
        win_state = win_new.reshape(b, t, 2, NSA_KV, HEAD_DIM)[:, -min(WINDOW, t):]
    else:
        cache_kv, page_table, cache_win, s_r, prev, s_h = past
        kv_new, win_new = _kvpost(proj_a, (c_tab, su_tab, sd_tab), b, t, tm, False)
        win5 = win_new.reshape(b, t, 2, NSA_KV, HEAD_DIM)
        win_state = jnp.concatenate([cache_win[l][:, t:], win5], axis=1)
        gates = jax.nn.sigmoid(seg(A_GATE, 3 * NSA_HEADS)).reshape(b, t, NSA_HEADS, 3)
        q = seg(A_Q, NSA_WIDTH).reshape(b, t, NSA_HEADS, HEAD_DIM)
        y_nsa = _nsa_decode_pallas(l, q, gates, cache_kv, page_table, kv_new.reshape(b, t, 512), cache_win,
                                   win_new.reshape(b, t, 256), prm['nsa_cmp_pe'][l], prm['nsa_cmp_w'][l])
    rwkv_prm = (prm['rwkv_mu'][l], prm['rwkv_w0'][l], prm['rwkv_w_up'][l], prm['rwkv_a0'][l], prm['rwkv_a_up'][l],
                prm['rwkv_k_k'][l], prm['rwkv_k_a'][l], prm['rwkv_r_k'][l], prm['rwkv_ln_w'][l])
    shift_state = rcols[:, -1]
    hgrn_cols = (A_HQ, A_HF, A_HI, A_HZ)
    if past is None:
        proj_a3 = proj_a.reshape(b, t, A_WIDTH)
        y_rwkv, s_r = _rwkv_call(rcols, proj_a3, A_RZ // RWKV_WIDTH, jnp.zeros((b, SHIFT_W), F32),
                                 jnp.zeros((b, RWKV_HEADS, RWKV_HEAD, RWKV_HEAD), F32), t, RWKV_CHUNK, *rwkv_prm)
        y_hgrn, s_h = _hgrn_call([proj_a3] * 4, [off // HGRN_WIDTH for off in hgrn_cols],
                                 jnp.zeros((b, HGRN_HEADS, HGRN_HEAD, HGRN_HEAD), F32), t, HGRN_CHUNK, HGRN_SUB,
                                 lb, prm['hgrn_ln_w'][l])
    else:
        pad = lambda a: jnp.pad(a, ((0, 0), (0, DEC_PAD - t), (0, 0)))
        y_rwkv, s_r = _rwkv_call(pad(rcols), pad(seg(A_RZ, RWKV_WIDTH)), 0, prev, s_r, t, DEC_PAD, *rwkv_prm)
        y_hgrn, s_h = _hgrn_call([pad(seg(off, HGRN_WIDTH)) for off in hgrn_cols], [0] * 4, s_h, t, DEC_PAD, DEC_PAD,
                                 lb, prm['hgrn_ln_w'][l])
        y_rwkv, y_hgrn = y_rwkv[:, :t], y_hgrn[:, :t]
    x_new = _merge(x2d, y_nsa.reshape(m, NSA_WIDTH), proj_a, y_rwkv.reshape(m, RWKV_WIDTH), y_hgrn.reshape(m, HGRN_WIDTH),
                   mg, prm['w_branch_bf16'][l], prm['w_out_bf16'][l], min(tm, 512))
    kv_state = kv_new.reshape(b, t, 4, NSA_KV, HEAD_DIM)
    return x_new.reshape(b, t, D_MODEL), (kv_state, win_state, s_r, shift_state, s_h)


def kernel(x_prompt, x_sample, cache_kv, cache_win, state_rwkv, state_shift, state_hgrn, page_table,
           norm_w, w_in, nsa_cmp_pe, nsa_cmp_w, rwkv_mu, rwkv_w0, rwkv_w_up, rwkv_a0, rwkv_a_up,
           rwkv_k_k, rwkv_k_a, rwkv_r_k, rwkv_ln_w, hgrn_lb_logits, hgrn_ln_w, w_branch, w_out, norm_f):
    prm = {'norm_w': norm_w, 'nsa_cmp_pe': nsa_cmp_pe, 'nsa_cmp_w': nsa_cmp_w,
           'rwkv_mu': rwkv_mu, 'rwkv_w0': rwkv_w0, 'rwkv_w_up': rwkv_w_up, 'rwkv_a0': rwkv_a0,
           'rwkv_a_up': rwkv_a_up, 'rwkv_k_k': rwkv_k_k, 'rwkv_k_a': rwkv_k_a, 'rwkv_r_k': rwkv_r_k,
           'rwkv_ln_w': rwkv_ln_w, 'hgrn_ln_w': hgrn_ln_w,
           'w_split': [_split_w_in(w_in[l]) for l in range(DEPTH)],
           'w_branch_bf16': w_branch.astype(BF16), 'w_out_bf16': w_out.astype(BF16)}
    cs = jnp.cumsum(jax.nn.softmax(hgrn_lb_logits, axis=0), axis=0)
    lbs = cs - cs[0:1]
    n_dec, dec_t = x_sample.shape[:2]
    bp, tp = x_prompt.shape[:2]
    tabs_p = _rope_tables(jnp.arange(tp))
    tabs_s = _rope_tables(PAST_LEN + jnp.arange(n_dec * dec_t) % dec_t)
    tm_p = min(1024, bp * tp)
    tm_s = n_dec * dec_t
    xp, xs = x_prompt, x_sample
    outs = [[] for _ in range(10)]
    for l in range(DEPTH):
        xp, st_p = _layer(xp, l, None, prm, lbs[l], tabs_p, tm_p)
        past = (cache_kv, page_table, cache_win, state_rwkv[l], state_shift[l], state_hgrn[l])
        xs, st_s = _layer(xs, l, past, prm, lbs[l], tabs_s, tm_s)
        for n in range(5):
            outs[2 * n].append(st_p[n])
            outs[2 * n + 1].append(st_s[n])
    y_prompt = _final_norm(xp.reshape(bp * tp, D_MODEL), norm_f, tm_p).reshape(xp.shape)
    y_sample = _final_norm(xs.reshape(n_dec * dec_t, D_MODEL), norm_f, tm_s).reshape(xs.shape)
    return (y_prompt, y_sample) + tuple(jnp.stack(o) for o in outs)
```
